```python
import math
import jax, jax.numpy as jnp
from jax import lax
import numpy as np

D_MODEL = 1024
BATCH = 32
SEQ = 2048
DEPTH = 1

RWKV_WIDTH = 512
RWKV_HEAD = 64
RWKV_HEADS = RWKV_WIDTH // RWKV_HEAD
LORA_W = 64
LORA_A = 64
LORA_G = 128
S5_WIDTH = 512
S5_GROUP = 16
S5_GROUPS = S5_WIDTH // S5_GROUP
S5_STATE = 64
N_BRANCH = 2
D_FF = 2816
CONV_W = 3
RMS_EPS = 1e-6
GN_EPS = 64e-5
L2_EPS = 1e-12
DT_MIN = 1e-3
DT_MAX = 1e-1
SHIFT_COLS = 3 * RWKV_WIDTH + LORA_W + LORA_A + LORA_G
IN_COLS = SHIFT_COLS + S5_WIDTH + N_BRANCH * D_MODEL

kernel_name = "hybrid_rwkv7_s5_convffn_adaln"


def _rms_norm(x, g):
    xf = x.astype(jnp.float32)
    y = xf * lax.rsqrt(jnp.mean(xf * xf, axis=-1, keepdims=True) + RMS_EPS)
    return (y * g.astype(jnp.float32)).astype(x.dtype)


def _token_shift(p, mu):
    prev = jnp.pad(p, ((0, 0), (1, 0), (0, 0)))[:, :-1]
    return p + (prev - p) * mu


def _causal_dwconv(u, w, b):
    up = jnp.pad(u, ((0, 0), (CONV_W - 1, 0), (0, 0)))
    s = u.shape[1]
    y = b
    for j in range(CONV_W):
        y = y + w[j] * up[:, j:j + s]
    return y


def _wkv7(r, decay, k, v, a, b):
    bsz, _, h, n = r.shape

    def step(state, inp):
        r_t, w_t, k_t, v_t, a_t, b_t = inp
        sa = jnp.einsum('bhvk,bhk->bhv', state, a_t)
        state = (state * w_t[:, :, None, :] + sa[..., None] * b_t[:, :, None, :]
                 + v_t[..., None] * k_t[:, :, None, :])
        y_t = jnp.einsum('bhvk,bhk->bhv', state, r_t)
        return state, y_t

    xs = (jnp.moveaxis(r, 1, 0), jnp.moveaxis(decay, 1, 0), jnp.moveaxis(k, 1, 0),
          jnp.moveaxis(v, 1, 0), jnp.moveaxis(a, 1, 0), jnp.moveaxis(b, 1, 0))
    state0 = jnp.zeros((bsz, h, n, n), r.dtype)
    _, ys = lax.scan(step, state0, xs)
    return jnp.moveaxis(ys, 0, 1)


def _rwkv7_branch(p, w0, w_up, a0, a_up, g_up, k_k, k_a, r_k, ln_g, ln_b):
    bsz, s, _ = p.shape
    W, H, N = RWKV_WIDTH, RWKV_HEADS, RWKV_HEAD
    r, k, v, wd, ad, gd = jnp.split(
        p, [W, 2 * W, 3 * W, 3 * W + LORA_W, 3 * W + LORA_W + LORA_A], axis=-1)
    w_raw = w0 + jnp.tanh(wd) @ w_up
    log_w = -jax.nn.softplus(-w_raw) - 0.5
    decay = jnp.exp(-jnp.exp(log_w))
    a = jax.nn.sigmoid(a0 + ad @ a_up)
    g = jax.nn.sigmoid(gd) @ g_up
    heads = lambda t: t.reshape(bsz, s, H, N)
    kk = heads(k * k_k).astype(jnp.float32)
    kk = (kk * lax.rsqrt(jnp.sum(kk * kk, axis=-1, keepdims=True) + L2_EPS)).astype(k.dtype)
    k = k * (1.0 + (a - 1.0) * k_a)
    r_h, k_h, v_h, a_h = heads(r), heads(k), heads(v), heads(a)
    y = _wkv7(r_h, heads(decay), k_h, v_h, -kk, kk * a_h)
    yf = y.astype(jnp.float32)
    mean = jnp.mean(yf, axis=-1, keepdims=True)
    var = jnp.mean(jnp.square(yf - mean), axis=-1, keepdims=True)
    y = ((yf - mean) * lax.rsqrt(var + GN_EPS)).astype(y.dtype)
    y = y * ln_g.reshape(H, N) + ln_b.reshape(H, N)
    bonus = jnp.sum(r_h * k_h * r_k, axis=-1, keepdims=True) * v_h
    return (y + bonus).reshape(bsz, s, W) * g


def _complex_linear_combine(e_i, e_j):
    a_re_i, a_im_i, b_re_i, b_im_i = e_i
    a_re_j, a_im_j, b_re_j, b_im_j = e_j
    a_re = a_re_j * a_re_i - a_im_j * a_im_i
    a_im = a_re_j * a_im_i + a_im_j * a_re_i
    b_re = a_re_j * b_re_i - a_im_j * b_im_i + b_re_j
    b_im = a_re_j * b_im_i + a_im_j * b_re_i + b_im_j
    return (a_re, a_im, b_re, b_im)


def _s5_branch(u, a_re, a_im, log_dt, b_re, b_im, c_re, c_im, d, w_glu):
    bsz, s, _ = u.shape
    G, P, C = S5_GROUPS, S5_STATE, S5_GROUP
    dt = jnp.exp(log_dt)[:, None]
    z_re, z_im = a_re * dt, a_im * dt
    mag = jnp.exp(z_re)
    ab_re, ab_im = mag * jnp.cos(z_im), mag * jnp.sin(z_im)
    den = a_re * a_re + a_im * a_im
    q_re = ((ab_re - 1.0) * a_re + ab_im * a_im) / den
    q_im = (ab_im * a_re - (ab_re - 1.0) * a_im) / den
    bb_re = q_re[..., None] * b_re - q_im[..., None] * b_im
    bb_im = q_re[..., None] * b_im + q_im[..., None] * b_re
    ug = u.reshape(bsz, s, G, C)
    bu_re = jnp.einsum('bsgc,gpc->bsgp', ug, bb_re)
    bu_im = jnp.einsum('bsgc,gpc->bsgp', ug, bb_im)
    a_seq_re = jnp.broadcast_to(ab_re[None, None], (1, s, G, P))
    a_seq_im = jnp.broadcast_to(ab_im[None, None], (1, s, G, P))
    _, _, x_re, x_im = lax.associative_scan(
        _complex_linear_combine, (a_seq_re, a_seq_im, bu_re, bu_im), axis=1)
    y = jnp.einsum('bsgp,gcp->bsgc', x_re, c_re) - jnp.einsum('bsgp,gcp->bsgc', x_im, c_im)
    y = y.reshape(bsz, s, S5_WIDTH) + d * u
    y = jax.nn.gelu(y)
    glu_a, glu_b = jnp.split(y @ w_glu, 2, axis=-1)
    return glu_a * jax.nn.sigmoid(glu_b)


def setup_inputs(seed: int = 0) -> dict:
    key = jax.random.key(seed)
    ks = iter(jax.random.split(key, 40))
    L = DEPTH

    def nrm(shape, scale):
        return scale * jax.random.normal(next(ks), shape, jnp.float32)

    def uni(shape, lo, hi):
        return jax.random.uniform(next(ks), shape, jnp.float32, lo, hi)

    n_idx = jnp.arange(S5_STATE, dtype=jnp.float32)
    return {
        "x": nrm((BATCH, SEQ, D_MODEL), 1.0),
        "c": nrm((BATCH, D_MODEL), 1.0),
        "w_ada": nrm((L, D_MODEL, 6 * D_MODEL), 0.02),
        "b_ada": nrm((L, 6 * D_MODEL), 0.02),
        "norm1_g": 1.0 + nrm((L, D_MODEL), 0.02),
        "w_in": nrm((L, D_MODEL, IN_COLS), D_MODEL ** -0.5),
        "mu_shift": uni((L, SHIFT_COLS), 0.0, 1.0),
        "rwkv_w0": uni((L, RWKV_WIDTH), -6.0, -0.5),
        "rwkv_w_up": nrm((L, LORA_W, RWKV_WIDTH), LORA_W ** -0.5),
        "rwkv_a0": nrm((L, RWKV_WIDTH), 0.1),
        "rwkv_a_up": nrm((L, LORA_A, RWKV_WIDTH), LORA_A ** -0.5),
        "rwkv_g_up": nrm((L, LORA_G, RWKV_WIDTH), LORA_G ** -0.5),
        "rwkv_k_k": 0.85 + nrm((L, RWKV_WIDTH), 0.02),
        "rwkv_k_a": 1.0 + nrm((L, RWKV_WIDTH), 0.02),
        "rwkv_r_k": nrm((L, RWKV_HEADS, RWKV_HEAD), 0.1),
        "rwkv_ln_g": 1.0 + nrm((L, RWKV_WIDTH), 0.02),
        "rwkv_ln_b": nrm((L, RWKV_WIDTH), 0.02),
        "w_out_rwkv": nrm((L, RWKV_WIDTH, D_MODEL), RWKV_WIDTH ** -0.5),
        "s5_a_re": -0.5 + nrm((L, S5_GROUPS, S5_STATE), 0.01),
        "s5_a_im": math.pi * n_idx + nrm((L, S5_GROUPS, S5_STATE), 0.01),
        "s5_log_dt": uni((L, S5_GROUPS), math.log(DT_MIN), math.log(DT_MAX)),
        "s5_b_re": nrm((L, S5_GROUPS, S5_STATE, S5_GROUP), (2 * S5_GROUP) ** -0.5),
        "s5_b_im": nrm((L, S5_GROUPS, S5_STATE, S5_GROUP), (2 * S5_GROUP) ** -0.5),
        "s5_c_re": nrm((L, S5_GROUPS, S5_GROUP, S5_STATE), (2 * S5_STATE) ** -0.5),
        "s5_c_im": nrm((L, S5_GROUPS, S5_GROUP, S5_STATE), (2 * S5_STATE) ** -0.5),
        "s5_d": nrm((L, S5_WIDTH), 1.0),
        "w_glu": nrm((L, S5_WIDTH, 2 * D_MODEL), S5_WIDTH ** -0.5),
        "w_out": nrm((L, D_MODEL, D_MODEL), D_MODEL ** -0.5),
        "norm2_g": 1.0 + nrm((L, D_MODEL), 0.02),
        "w_ffn_up": nrm((L, D_MODEL, 2 * D_FF), D_MODEL ** -0.5),
        "ffn_conv_w": nrm((L, CONV_W, 2 * D_FF), CONV_W ** -0.5),
        "ffn_conv_b": nrm((L, 2 * D_FF), 0.02),
        "w_ffn_down": nrm((L, D_FF, D_MODEL), D_FF ** -0.5),
        "norm_f_g": 1.0 + nrm((D_MODEL,), 0.02),
    }


def reference(x, c, w_ada, b_ada, norm1_g, w_in, mu_shift, rwkv_w0, rwkv_w_up, rwkv_a0,
              rwkv_a_up, rwkv_g_up, rwkv_k_k, rwkv_k_a, rwkv_r_k, rwkv_ln_g, rwkv_ln_b,
              w_out_rwkv, s5_a_re, s5_a_im, s5_log_dt, s5_b_re, s5_b_im, s5_c_re, s5_c_im,
              s5_d, w_glu, w_out, norm2_g, w_ffn_up, ffn_conv_w, ffn_conv_b, w_ffn_down,
              norm_f_g):
    for l in range(DEPTH):
        mod = (jax.nn.silu(c) @ w_ada[l] + b_ada[l])[:, None, :]
        sh1, sc1, gt1, sh2, sc2, gt2 = jnp.split(mod, 6, axis=-1)

        h = _rms_norm(x, norm1_g[l]) * (1.0 + sc1) + sh1
        proj = h @ w_in[l]
        p_rwkv, u_s5, gates = jnp.split(proj, [SHIFT_COLS, SHIFT_COLS + S5_WIDTH], axis=-1)
        p_rwkv = _token_shift(p_rwkv, mu_shift[l])
        y_a = _rwkv7_branch(p_rwkv, rwkv_w0[l], rwkv_w_up[l], rwkv_a0[l], rwkv_a_up[l],
                            rwkv_g_up[l], rwkv_k_k[l], rwkv_k_a[l], rwkv_r_k[l],
                            rwkv_ln_g[l], rwkv_ln_b[l]) @ w_out_rwkv[l]
        y_b = _s5_branch(u_s5, s5_a_re[l], s5_a_im[l], s5_log_dt[l], s5_b_re[l], s5_b_im[l],
                         s5_c_re[l], s5_c_im[l], s5_d[l], w_glu[l])
        g_a, g_b = jnp.split(jax.nn.sigmoid(gates), 2, axis=-1)
        mixed = (g_a * y_a + g_b * y_b) @ w_out[l]
        x = x + gt1 * mixed

        h = _rms_norm(x, norm2_g[l]) * (1.0 + sc2) + sh2
        hid = _causal_dwconv(h @ w_ffn_up[l], ffn_conv_w[l], ffn_conv_b[l])
        gate, up = jnp.split(hid, 2, axis=-1)
        x = x + gt2 * ((jax.nn.silu(gate) * up) @ w_ffn_down[l])
    return _rms_norm(x, norm_f_g)
```

```python
import functools
import math

import jax
import jax.numpy as jnp
from jax import lax
from jax.experimental import pallas as pl
from jax.experimental.pallas import tpu as pltpu

F32 = jnp.float32
BF16 = jnp.bfloat16
HIGHEST = lax.Precision.HIGHEST

D_MODEL = 1024
RWKV_WIDTH = 512
RWKV_HEAD = 64
LORA_W = 64
LORA_A = 64
LORA_G = 128
S5_WIDTH = 512
S5_GROUP = 16
S5_GROUPS = 32
S5_STATE = 64
D_FF = 2816
CONV_W = 3
RMS_EPS = 1e-6
GN_EPS = 64e-5
L2_EPS = 1e-12
SHIFT_COLS = 3 * RWKV_WIDTH + LORA_W + LORA_A + LORA_G
IN_COLS = SHIFT_COLS + S5_WIDTH + 2 * D_MODEL

LANES = 128
SUBLANES = 8
WKV_CHUNK = 64
HEADS_PER_TILE = LANES // RWKV_HEAD
STACK = HEADS_PER_TILE * WKV_CHUNK
S5_SLABS = S5_WIDTH // LANES
S5_SLAB_STATE = (LANES // S5_GROUP) * S5_STATE
FFN_SPLIT = 2
FFN_CH = D_FF // FFN_SPLIT
VMEM_LIMIT = 56 * 1024 * 1024


def _dot(a, b):
    return jnp.dot(a.astype(BF16), b.astype(BF16), preferred_element_type=F32)


def _dot_split(a, g):
    hi = a.astype(BF16)
    lo = (a - hi.astype(F32)).astype(BF16)
    return (jnp.dot(hi, g, preferred_element_type=F32)
            + jnp.dot(lo, g, preferred_element_type=F32))


def _const_spec(shape):
    nd = len(shape)
    return pl.BlockSpec(shape, lambda *_: (0,) * nd, pipeline_mode=pl.Buffered(1))


def _params(*sem):
    return pltpu.CompilerParams(dimension_semantics=sem, vmem_limit_bytes=VMEM_LIMIT)


def _adaln_kernel(c_ref, w_ref, b_ref, o_ref):
    c = c_ref[...]
    s = c * jax.nn.sigmoid(c)
    o_ref[...] = jnp.dot(s, w_ref[...], preferred_element_type=F32, precision=HIGHEST) + b_ref[...]


def _adaln(c, w_ada, b_ada):
    bsz = c.shape[0]
    n = w_ada.shape[1]
    bn = D_MODEL
    return pl.pallas_call(
        _adaln_kernel,
        grid=(n // bn,),
        in_specs=[pl.BlockSpec((bsz, D_MODEL), lambda j: (0, 0)),
                  pl.BlockSpec((D_MODEL, bn), lambda j: (0, j)),
                  pl.BlockSpec((1, bn), lambda j: (0, j))],
        out_specs=pl.BlockSpec((bsz, bn), lambda j: (0, j)),
        out_shape=jax.ShapeDtypeStruct((bsz, n), F32),
        compiler_params=_params("arbitrary"),
        name="adaln",
    )(c, w_ada, b_ada.reshape(1, n))


def _s5prep_kernel(are_ref, aim_ref, ldt_ref, bre_ref, bim_ref, abre_o, abim_o, bbre_o, bbim_o):
    a_re = are_ref[...]
    a_im = aim_ref[...]
    dt = jnp.exp(ldt_ref[...])
    z_re, z_im = a_re * dt, a_im * dt
    mag = jnp.exp(z_re)
    ab_re, ab_im = mag * jnp.cos(z_im), mag * jnp.sin(z_im)
    den = a_re * a_re + a_im * a_im
    q_re = ((ab_re - 1.0) * a_re + ab_im * a_im) / den
    q_im = (ab_im * a_re - (ab_re - 1.0) * a_im) / den
    abre_o[...] = ab_re
    abim_o[...] = ab_im
    b_re = bre_ref[...]
    b_im = bim_ref[...]
    bbre_o[...] = q_re[:, None, :] * b_re - q_im[:, None, :] * b_im
    bbim_o[...] = q_re[:, None, :] * b_im + q_im[:, None, :] * b_re


def _s5prep(a_re, a_im, log_dt, b_re, b_im):
    g, p = a_re.shape
    c = b_re.shape[-1]
    b_re_t = jnp.transpose(b_re, (0, 2, 1))
    b_im_t = jnp.transpose(b_im, (0, 2, 1))
    return pl.pallas_call(
        _s5prep_kernel,
        out_shape=(jax.ShapeDtypeStruct((g, p), F32), jax.ShapeDtypeStruct((g, p), F32),
                   jax.ShapeDtypeStruct((g, c, p), F32), jax.ShapeDtypeStruct((g, c, p), F32)),
        name="s5prep",
    )(a_re, a_im, log_dt.reshape(g, 1), b_re_t, b_im_t)


def _inproj_kernel(x_ref, mod_ref, g1_ref, win_ref, mu_ref, w0_ref, waup_ref, a0_ref, gup_ref,
                   kk_ref, ka_ref, rk_ref, g64_ref,
                   r_o, k_o, v_o, ld_o, kn_o, b_o, g_o, bg_o, u_o, gates_o, carry, *, rows):
    @pl.when(pl.program_id(1) == 0)
    def _():
        carry[...] = jnp.zeros_like(carry)

    x = x_ref[...]
    mod = mod_ref[...]
    sh1, sc1 = mod[0:1], mod[1:2]
    ms = jnp.mean(x * x, axis=-1, keepdims=True)
    h = x * lax.rsqrt(ms + RMS_EPS) * g1_ref[...]
    h = h * (1.0 + sc1) + sh1
    proj = jnp.dot(h.astype(BF16), win_ref[...], preferred_element_type=F32)

    p = proj[:, :SHIFT_COLS]
    row = lax.broadcasted_iota(jnp.int32, p.shape, 0)
    prev = jnp.where(row == 0, carry[...], pltpu.roll(p, 1, axis=0))
    carry[...] = p[rows - 1:rows, :]
    ps = p + (prev - p) * mu_ref[...]

    w = RWKV_WIDTH
    r, k, v = ps[:, 0:w], ps[:, w:2 * w], ps[:, 2 * w:3 * w]
    wa = ps[:, 3 * w:3 * w + LORA_W + LORA_A]
    lane = lax.broadcasted_iota(jnp.int32, wa.shape, 1)
    wa = jnp.where(lane < LORA_W, jnp.tanh(wa), wa)
    delta = _dot(wa, waup_ref[...])
    w_raw = w0_ref[...] + delta[:, :w]
    ld = -math.exp(-0.5) * jax.nn.sigmoid(w_raw)
    eta = jax.nn.sigmoid(a0_ref[...] + delta[:, w:])
    gd = ps[:, 3 * w + LORA_W + LORA_A:SHIFT_COLS]
    g = _dot(jax.nn.sigmoid(gd), gup_ref[...])

    g64 = g64_ref[...]
    kk = k * kk_ref[...]
    kn = kk * lax.rsqrt(_dot_split(kk * kk, g64) + L2_EPS)
    k2 = k * (1.0 + (eta - 1.0) * ka_ref[...])
    bonus = _dot_split(r * k2 * rk_ref[...], g64) * v

    r_o[...] = r
    k_o[...] = k2
    v_o[...] = v
    ld_o[...] = ld
    kn_o[...] = kn
    b_o[...] = kn * eta
    g_o[...] = g
    bg_o[...] = bonus * g
    u_o[...] = proj[:, SHIFT_COLS:SHIFT_COLS + S5_WIDTH]
    gates_o[...] = jax.nn.sigmoid(proj[:, SHIFT_COLS + S5_WIDTH:])


def _inproj(x2, mod3, g1, win, mu, w0, waup, a0, gup, k_k, k_a, r_k, g64, *, bsz, seq, rows):
    n = bsz * seq
    nt = seq // rows
    w = RWKV_WIDTH
    row_spec = lambda cols: pl.BlockSpec((rows, cols), lambda b, t: (b * nt + t, 0))
    outs = [jax.ShapeDtypeStruct((n, w), F32)] * 9 + [jax.ShapeDtypeStruct((n, 2 * D_MODEL), F32)]
    return pl.pallas_call(
        functools.partial(_inproj_kernel, rows=rows),
        grid=(bsz, nt),
        in_specs=[row_spec(D_MODEL),
                  pl.BlockSpec((None, 6, D_MODEL), lambda b, t: (b, 0, 0)),
                  _const_spec((1, D_MODEL)),
                  _const_spec((D_MODEL, IN_COLS)),
                  _const_spec((1, SHIFT_COLS)),
                  _const_spec((1, w)),
                  _const_spec((LORA_W + LORA_A, 2 * w)),
                  _const_spec((1, w)),
                  _const_spec((LORA_G, w)),
                  _const_spec((1, w)), _const_spec((1, w)), _const_spec((1, w)),
                  _const_spec((w, w))],
        out_specs=[row_spec(w)] * 9 + [row_spec(2 * D_MODEL)],
        out_shape=outs,
        scratch_shapes=[pltpu.VMEM((1, SHIFT_COLS), F32)],
        compiler_params=_params("arbitrary", "arbitrary"),
        name="inproj",
    )(x2, mod3, g1, win, mu, w0, waup, a0, gup, k_k, k_a, r_k, g64)


def _stack_heads(z):
    lane = lax.broadcasted_iota(jnp.int32, z.shape, 1)
    first = lane < RWKV_HEAD
    return jnp.concatenate([jnp.where(first, z, 0.0), jnp.where(first, 0.0, z)], axis=0)


def _wkv_chunk_terms(r, k, v, ld, kn, b):
    t = WKV_CHUNK
    ti = lax.broadcasted_iota(jnp.int32, (t, t), 0)
    tj = lax.broadcasted_iota(jnp.int32, (t, t), 1)
    tri = (ti >= tj).astype(F32)
    cl = jnp.dot(tri, ld, preferred_element_type=F32, precision=HIGHEST)
    cl_end = cl[t - 1:t, :]
    e_incl = jnp.exp(cl)
    e_excl = jnp.exp(cl - ld)
    e_neg = jnp.exp(-cl)
    e_end = jnp.exp(cl_end - cl)

    a_s = _stack_heads(-kn * e_excl)
    r_s = _stack_heads(r * e_incl)
    b_s = _stack_heads(b * e_neg)
    k_s = _stack_heads(k * e_neg)
    bh_s = _stack_heads(b * e_end)
    kh_s = _stack_heads(k * e_end)
    v_s = _stack_heads(v)

    n = STACK
    ri = lax.broadcasted_iota(jnp.int32, (n, n), 0)
    ci = lax.broadcasted_iota(jnp.int32, (n, n), 1)
    same_head = (ri // t) == (ci // t)
    strict = same_head & (ri > ci)
    incl = same_head & (ri >= ci)

    lhs = jnp.concatenate([a_s, r_s], axis=0).astype(BF16)
    rhs = jnp.concatenate([b_s, k_s], axis=0).astype(BF16)
    amat = lax.dot_general(lhs, rhs, (((1,), (1,)), ((), ())), preferred_element_type=F32)
    a_ab = jnp.where(strict, amat[:n, :n], 0.0)
    a_ak = jnp.where(strict, amat[:n, n:], 0.0)
    a_rb = jnp.where(incl, amat[n:, :n], 0.0)
    a_rk = jnp.where(incl, amat[n:, n:], 0.0)

    x = jnp.where(ri == ci, 1.0, 0.0) + jnp.where((ri // 2 == ci // 2), a_ab, 0.0)
    s = 2
    while s < t:
        level = ((ri // (2 * s)) == (ci // (2 * s))) & ((ri // s) % 2 == 1) & ((ci // s) % 2 == 0)
        c = jnp.where(level, a_ab, 0.0)
        x = x + _dot(_dot(x, c), x)
        s *= 2

    akv = _dot(a_ak, v_s)
    tg = _dot(x, jnp.concatenate([a_s, akv], axis=1))
    a_hat, u_v = tg[:, :LANES], tg[:, LANES:]
    low = jnp.concatenate([jnp.concatenate([a_hat, u_v], axis=1),
                           jnp.concatenate([jnp.zeros_like(v_s), v_s], axis=1)], axis=0).astype(BF16)
    out1 = _dot(jnp.concatenate([a_rb, a_rk], axis=1), low)
    r_hat = r_s + out1[:, :LANES]
    y_v = out1[:, LANES:]
    bk = jnp.concatenate([bh_s, kh_s], axis=0).astype(BF16)
    out2 = lax.dot_general(bk, low, (((0,), (0,)), ((), ())), preferred_element_type=F32)
    ki = lax.broadcasted_iota(jnp.int32, (LANES, LANES), 0)
    kj = lax.broadcasted_iota(jnp.int32, (LANES, LANES), 1)
    m = out2[:, :LANES] + jnp.where(ki == kj, jnp.exp(cl_end), 0.0)
    n_c = out2[:, LANES:]
    return r_hat, y_v, m, n_c


def _wkv_kernel(r_ref, k_ref, v_ref, ld_ref, kn_ref, b_ref, y_ref, s_ref, *, seq, group):
    t = WKV_CHUNK
    s_ref[...] = jnp.zeros_like(s_ref)

    def body(i, _):
        terms = []
        for j in range(group):
            sl = pl.ds(pl.multiple_of((i * group + j) * t, t), t)
            terms.append((sl, _wkv_chunk_terms(r_ref[sl, :], k_ref[sl, :], v_ref[sl, :],
                                               ld_ref[sl, :], kn_ref[sl, :], b_ref[sl, :])))
        for sl, (r_hat, y_v, m, n_c) in terms:
            out = _dot(jnp.concatenate([m, r_hat], axis=0), s_ref[...])
            s_ref[...] = out[:LANES] + n_c
            ys = out[LANES:] + y_v
            y_ref[sl, :] = ys[:t] + ys[t:]
        return 0

    lax.fori_loop(0, seq // (t * group), body, 0)


def _wkv(r, k, v, ld, kn, b, *, bsz, seq):
    n = bsz * seq
    group = 2
    spec = pl.BlockSpec((seq, LANES), lambda bi, p: (bi, p))
    return pl.pallas_call(
        functools.partial(_wkv_kernel, seq=seq, group=group),
        grid=(bsz, RWKV_WIDTH // LANES),
        in_specs=[spec] * 6,
        out_specs=spec,
        out_shape=jax.ShapeDtypeStruct((n, RWKV_WIDTH), F32),
        scratch_shapes=[pltpu.VMEM((LANES, LANES), F32)],
        compiler_params=_params("arbitrary", "arbitrary"),
        name="wkv",
    )(r, k, v, ld, kn, b)


def _gelu_tanh(x):
    return 0.5 * x * (1.0 + jnp.tanh(math.sqrt(2.0 / math.pi) * (x + 0.044715 * x * x * x)))


def _s5_kernel(u_ref, wre_ref, wim_ref, cc_ref, ar_ref, ai_ref, d_ref, y_ref,
               xre, xim, st_re, st_im, *, steps):
    @pl.when(pl.program_id(1) == 0)
    def _():
        st_re[...] = jnp.zeros_like(st_re)
        st_im[...] = jnp.zeros_like(st_im)

    rows = steps * SUBLANES
    u = u_ref[...].reshape(rows, S5_WIDTH)
    ub = u.astype(BF16)
    ns = S5_SLAB_STATE
    for j in range(S5_SLABS):
        uj = ub[:, j * LANES:(j + 1) * LANES]
        xre[:, j * ns:(j + 1) * ns] = jnp.dot(uj, wre_ref[j], preferred_element_type=F32)
        xim[:, j * ns:(j + 1) * ns] = jnp.dot(uj, wim_ref[j], preferred_element_type=F32)

    half = S5_GROUPS * S5_STATE // 2
    for c in range(2):
        cs = slice(c * half, (c + 1) * half)
        ar = jnp.broadcast_to(ar_ref[:, cs], (SUBLANES, half))
        ai = jnp.broadcast_to(ai_ref[:, cs], (SUBLANES, half))

        def step(i, state, cs=cs, ar=ar, ai=ai):
            x_re, x_im = state
            rs = pl.ds(pl.multiple_of(i * SUBLANES, SUBLANES), SUBLANES)
            n_re = ar * x_re - ai * x_im + xre[rs, cs]
            n_im = ar * x_im + ai * x_re + xim[rs, cs]
            xre[rs, cs] = n_re
            xim[rs, cs] = n_im
            return n_re, n_im

        x_re, x_im = lax.fori_loop(0, steps, step, (st_re[:, cs], st_im[:, cs]), unroll=2)
        st_re[:, cs] = x_re
        st_im[:, cs] = x_im

    for j in range(S5_SLABS):
        xs = jnp.concatenate([xre[:, j * ns:(j + 1) * ns], xim[:, j * ns:(j + 1) * ns]], axis=1)
        ls = slice(j * LANES, (j + 1) * LANES)
        yj = jnp.dot(xs.astype(BF16), cc_ref[j], preferred_element_type=F32) + d_ref[:, ls] * u[:, ls]
        y_ref[:, :, ls] = _gelu_tanh(yj).reshape(steps, SUBLANES, LANES)


def _s5(u_t, wre, wim, cc, ar, ai, d, *, bsz, seq, steps):
    nstate = S5_GROUPS * S5_STATE
    rows = steps * SUBLANES
    spec = pl.BlockSpec((steps, SUBLANES, S5_WIDTH), lambda g, t: (t, g, 0))
    return pl.pallas_call(
        functools.partial(_s5_kernel, steps=steps),
        grid=(bsz // SUBLANES, seq // steps),
        in_specs=[spec,
                  _const_spec((S5_SLABS, LANES, S5_SLAB_STATE)),
                  _const_spec((S5_SLABS, LANES, S5_SLAB_STATE)),
                  _const_spec((S5_SLABS, 2 * S5_SLAB_STATE, LANES)),
                  _const_spec((1, nstate)), _const_spec((1, nstate)),
                  _const_spec((1, S5_WIDTH))],
        out_specs=spec,
        out_shape=jax.ShapeDtypeStruct((seq, bsz, S5_WIDTH), F32),
        scratch_shapes=[pltpu.VMEM((rows, nstate), F32), pltpu.VMEM((rows, nstate), F32),
                        pltpu.VMEM((SUBLANES, nstate), F32), pltpu.VMEM((SUBLANES, nstate), F32)],
        compiler_params=_params("arbitrary", "arbitrary"),
        name="s5",
    )(u_t, wre, wim, cc, ar, ai, d)


def _merge_kernel(y_ref, g_ref, bg_ref, ys5_ref, gates_ref, x_ref, mod_ref, lng_ref, lnb_ref,
                  gmean_ref, wrw_ref, wglu_ref, wout_ref, o_ref):
    gm = gmean_ref[...]
    y = y_ref[...]
    mean = _dot_split(y, gm)
    d = y - mean
    var = _dot_split(d * d, gm)
    yn = d * lax.rsqrt(var + GN_EPS) * lng_ref[...] + lnb_ref[...]
    ya = _dot(yn * g_ref[...] + bg_ref[...], wrw_ref[...])
    z = _dot(ys5_ref[...], wglu_ref[...])
    yb = z[:, :D_MODEL] * jax.nn.sigmoid(z[:, D_MODEL:])
    gates = gates_ref[...]
    mixed = _dot(gates[:, :D_MODEL] * ya + gates[:, D_MODEL:] * yb, wout_ref[...])
    gt1 = mod_ref[...][2:3]
    o_ref[...] = x_ref[...] + gt1 * mixed


def _merge(y, g, bg, ys5, gates, x2, mod3, lng, lnb, gmean, wrw, wglu, wout, *, bsz, seq, rows):
    n = bsz * seq
    nt = seq // rows
    w = RWKV_WIDTH
    row_spec = lambda cols: pl.BlockSpec((rows, cols), lambda i: (i, 0))
    return pl.pallas_call(
        _merge_kernel,
        grid=(n // rows,),
        in_specs=[row_spec(w), row_spec(w), row_spec(w), row_spec(w), row_spec(2 * D_MODEL),
                  row_spec(D_MODEL),
                  pl.BlockSpec((None, 6, D_MODEL), lambda i: (i // nt, 0, 0)),
                  _const_spec((1, w)), _const_spec((1, w)), _const_spec((w, w)),
                  _const_spec((w, D_MODEL)), _const_spec((S5_WIDTH, 2 * D_MODEL)),
                  _const_spec((D_MODEL, D_MODEL))],
        out_specs=row_spec(D_MODEL),
        out_shape=jax.ShapeDtypeStruct((n, D_MODEL), F32),
        compiler_params=_params("arbitrary"),
        name="merge",
    )(y, g, bg, ys5, gates, x2, mod3, lng, lnb, gmean, wrw, wglu, wout)


def _ffn_kernel(x_ref, mod_ref, g2_ref, wup_ref, cw_ref, cb_ref, wdn_ref, gf_ref, o_ref, buf, *, rows):
    pad = SUBLANES

    @pl.when(pl.program_id(1) == 0)
    def _():
        buf[:, 0:pad, :] = jnp.zeros((FFN_SPLIT, pad, 2 * FFN_CH), F32)

    x = x_ref[...]
    mod = mod_ref[...]
    sh2, sc2, gt2 = mod[3:4], mod[4:5], mod[5:6]
    ms = jnp.mean(x * x, axis=-1, keepdims=True)
    h = x * lax.rsqrt(ms + RMS_EPS) * g2_ref[...]
    hb = (h * (1.0 + sc2) + sh2).astype(BF16)

    acc = jnp.zeros((rows, D_MODEL), F32)
    for c in range(FFN_SPLIT):
        buf[c, pad:pad + rows, :] = jnp.dot(hb, wup_ref[c], preferred_element_type=F32)
        cw = cw_ref[c]
        hid = cb_ref[c]
        for j in range(CONV_W):
            off = pad - (CONV_W - 1) + j
            hid = hid + cw[j:j + 1, :] * buf[c, off:off + rows, :]
        buf[c, pad - (CONV_W - 1):pad, :] = buf[c, pad + rows - (CONV_W - 1):pad + rows, :]
        gate, up = hid[:, :FFN_CH], hid[:, FFN_CH:]
        act = gate * jax.nn.sigmoid(gate) * up
        acc = acc + jnp.dot(act.astype(BF16), wdn_ref[c], preferred_element_type=F32)

    x2 = x + gt2 * acc
    ms2 = jnp.mean(x2 * x2, axis=-1, keepdims=True)
    o_ref[...] = x2 * lax.rsqrt(ms2 + RMS_EPS) * gf_ref[...]


def _ffn(x1, mod3, g2, wup, cw, cb, wdn, gf, *, bsz, seq, rows):
    n = bsz * seq
    nt = seq // rows
    row_spec = pl.BlockSpec((rows, D_MODEL), lambda b, t: (b * nt + t, 0))
    return pl.pallas_call(
        functools.partial(_ffn_kernel, rows=rows),
        grid=(bsz, nt),
        in_specs=[row_spec,
                  pl.BlockSpec((None, 6, D_MODEL), lambda b, t: (b, 0, 0)),
                  _const_spec((1, D_MODEL)),
                  _const_spec((FFN_SPLIT, D_MODEL, 2 * FFN_CH)),
                  _const_spec((FFN_SPLIT, CONV_W, 2 * FFN_CH)),
                  _const_spec((FFN_SPLIT, 1, 2 * FFN_CH)),
                  _const_spec((FFN_SPLIT, FFN_CH, D_MODEL)),
                  _const_spec((1, D_MODEL))],
        out_specs=row_spec,
        out_shape=jax.ShapeDtypeStruct((n, D_MODEL), F32),
        scratch_shapes=[pltpu.VMEM((FFN_SPLIT, rows + SUBLANES, 2 * FFN_CH), F32)],
        compiler_params=_params("arbitrary", "arbitrary"),
        name="ffn",
    )(x1, mod3, g2, wup, cw, cb, wdn, gf)


def _split_cols(a):
    parts = [jnp.concatenate([a[..., c * FFN_CH:(c + 1) * FFN_CH],
                              a[..., D_FF + c * FFN_CH:D_FF + (c + 1) * FFN_CH]], axis=-1)
             for c in range(FFN_SPLIT)]
    return jnp.stack(parts, axis=0)


def _block_diag_ones(n, blk, value, dtype):
    i = jnp.arange(n) // blk
    return jnp.where(i[:, None] == i[None, :], value, 0.0).astype(dtype)


def kernel(x, c, w_ada, b_ada, norm1_g, w_in, mu_shift, rwkv_w0, rwkv_w_up, rwkv_a0, rwkv_a_up,
           rwkv_g_up, rwkv_k_k, rwkv_k_a, rwkv_r_k, rwkv_ln_g, rwkv_ln_b, w_out_rwkv, s5_a_re,
           s5_a_im, s5_log_dt, s5_b_re, s5_b_im, s5_c_re, s5_c_im, s5_d, w_glu, w_out, norm2_g,
           w_ffn_up, ffn_conv_w, ffn_conv_b, w_ffn_down, norm_f_g):
    bsz, seq, _ = x.shape
    depth = w_ada.shape[0]
    n = bsz * seq
    w = RWKV_WIDTH
    rows_a = min(256, seq)
    rows_m = min(256, seq)
    rows_f = min(256, seq)
    s5_steps = min(64, seq)

    g64 = _block_diag_ones(w, RWKV_HEAD, 1.0, BF16)
    gmean = _block_diag_ones(w, RWKV_HEAD, 1.0 / RWKV_HEAD, BF16)
    eye8 = jnp.eye(LANES // S5_GROUP, dtype=F32)
    row = lambda a: a.reshape(1, -1)

    assert depth == 1, depth
    x2 = x.reshape(n, D_MODEL)
    for l in range(depth):
        mod3 = _adaln(c, w_ada[l], b_ada[l]).reshape(bsz, 6, D_MODEL)

        zero = jnp.zeros((LORA_W, w), F32)
        waup = jnp.concatenate([jnp.concatenate([rwkv_w_up[l], zero], axis=1),
                                jnp.concatenate([zero, rwkv_a_up[l]], axis=1)], axis=0).astype(BF16)
        r, k, v, ld, kn, b, g, bg, u, gates = _inproj(
            x2, mod3, row(norm1_g[l]), w_in[l].astype(BF16), row(mu_shift[l]), row(rwkv_w0[l]), waup,
            row(rwkv_a0[l]), rwkv_g_up[l].astype(BF16), row(rwkv_k_k[l]), row(rwkv_k_a[l]),
            row(rwkv_r_k[l]), g64, bsz=bsz, seq=seq, rows=rows_a)

        y = _wkv(r, k, v, ld, kn, b, bsz=bsz, seq=seq)

        ab_re, ab_im, bb_re, bb_im = _s5prep(s5_a_re[l], s5_a_im[l], s5_log_dt[l], s5_b_re[l], s5_b_im[l])
        gl = LANES // S5_GROUP
        slab_in = lambda bb: jnp.einsum('jgcp,gh->jgchp', bb.reshape(S5_SLABS, gl, S5_GROUP, S5_STATE),
                                        eye8).reshape(S5_SLABS, LANES, S5_SLAB_STATE).astype(BF16)
        slab_out = lambda cc: jnp.einsum('jgcp,gh->jgphc', cc.reshape(S5_SLABS, gl, S5_GROUP, S5_STATE),
                                         eye8).reshape(S5_SLABS, S5_SLAB_STATE, LANES)
        cc = jnp.concatenate([slab_out(s5_c_re[l]), -slab_out(s5_c_im[l])], axis=1).astype(BF16)
        u_t = jnp.transpose(u.reshape(bsz, seq, S5_WIDTH), (1, 0, 2))
        ys5_t = _s5(u_t, slab_in(bb_re), slab_in(bb_im), cc, row(ab_re), row(ab_im), row(s5_d[l]),
                    bsz=bsz, seq=seq, steps=s5_steps)
        ys5 = jnp.transpose(ys5_t, (1, 0, 2)).reshape(n, S5_WIDTH)

        x1 = _merge(y, g, bg, ys5, gates, x2, mod3, row(rwkv_ln_g[l]), row(rwkv_ln_b[l]), gmean,
                    w_out_rwkv[l].astype(BF16), w_glu[l].astype(BF16), w_out[l].astype(BF16),
                    bsz=bsz, seq=seq, rows=rows_m)

        gf = row(norm_f_g)
        wdn = w_ffn_down[l].astype(BF16).reshape(FFN_SPLIT, FFN_CH, D_MODEL)
        x2 = _ffn(x1, mod3, row(norm2_g[l]), _split_cols(w_ffn_up[l]).astype(BF16),
                  _split_cols(ffn_conv_w[l]), _split_cols(ffn_conv_b[l].reshape(1, -1)), wdn,
                  gf, bsz=bsz, seq=seq, rows=rows_f)
    return x2.reshape(bsz, seq, D_MODEL)
```

```python
import functools
import math

import jax
import jax.numpy as jnp
from jax import lax
from jax.experimental import pallas as pl
from jax.experimental.pallas import tpu as pltpu

F32 = jnp.float32
BF16 = jnp.bfloat16
HIGHEST = lax.Precision.HIGHEST

D_MODEL = 1024
RWKV_WIDTH = 512
RWKV_HEAD = 64
LORA_W = 64
LORA_A = 64
LORA_G = 128
S5_WIDTH = 512
S5_GROUP = 16
S5_GROUPS = 32
S5_STATE = 64
D_FF = 2816
CONV_W = 3
RMS_EPS = 1e-6
GN_EPS = 64e-5
L2_EPS = 1e-12
SHIFT_COLS = 3 * RWKV_WIDTH + LORA_W + LORA_A + LORA_G
IN_COLS = SHIFT_COLS + S5_WIDTH + 2 * D_MODEL

LANES = 128
SUBLANES = 8
WKV_CHUNK = 64
HEADS_PER_TILE = LANES // RWKV_HEAD
STACK = HEADS_PER_TILE * WKV_CHUNK
S5_SLABS = S5_WIDTH // LANES
S5_SLAB_STATE = (LANES // S5_GROUP) * S5_STATE
FFN_SPLIT = 2
FFN_CH = D_FF // FFN_SPLIT
VMEM_LIMIT = 56 * 1024 * 1024


def _dot(a, b):
    return jnp.dot(a.astype(BF16), b.astype(BF16), preferred_element_type=F32)


def _dot_split(a, g):
    hi = a.astype(BF16)
    lo = (a - hi.astype(F32)).astype(BF16)
    return (jnp.dot(hi, g, preferred_element_type=F32)
            + jnp.dot(lo, g, preferred_element_type=F32))


def _const_spec(shape):
    nd = len(shape)
    return pl.BlockSpec(shape, lambda *_: (0,) * nd, pipeline_mode=pl.Buffered(1))


def _params(*sem):
    return pltpu.CompilerParams(dimension_semantics=sem, vmem_limit_bytes=VMEM_LIMIT)


def _adaln_kernel(c_ref, w_ref, b_ref, o_ref):
    c = c_ref[...]
    s = c * jax.nn.sigmoid(c)
    o_ref[...] = jnp.dot(s, w_ref[...], preferred_element_type=F32, precision=HIGHEST) + b_ref[...]


def _adaln(c, w_ada, b_ada):
    bsz = c.shape[0]
    n = w_ada.shape[1]
    bn = D_MODEL
    return pl.pallas_call(
        _adaln_kernel,
        grid=(n // bn,),
        in_specs=[pl.BlockSpec((bsz, D_MODEL), lambda j: (0, 0)),
                  pl.BlockSpec((D_MODEL, bn), lambda j: (0, j)),
                  pl.BlockSpec((1, bn), lambda j: (0, j))],
        out_specs=pl.BlockSpec((bsz, bn), lambda j: (0, j)),
        out_shape=jax.ShapeDtypeStruct((bsz, n), F32),
        compiler_params=_params("arbitrary"),
        name="adaln",
    )(c, w_ada, b_ada.reshape(1, n))


def _s5prep_kernel(are_ref, aim_ref, ldt_ref, bre_ref, bim_ref, abre_o, abim_o, bbre_o, bbim_o):
    a_re = are_ref[...]
    a_im = aim_ref[...]
    dt = jnp.exp(ldt_ref[...])
    z_re, z_im = a_re * dt, a_im * dt
    mag = jnp.exp(z_re)
    ab_re, ab_im = mag * jnp.cos(z_im), mag * jnp.sin(z_im)
    den = a_re * a_re + a_im * a_im
    q_re = ((ab_re - 1.0) * a_re + ab_im * a_im) / den
    q_im = (ab_im * a_re - (ab_re - 1.0) * a_im) / den
    abre_o[...] = ab_re
    abim_o[...] = ab_im
    b_re = bre_ref[...]
    b_im = bim_ref[...]
    bbre_o[...] = q_re[:, None, :] * b_re - q_im[:, None, :] * b_im
    bbim_o[...] = q_re[:, None, :] * b_im + q_im[:, None, :] * b_re


def _s5prep(a_re, a_im, log_dt, b_re, b_im):
    g, p = a_re.shape
    c = b_re.shape[-1]
    b_re_t = jnp.transpose(b_re, (0, 2, 1))
    b_im_t = jnp.transpose(b_im, (0, 2, 1))
    return pl.pallas_call(
        _s5prep_kernel,
        out_shape=(jax.ShapeDtypeStruct((g, p), F32), jax.ShapeDtypeStruct((g, p), F32),
                   jax.ShapeDtypeStruct((g, c, p), F32), jax.ShapeDtypeStruct((g, c, p), F32)),
        name="s5prep",
    )(a_re, a_im, log_dt.reshape(g, 1), b_re_t, b_im_t)


def _inproj_kernel(x_ref, mod_ref, g1_ref, win_ref, mu_ref, w0_ref, waup_ref, a0_ref, gup_ref,
                   kk_ref, ka_ref, rk_ref, g64_ref,
                   r_o, k_o, v_o, ld_o, kn_o, b_o, g_o, bg_o, u_o, gates_o, carry, *, rows):
    @pl.when(pl.program_id(1) == 0)
    def _():
        carry[...] = jnp.zeros_like(carry)

    x = x_ref[...]
    mod = mod_ref[...]
    sh1, sc1 = mod[0:1], mod[1:2]
    ms = jnp.mean(x * x, axis=-1, keepdims=True)
    h = x * lax.rsqrt(ms + RMS_EPS) * g1_ref[...]
    h = h * (1.0 + sc1) + sh1
    proj = jnp.dot(h.astype(BF16), win_ref[...], preferred_element_type=F32)

    p = proj[:, :SHIFT_COLS]
    row = lax.broadcasted_iota(jnp.int32, p.shape, 0)
    prev = jnp.where(row == 0, carry[...], pltpu.roll(p, 1, axis=0))
    carry[...] = p[rows - 1:rows, :]
    ps = p + (prev - p) * mu_ref[...]

    w = RWKV_WIDTH
    r, k, v = ps[:, 0:w], ps[:, w:2 * w], ps[:, 2 * w:3 * w]
    wa = ps[:, 3 * w:3 * w + LORA_W + LORA_A]
    lane = lax.broadcasted_iota(jnp.int32, wa.shape, 1)
    wa = jnp.where(lane < LORA_W, jnp.tanh(wa), wa)
    delta = _dot(wa, waup_ref[...])
    w_raw = w0_ref[...] + delta[:, :w]
    ld = -math.exp(-0.5) * jax.nn.sigmoid(w_raw)
    eta = jax.nn.sigmoid(a0_ref[...] + delta[:, w:])
    gd = ps[:, 3 * w + LORA_W + LORA_A:SHIFT_COLS]
    g = _dot(jax.nn.sigmoid(gd), gup_ref[...])

    g64 = g64_ref[...]
    kk = k * kk_ref[...]
    kn = kk * lax.rsqrt(_dot_split(kk * kk, g64) + L2_EPS)
    k2 = k * (1.0 + (eta - 1.0) * ka_ref[...])
    bonus = _dot_split(r * k2 * rk_ref[...], g64) * v

    r_o[...] = r
    k_o[...] = k2
    v_o[...] = v
    ld_o[...] = ld
    kn_o[...] = kn
    b_o[...] = kn * eta
    g_o[...] = g
    bg_o[...] = bonus * g
    u_o[...] = proj[:, SHIFT_COLS:SHIFT_COLS + S5_WIDTH]
    gates_o[...] = jax.nn.sigmoid(proj[:, SHIFT_COLS + S5_WIDTH:])


def _inproj(x2, mod3, g1, win, mu, w0, waup, a0, gup, k_k, k_a, r_k, g64, *, bsz, seq, rows):
    n = bsz * seq
    nt = seq // rows
    w = RWKV_WIDTH
    row_spec = lambda cols: pl.BlockSpec((rows, cols), lambda b, t: (b * nt + t, 0))
    outs = [jax.ShapeDtypeStruct((n, w), F32)] * 9 + [jax.ShapeDtypeStruct((n, 2 * D_MODEL), F32)]
    return pl.pallas_call(
        functools.partial(_inproj_kernel, rows=rows),
        grid=(bsz, nt),
        in_specs=[row_spec(D_MODEL),
                  pl.BlockSpec((None, 6, D_MODEL), lambda b, t: (b, 0, 0)),
                  _const_spec((1, D_MODEL)),
                  _const_spec((D_MODEL, IN_COLS)),
                  _const_spec((1, SHIFT_COLS)),
                  _const_spec((1, w)),
                  _const_spec((LORA_W + LORA_A, 2 * w)),
                  _const_spec((1, w)),
                  _const_spec((LORA_G, w)),
                  _const_spec((1, w)), _const_spec((1, w)), _const_spec((1, w)),
                  _const_spec((w, w))],
        out_specs=[row_spec(w)] * 9 + [row_spec(2 * D_MODEL)],
        out_shape=outs,
        scratch_shapes=[pltpu.VMEM((1, SHIFT_COLS), F32)],
        compiler_params=_params("arbitrary", "arbitrary"),
        name="inproj",
    )(x2, mod3, g1, win, mu, w0, waup, a0, gup, k_k, k_a, r_k, g64)


def _stack_heads(z):
    lane = lax.broadcasted_iota(jnp.int32, z.shape, 1)
    first = lane < RWKV_HEAD
    return jnp.concatenate([jnp.where(first, z, 0.0), jnp.where(first, 0.0, z)], axis=0)


def _wkv_terms(chains):
    t = WKV_CHUNK
    n = STACK
    ri = lax.broadcasted_iota(jnp.int32, (n, n), 0)
    ci = lax.broadcasted_iota(jnp.int32, (n, n), 1)
    same_head = (ri // t) == (ci // t)
    strict = same_head & (ri > ci)
    incl = same_head & (ri >= ci)
    eye = jnp.where(ri == ci, 1.0, 0.0)

    a_s, r_s, v_s, lhs, rhs, bk, w_end = [], [], [], [], [], [], []
    for r, k, v, ld, kn, b, cl in chains:
        cl_end = cl[t - 1:t, :]
        e_incl = jnp.exp(cl)
        e_excl = jnp.exp(cl - ld)
        e_neg = jnp.exp(-cl)
        e_end = jnp.exp(cl_end - cl)
        a_s.append(_stack_heads(-kn * e_excl))
        r_s.append(_stack_heads(r * e_incl))
        v_s.append(_stack_heads(v))
        lhs.append(jnp.concatenate([a_s[-1], r_s[-1]], axis=0).astype(BF16))
        rhs.append(jnp.concatenate([_stack_heads(b * e_neg), _stack_heads(k * e_neg)], axis=0).astype(BF16))
        bk.append(jnp.concatenate([_stack_heads(b * e_end), _stack_heads(k * e_end)], axis=0).astype(BF16))
        w_end.append(jnp.exp(cl_end))

    amat = [lax.dot_general(l, rr, (((1,), (1,)), ((), ())), preferred_element_type=F32)
            for l, rr in zip(lhs, rhs)]
    a_ab = [jnp.where(strict, a[:n, :n], 0.0) for a in amat]
    a_ak = [jnp.where(strict, a[:n, n:], 0.0).astype(BF16) for a in amat]
    a_r = [jnp.concatenate([jnp.where(incl, a[n:, :n], 0.0), jnp.where(incl, a[n:, n:], 0.0)],
                           axis=1).astype(BF16) for a in amat]

    x = [eye + jnp.where(ri // 2 == ci // 2, a, 0.0) for a in a_ab]
    s = 2
    while s < t:
        level = ((ri // (2 * s)) == (ci // (2 * s))) & ((ri // s) % 2 == 1) & ((ci // s) % 2 == 0)
        c = [jnp.where(level, a, 0.0).astype(BF16) for a in a_ab]
        xb = [xi.astype(BF16) for xi in x]
        xc = [jnp.dot(xi, cc, preferred_element_type=F32).astype(BF16) for xi, cc in zip(xb, c)]
        x = [xi + jnp.dot(xci, xbi, preferred_element_type=F32) for xi, xci, xbi in zip(x, xc, xb)]
        s *= 2

    akv = [jnp.dot(a, v.astype(BF16), preferred_element_type=F32) for a, v in zip(a_ak, v_s)]
    tg = [_dot(xi, jnp.concatenate([a, kv], axis=1)) for xi, a, kv in zip(x, a_s, akv)]
    low = [jnp.concatenate([g, jnp.concatenate([jnp.zeros_like(v), v], axis=1)], axis=0).astype(BF16)
           for g, v in zip(tg, v_s)]
    out1 = [jnp.dot(a, lo, preferred_element_type=F32) for a, lo in zip(a_r, low)]
    out2 = [lax.dot_general(bb, lo, (((0,), (0,)), ((), ())), preferred_element_type=F32)
            for bb, lo in zip(bk, low)]
    ki = lax.broadcasted_iota(jnp.int32, (LANES, LANES), 0)
    kj = lax.broadcasted_iota(jnp.int32, (LANES, LANES), 1)
    terms = []
    for rs, o1, o2, we in zip(r_s, out1, out2, w_end):
        r_hat = rs + o1[:, :LANES]
        m = o2[:, :LANES] + jnp.where(ki == kj, we, 0.0)
        terms.append((r_hat, o1[:, LANES:], m, o2[:, LANES:]))
    return terms


def _wkv_kernel(r_ref, k_ref, v_ref, ld_ref, kn_ref, b_ref, y_ref, s_ref, mr_ref, nc_ref, yv_ref,
                *, rows, group):
    t = WKV_CHUNK
    npair = RWKV_WIDTH // LANES

    @pl.when(pl.program_id(1) == 0)
    def _():
        s_ref[...] = jnp.zeros_like(s_ref)

    ti = lax.broadcasted_iota(jnp.int32, (t, t), 0)
    tj = lax.broadcasted_iota(jnp.int32, (t, t), 1)
    tri = (ti >= tj).astype(F32)

    def pre_body(i, _):
        chains, where = [], []
        for j in range(group):
            c = i * group + j
            sl = pl.ds(pl.multiple_of(c * t, t), t)
            tiles = [ref[sl, :] for ref in (r_ref, k_ref, v_ref, ld_ref, kn_ref, b_ref)]
            cl = jnp.dot(tri, tiles[3], preferred_element_type=F32, precision=HIGHEST)
            for p in range(npair):
                ls = slice(p * LANES, (p + 1) * LANES)
                chains.append(tuple(a[:, ls] for a in tiles) + (cl[:, ls],))
                where.append((c, p))
        for (c, p), (r_hat, y_v, m, n_c) in zip(where, _wkv_terms(chains)):
            mr_ref[p, pl.ds(pl.multiple_of(c * 2 * STACK, 2 * STACK), 2 * STACK), :] = (
                jnp.concatenate([m, r_hat], axis=0).astype(BF16))
            rs = pl.ds(pl.multiple_of(c * STACK, STACK), STACK)
            nc_ref[p, rs, :] = n_c
            yv_ref[p, rs, :] = y_v
        return 0

    lax.fori_loop(0, rows // (t * group), pre_body, 0)

    def chain_body(c, _):
        rs = pl.ds(pl.multiple_of(c * STACK, STACK), STACK)
        outs = [jnp.dot(mr_ref[p, pl.ds(pl.multiple_of(c * 2 * STACK, 2 * STACK), 2 * STACK), :],
                        s_ref[p].astype(BF16), preferred_element_type=F32) for p in range(npair)]
        for p, out in enumerate(outs):
            s_ref[p] = out[:LANES] + nc_ref[p, rs, :]
            ys = out[LANES:] + yv_ref[p, rs, :]
            y_ref[pl.ds(pl.multiple_of(c * t, t), t), p * LANES:(p + 1) * LANES] = ys[:t] + ys[t:]
        return 0

    lax.fori_loop(0, rows // t, chain_body, 0)


def _wkv(r, k, v, ld, kn, b, *, bsz, seq, rows):
    n = bsz * seq
    nt = seq // rows
    npair = RWKV_WIDTH // LANES
    nchunk = rows // WKV_CHUNK
    spec = pl.BlockSpec((rows, RWKV_WIDTH), lambda bi, ti: (bi * nt + ti, 0))
    return pl.pallas_call(
        functools.partial(_wkv_kernel, rows=rows, group=2),
        grid=(bsz, nt),
        in_specs=[spec] * 6,
        out_specs=spec,
        out_shape=jax.ShapeDtypeStruct((n, RWKV_WIDTH), F32),
        scratch_shapes=[pltpu.VMEM((npair, LANES, LANES), F32),
                        pltpu.VMEM((npair, nchunk * 2 * STACK, LANES), BF16),
                        pltpu.VMEM((npair, nchunk * STACK, LANES), F32),
                        pltpu.VMEM((npair, nchunk * STACK, LANES), F32)],
        compiler_params=_params("arbitrary", "arbitrary"),
        name="wkv",
    )(r, k, v, ld, kn, b)


def _gelu_tanh(x):
    return 0.5 * x * (1.0 + jnp.tanh(math.sqrt(2.0 / math.pi) * (x + 0.044715 * x * x * x)))


def _s5_kernel(u_ref, wre_ref, wim_ref, cc_ref, ar_ref, ai_ref, d_ref, y_ref,
               xre, xim, st_re, st_im, *, steps):
    @pl.when(pl.program_id(1) == 0)
    def _():
        st_re[...] = jnp.zeros_like(st_re)
        st_im[...] = jnp.zeros_like(st_im)

    rows = steps * SUBLANES
    u = u_ref[...].reshape(rows, S5_WIDTH)
    ub = u.astype(BF16)
    ns = S5_SLAB_STATE
    for j in range(S5_SLABS):
        uj = ub[:, j * LANES:(j + 1) * LANES]
        xre[:, j * ns:(j + 1) * ns] = jnp.dot(uj, wre_ref[j], preferred_element_type=F32)
        xim[:, j * ns:(j + 1) * ns] = jnp.dot(uj, wim_ref[j], preferred_element_type=F32)

    half = S5_GROUPS * S5_STATE // 2
    for c in range(2):
        cs = slice(c * half, (c + 1) * half)
        ar = jnp.broadcast_to(ar_ref[:, cs], (SUBLANES, half))
        ai = jnp.broadcast_to(ai_ref[:, cs], (SUBLANES, half))

        def step(i, state, cs=cs, ar=ar, ai=ai):
            x_re, x_im = state
            rs = pl.ds(pl.multiple_of(i * SUBLANES, SUBLANES), SUBLANES)
            n_re = ar * x_re - ai * x_im + xre[rs, cs]
            n_im = ar * x_im + ai * x_re + xim[rs, cs]
            xre[rs, cs] = n_re
            xim[rs, cs] = n_im
            return n_re, n_im

        x_re, x_im = lax.fori_loop(0, steps, step, (st_re[:, cs], st_im[:, cs]), unroll=2)
        st_re[:, cs] = x_re
        st_im[:, cs] = x_im

    for j in range(S5_SLABS):
        xs = jnp.concatenate([xre[:, j * ns:(j + 1) * ns], xim[:, j * ns:(j + 1) * ns]], axis=1)
        ls = slice(j * LANES, (j + 1) * LANES)
        yj = jnp.dot(xs.astype(BF16), cc_ref[j], preferred_element_type=F32) + d_ref[:, ls] * u[:, ls]
        y_ref[:, :, ls] = _gelu_tanh(yj).reshape(steps, SUBLANES, LANES)


def _s5(u_t, wre, wim, cc, ar, ai, d, *, bsz, seq, steps):
    nstate = S5_GROUPS * S5_STATE
    rows = steps * SUBLANES
    spec = pl.BlockSpec((steps, SUBLANES, S5_WIDTH), lambda g, t: (t, g, 0))
    return pl.pallas_call(
        functools.partial(_s5_kernel, steps=steps),
        grid=(bsz // SUBLANES, seq // steps),
        in_specs=[spec,
                  _const_spec((S5_SLABS, LANES, S5_SLAB_STATE)),
                  _const_spec((S5_SLABS, LANES, S5_SLAB_STATE)),
                  _const_spec((S5_SLABS, 2 * S5_SLAB_STATE, LANES)),
                  _const_spec((1, nstate)), _const_spec((1, nstate)),
                  _const_spec((1, S5_WIDTH))],
        out_specs=spec,
        out_shape=jax.ShapeDtypeStruct((seq, bsz, S5_WIDTH), F32),
        scratch_shapes=[pltpu.VMEM((rows, nstate), F32), pltpu.VMEM((rows, nstate), F32),
                        pltpu.VMEM((SUBLANES, nstate), F32), pltpu.VMEM((SUBLANES, nstate), F32)],
        compiler_params=_params("arbitrary", "arbitrary"),
        name="s5",
    )(u_t, wre, wim, cc, ar, ai, d)


def _merge_kernel(y_ref, g_ref, bg_ref, ys5_ref, gates_ref, x_ref, mod_ref, lng_ref, lnb_ref,
                  gmean_ref, wrw_ref, wglu_ref, wout_ref, o_ref):
    gm = gmean_ref[...]
    y = y_ref[...]
    mean = _dot_split(y, gm)
    d = y - mean
    var = _dot_split(d * d, gm)
    yn = d * lax.rsqrt(var + GN_EPS) * lng_ref[...] + lnb_ref[...]
    ya = _dot(yn * g_ref[...] + bg_ref[...], wrw_ref[...])
    z = _dot(ys5_ref[...], wglu_ref[...])
    yb = z[:, :D_MODEL] * jax.nn.sigmoid(z[:, D_MODEL:])
    gates = gates_ref[...]
    mixed = _dot(gates[:, :D_MODEL] * ya + gates[:, D_MODEL:] * yb, wout_ref[...])
    gt1 = mod_ref[...][2:3]
    o_ref[...] = x_ref[...] + gt1 * mixed


def _merge(y, g, bg, ys5, gates, x2, mod3, lng, lnb, gmean, wrw, wglu, wout, *, bsz, seq, rows):
    n = bsz * seq
    nt = seq // rows
    w = RWKV_WIDTH
    row_spec = lambda cols: pl.BlockSpec((rows, cols), lambda i: (i, 0))
    return pl.pallas_call(
        _merge_kernel,
        grid=(n // rows,),
        in_specs=[row_spec(w), row_spec(w), row_spec(w), row_spec(w), row_spec(2 * D_MODEL),
                  row_spec(D_MODEL),
                  pl.BlockSpec((None, 6, D_MODEL), lambda i: (i // nt, 0, 0)),
                  _const_spec((1, w)), _const_spec((1, w)), _const_spec((w, w)),
                  _const_spec((w, D_MODEL)), _const_spec((S5_WIDTH, 2 * D_MODEL)),
                  _const_spec((D_MODEL, D_MODEL))],
        out_specs=row_spec(D_MODEL),
        out_shape=jax.ShapeDtypeStruct((n, D_MODEL), F32),
        compiler_params=_params("arbitrary"),
        name="merge",
    )(y, g, bg, ys5, gates, x2, mod3, lng, lnb, gmean, wrw, wglu, wout)


def _ffn_kernel(x_ref, mod_ref, g2_ref, wup_ref, cw_ref, cb_ref, wdn_ref, gf_ref, o_ref, buf, *, rows):
    pad = SUBLANES

    @pl.when(pl.program_id(1) == 0)
    def _():
        buf[:, 0:pad, :] = jnp.zeros((FFN_SPLIT, pad, 2 * FFN_CH), F32)

    x = x_ref[...]
    mod = mod_ref[...]
    sh2, sc2, gt2 = mod[3:4], mod[4:5], mod[5:6]
    ms = jnp.mean(x * x, axis=-1, keepdims=True)
    h = x * lax.rsqrt(ms + RMS_EPS) * g2_ref[...]
    hb = (h * (1.0 + sc2) + sh2).astype(BF16)

    acc = jnp.zeros((rows, D_MODEL), F32)
    for c in range(FFN_SPLIT):
        buf[c, pad:pad + rows, :] = jnp.dot(hb, wup_ref[c], preferred_element_type=F32)
        cw = cw_ref[c]
        hid = cb_ref[c]
        for j in range(CONV_W):
            off = pad - (CONV_W - 1) + j
            hid = hid + cw[j:j + 1, :] * buf[c, off:off + rows, :]
        buf[c, pad - (CONV_W - 1):pad, :] = buf[c, pad + rows - (CONV_W - 1):pad + rows, :]
        gate, up = hid[:, :FFN_CH], hid[:, FFN_CH:]
        act = gate * jax.nn.sigmoid(gate) * up
        acc = acc + jnp.dot(act.astype(BF16), wdn_ref[c], preferred_element_type=F32)

    x2 = x + gt2 * acc
    ms2 = jnp.mean(x2 * x2, axis=-1, keepdims=True)
    o_ref[...] = x2 * lax.rsqrt(ms2 + RMS_EPS) * gf_ref[...]


def _ffn(x1, mod3, g2, wup, cw, cb, wdn, gf, *, bsz, seq, rows):
    n = bsz * seq
    nt = seq // rows
    row_spec = pl.BlockSpec((rows, D_MODEL), lambda b, t: (b * nt + t, 0))
    return pl.pallas_call(
        functools.partial(_ffn_kernel, rows=rows),
        grid=(bsz, nt),
        in_specs=[row_spec,
                  pl.BlockSpec((None, 6, D_MODEL), lambda b, t: (b, 0, 0)),
                  _const_spec((1, D_MODEL)),
                  _const_spec((FFN_SPLIT, D_MODEL, 2 * FFN_CH)),
                  _const_spec((FFN_SPLIT, CONV_W, 2 * FFN_CH)),
                  _const_spec((FFN_SPLIT, 1, 2 * FFN_CH)),
                  _const_spec((FFN_SPLIT, FFN_CH, D_MODEL)),
                  _const_spec((1, D_MODEL))],
        out_specs=row_spec,
        out_shape=jax.ShapeDtypeStruct((n, D_MODEL), F32),
        scratch_shapes=[pltpu.VMEM((FFN_SPLIT, rows + SUBLANES, 2 * FFN_CH), F32)],
        compiler_params=_params("arbitrary", "arbitrary"),
        name="ffn",
    )(x1, mod3, g2, wup, cw, cb, wdn, gf)


def _split_cols(a):
    parts = [jnp.concatenate([a[..., c * FFN_CH:(c + 1) * FFN_CH],
                              a[..., D_FF + c * FFN_CH:D_FF + (c + 1) * FFN_CH]], axis=-1)
             for c in range(FFN_SPLIT)]
    return jnp.stack(parts, axis=0)


def _block_diag_ones(n, blk, value, dtype):
    i = jnp.arange(n) // blk
    return jnp.where(i[:, None] == i[None, :], value, 0.0).astype(dtype)


def kernel(x, c, w_ada, b_ada, norm1_g, w_in, mu_shift, rwkv_w0, rwkv_w_up, rwkv_a0, rwkv_a_up,
           rwkv_g_up, rwkv_k_k, rwkv_k_a, rwkv_r_k, rwkv_ln_g, rwkv_ln_b, w_out_rwkv, s5_a_re,
           s5_a_im, s5_log_dt, s5_b_re, s5_b_im, s5_c_re, s5_c_im, s5_d, w_glu, w_out, norm2_g,
           w_ffn_up, ffn_conv_w, ffn_conv_b, w_ffn_down, norm_f_g):
    bsz, seq, _ = x.shape
    depth = w_ada.shape[0]
    n = bsz * seq
    w = RWKV_WIDTH
    rows_a = min(256, seq)
    rows_m = min(256, seq)
    rows_f = min(256, seq)
    s5_steps = min(64, seq)

    g64 = _block_diag_ones(w, RWKV_HEAD, 1.0, BF16)
    gmean = _block_diag_ones(w, RWKV_HEAD, 1.0 / RWKV_HEAD, BF16)
    eye8 = jnp.eye(LANES // S5_GROUP, dtype=F32)
    row = lambda a: a.reshape(1, -1)

    assert depth == 1, depth
    x2 = x.reshape(n, D_MODEL)
    for l in range(depth):
        mod3 = _adaln(c, w_ada[l], b_ada[l]).reshape(bsz, 6, D_MODEL)

        zero = jnp.zeros((LORA_W, w), F32)
        waup = jnp.concatenate([jnp.concatenate([rwkv_w_up[l], zero], axis=1),
                                jnp.concatenate([zero, rwkv_a_up[l]], axis=1)], axis=0).astype(BF16)
        r, k, v, ld, kn, b, g, bg, u, gates = _inproj(
            x2, mod3, row(norm1_g[l]), w_in[l].astype(BF16), row(mu_shift[l]), row(rwkv_w0[l]), waup,
            row(rwkv_a0[l]), rwkv_g_up[l].astype(BF16), row(rwkv_k_k[l]), row(rwkv_k_a[l]),
            row(rwkv_r_k[l]), g64, bsz=bsz, seq=seq, rows=rows_a)

        y = _wkv(r, k, v, ld, kn, b, bsz=bsz, seq=seq, rows=min(512, seq))

        ab_re, ab_im, bb_re, bb_im = _s5prep(s5_a_re[l], s5_a_im[l], s5_log_dt[l], s5_b_re[l], s5_b_im[l])
        gl = LANES // S5_GROUP
        slab_in = lambda bb: jnp.einsum('jgcp,gh->jgchp', bb.reshape(S5_SLABS, gl, S5_GROUP, S5_STATE),
                                        eye8).reshape(S5_SLABS, LANES, S5_SLAB_STATE).astype(BF16)
        slab_out = lambda cc: jnp.einsum('jgcp,gh->jgphc', cc.reshape(S5_SLABS, gl, S5_GROUP, S5_STATE),
                                         eye8).reshape(S5_SLABS, S5_SLAB_STATE, LANES)
        cc = jnp.concatenate([slab_out(s5_c_re[l]), -slab_out(s5_c_im[l])], axis=1).astype(BF16)
        u_t = jnp.transpose(u.reshape(bsz, seq, S5_WIDTH), (1, 0, 2))
        ys5_t = _s5(u_t, slab_in(bb_re), slab_in(bb_im), cc, row(ab_re), row(ab_im), row(s5_d[l]),
                    bsz=bsz, seq=seq, steps=s5_steps)
        ys5 = jnp.transpose(ys5_t, (1, 0, 2)).reshape(n, S5_WIDTH)

        x1 = _merge(y, g, bg, ys5, gates, x2, mod3, row(rwkv_ln_g[l]), row(rwkv_ln_b[l]), gmean,
                    w_out_rwkv[l].astype(BF16), w_glu[l].astype(BF16), w_out[l].astype(BF16),
                    bsz=bsz, seq=seq, rows=rows_m)

        gf = row(norm_f_g)
        wdn = w_ffn_down[l].astype(BF16).reshape(FFN_SPLIT, FFN_CH, D_MODEL)
        x2 = _ffn(x1, mod3, row(norm2_g[l]), _split_cols(w_ffn_up[l]).astype(BF16),
                  _split_cols(ffn_conv_w[l]), _split_cols(ffn_conv_b[l].reshape(1, -1)), wdn,
                  gf, bsz=bsz, seq=seq, rows=rows_f)
    return x2.reshape(bsz, seq, D_MODEL)
```

```python
import functools
import math

import jax
import jax.numpy as jnp
from jax import lax
from jax.experimental import pallas as pl
from jax.experimental.pallas import tpu as pltpu

F32 = jnp.float32
BF16 = jnp.bfloat16
HIGHEST = lax.Precision.HIGHEST

D_MODEL = 1024
RWKV_WIDTH = 512
RWKV_HEAD = 64
LORA_W = 64
LORA_A = 64
LORA_G = 128
S5_WIDTH = 512
S5_GROUP = 16
S5_GROUPS = 32
S5_STATE = 64
D_FF = 2816
CONV_W = 3
RMS_EPS = 1e-6
GN_EPS = 64e-5
L2_EPS = 1e-12
SHIFT_COLS = 3 * RWKV_WIDTH + LORA_W + LORA_A + LORA_G
IN_COLS = SHIFT_COLS + S5_WIDTH + 2 * D_MODEL

LANES = 128
SUBLANES = 8
WKV_CHUNK = 64
HEADS_PER_TILE = LANES // RWKV_HEAD
STACK = HEADS_PER_TILE * WKV_CHUNK
S5_SLABS = S5_WIDTH // LANES
S5_SLAB_STATE = (LANES // S5_GROUP) * S5_STATE
FFN_SPLIT = 2
FFN_CH = D_FF // FFN_SPLIT
VMEM_LIMIT = 56 * 1024 * 1024


def _dot(a, b):
    return jnp.dot(a.astype(BF16), b.astype(BF16), preferred_element_type=F32)


def _dot_split(a, g):
    hi = a.astype(BF16)
    lo = (a - hi.astype(F32)).astype(BF16)
    return (jnp.dot(hi, g, preferred_element_type=F32)
            + jnp.dot(lo, g, preferred_element_type=F32))


def _aligned(i, m):
    return i if isinstance(i, int) else pl.multiple_of(i, m)


def _const_spec(shape):
    nd = len(shape)
    return pl.BlockSpec(shape, lambda *_: (0,) * nd, pipeline_mode=pl.Buffered(1))


def _params(*sem):
    return pltpu.CompilerParams(dimension_semantics=sem, vmem_limit_bytes=VMEM_LIMIT)


def _adaln_kernel(c_ref, w_ref, b_ref, o_ref):
    c = c_ref[...]
    s = c * jax.nn.sigmoid(c)
    o_ref[...] = jnp.dot(s, w_ref[...], preferred_element_type=F32, precision=HIGHEST) + b_ref[...]


def _adaln(c, w_ada, b_ada):
    bsz = c.shape[0]
    n = w_ada.shape[1]
    bn = D_MODEL
    return pl.pallas_call(
        _adaln_kernel,
        grid=(n // bn,),
        in_specs=[pl.BlockSpec((bsz, D_MODEL), lambda j: (0, 0)),
                  pl.BlockSpec((D_MODEL, bn), lambda j: (0, j)),
                  pl.BlockSpec((1, bn), lambda j: (0, j))],
        out_specs=pl.BlockSpec((bsz, bn), lambda j: (0, j)),
        out_shape=jax.ShapeDtypeStruct((bsz, n), F32),
        compiler_params=_params("arbitrary"),
        name="adaln",
    )(c, w_ada, b_ada.reshape(1, n))


def _s5prep_kernel(are_ref, aim_ref, ldt_ref, bre_ref, bim_ref, abre_o, abim_o, bbre_o, bbim_o):
    a_re = are_ref[...]
    a_im = aim_ref[...]
    dt = jnp.exp(ldt_ref[...])
    z_re, z_im = a_re * dt, a_im * dt
    mag = jnp.exp(z_re)
    ab_re, ab_im = mag * jnp.cos(z_im), mag * jnp.sin(z_im)
    den = a_re * a_re + a_im * a_im
    q_re = ((ab_re - 1.0) * a_re + ab_im * a_im) / den
    q_im = (ab_im * a_re - (ab_re - 1.0) * a_im) / den
    abre_o[...] = ab_re
    abim_o[...] = ab_im
    b_re = bre_ref[...]
    b_im = bim_ref[...]
    bbre_o[...] = q_re[:, None, :] * b_re - q_im[:, None, :] * b_im
    bbim_o[...] = q_re[:, None, :] * b_im + q_im[:, None, :] * b_re


def _s5prep(a_re, a_im, log_dt, b_re, b_im):
    g, p = a_re.shape
    c = b_re.shape[-1]
    b_re_t = jnp.transpose(b_re, (0, 2, 1))
    b_im_t = jnp.transpose(b_im, (0, 2, 1))
    return pl.pallas_call(
        _s5prep_kernel,
        out_shape=(jax.ShapeDtypeStruct((g, p), F32), jax.ShapeDtypeStruct((g, p), F32),
                   jax.ShapeDtypeStruct((g, c, p), F32), jax.ShapeDtypeStruct((g, c, p), F32)),
        name="s5prep",
    )(a_re, a_im, log_dt.reshape(g, 1), b_re_t, b_im_t)


def _inproj_kernel(x_ref, mod_ref, g1_ref, win_ref, mu_ref, w0_ref, waup_ref, a0_ref, gup_ref,
                   kk_ref, ka_ref, rk_ref, g64_ref,
                   r_o, k_o, v_o, ld_o, kn_o, b_o, g_o, bg_o, u_o, gates_o, carry, *, rows):
    @pl.when(pl.program_id(1) == 0)
    def _():
        carry[...] = jnp.zeros_like(carry)

    x = x_ref[...]
    mod = mod_ref[...]
    sh1, sc1 = mod[0:1], mod[1:2]
    ms = jnp.mean(x * x, axis=-1, keepdims=True)
    h = x * lax.rsqrt(ms + RMS_EPS) * g1_ref[...]
    h = h * (1.0 + sc1) + sh1
    proj = jnp.dot(h.astype(BF16), win_ref[...], preferred_element_type=F32)

    p = proj[:, :SHIFT_COLS]
    row = lax.broadcasted_iota(jnp.int32, p.shape, 0)
    prev = jnp.where(row == 0, carry[...], pltpu.roll(p, 1, axis=0))
    carry[...] = p[rows - 1:rows, :]
    ps = p + (prev - p) * mu_ref[...]

    w = RWKV_WIDTH
    r, k, v = ps[:, 0:w], ps[:, w:2 * w], ps[:, 2 * w:3 * w]
    wa = ps[:, 3 * w:3 * w + LORA_W + LORA_A]
    lane = lax.broadcasted_iota(jnp.int32, wa.shape, 1)
    wa = jnp.where(lane < LORA_W, jnp.tanh(wa), wa)
    delta = _dot(wa, waup_ref[...])
    w_raw = w0_ref[...] + delta[:, :w]
    ld = -math.exp(-0.5) * jax.nn.sigmoid(w_raw)
    eta = jax.nn.sigmoid(a0_ref[...] + delta[:, w:])
    gd = ps[:, 3 * w + LORA_W + LORA_A:SHIFT_COLS]
    g = _dot(jax.nn.sigmoid(gd), gup_ref[...])

    g64 = g64_ref[...]
    kk = k * kk_ref[...]
    kn = kk * lax.rsqrt(_dot_split(kk * kk, g64) + L2_EPS)
    k2 = k * (1.0 + (eta - 1.0) * ka_ref[...])
    bonus = _dot_split(r * k2 * rk_ref[...], g64) * v

    r_o[...] = r
    k_o[...] = k2
    v_o[...] = v
    ld_o[...] = ld
    kn_o[...] = kn
    b_o[...] = kn * eta
    g_o[...] = g
    bg_o[...] = bonus * g
    u_o[...] = proj[:, SHIFT_COLS:SHIFT_COLS + S5_WIDTH]
    gates_o[...] = jax.nn.sigmoid(proj[:, SHIFT_COLS + S5_WIDTH:])


def _inproj(x2, mod3, g1, win, mu, w0, waup, a0, gup, k_k, k_a, r_k, g64, *, bsz, seq, rows):
    n = bsz * seq
    nt = seq // rows
    w = RWKV_WIDTH
    row_spec = lambda cols: pl.BlockSpec((rows, cols), lambda b, t: (b * nt + t, 0))
    outs = [jax.ShapeDtypeStruct((n, w), F32)] * 9 + [jax.ShapeDtypeStruct((n, 2 * D_MODEL), F32)]
    return pl.pallas_call(
        functools.partial(_inproj_kernel, rows=rows),
        grid=(bsz, nt),
        in_specs=[row_spec(D_MODEL),
                  pl.BlockSpec((None, 6, D_MODEL), lambda b, t: (b, 0, 0)),
                  _const_spec((1, D_MODEL)),
                  _const_spec((D_MODEL, IN_COLS)),
                  _const_spec((1, SHIFT_COLS)),
                  _const_spec((1, w)),
                  _const_spec((LORA_W + LORA_A, 2 * w)),
                  _const_spec((1, w)),
                  _const_spec((LORA_G, w)),
                  _const_spec((1, w)), _const_spec((1, w)), _const_spec((1, w)),
                  _const_spec((w, w))],
        out_specs=[row_spec(w)] * 9 + [row_spec(2 * D_MODEL)],
        out_shape=outs,
        scratch_shapes=[pltpu.VMEM((1, SHIFT_COLS), F32)],
        compiler_params=_params("arbitrary", "arbitrary"),
        name="inproj",
    )(x2, mod3, g1, win, mu, w0, waup, a0, gup, k_k, k_a, r_k, g64)


def _stack_heads(z):
    lane = lax.broadcasted_iota(jnp.int32, z.shape, 1)
    first = lane < RWKV_HEAD
    return jnp.concatenate([jnp.where(first, z, 0.0), jnp.where(first, 0.0, z)], axis=0)


def _cumsum_rows(x):
    n = x.shape[0]
    row = lax.broadcasted_iota(jnp.int32, x.shape, 0)
    d = 1
    while d < n:
        if d < SUBLANES:
            shifted = jnp.where(row >= d, pltpu.roll(x, d, axis=0), 0.0)
        else:
            shifted = jnp.concatenate([jnp.zeros((d, x.shape[1]), x.dtype), x[:n - d]], axis=0)
        x = x + shifted
        d *= 2
    return x


_WKV_STAGE_BOUNDARIES = 8


def _wkv_terms(chains, side=()):
    t = WKV_CHUNK
    n = STACK
    jobs = list(side)
    calls = [0]

    def between():
        k = calls[0]
        calls[0] += 1
        for _ in range((k + 1) * len(side) // _WKV_STAGE_BOUNDARIES - k * len(side) // _WKV_STAGE_BOUNDARIES):
            jobs.pop(0)()

    ri = lax.broadcasted_iota(jnp.int32, (n, n), 0)
    ci = lax.broadcasted_iota(jnp.int32, (n, n), 1)
    same_head = (ri // t) == (ci // t)
    strict = same_head & (ri > ci)
    incl = same_head & (ri >= ci)
    eye = jnp.where(ri == ci, 1.0, 0.0)

    a_s, r_s, v_s, lhs, rhs, bk, w_end = [], [], [], [], [], [], []
    for r, k, v, ld, kn, b, cl in chains:
        cl_end = cl[t - 1:t, :]
        e_incl = jnp.exp(cl)
        e_excl = jnp.exp(cl - ld)
        e_neg = jnp.exp(-cl)
        e_end = jnp.exp(cl_end - cl)
        a_s.append(_stack_heads(-kn * e_excl))
        r_s.append(_stack_heads(r * e_incl))
        v_s.append(_stack_heads(v))
        lhs.append(jnp.concatenate([a_s[-1], r_s[-1]], axis=0).astype(BF16))
        rhs.append(jnp.concatenate([_stack_heads(b * e_neg), _stack_heads(k * e_neg)], axis=0).astype(BF16))
        bk.append(jnp.concatenate([_stack_heads(b * e_end), _stack_heads(k * e_end)], axis=0).astype(BF16))
        w_end.append(jnp.exp(cl_end))

    amat = [lax.dot_general(l, rr, (((1,), (1,)), ((), ())), preferred_element_type=F32)
            for l, rr in zip(lhs, rhs)]
    a_ab = [jnp.where(strict, a[:n, :n], 0.0) for a in amat]
    a_ak = [jnp.where(strict, a[:n, n:], 0.0).astype(BF16) for a in amat]
    a_r = [jnp.concatenate([jnp.where(incl, a[n:, :n], 0.0), jnp.where(incl, a[n:, n:], 0.0)],
                           axis=1).astype(BF16) for a in amat]

    x = [eye + jnp.where(ri // 2 == ci // 2, a, 0.0) for a in a_ab]
    between()
    s = 2
    while s < t:
        level = ((ri // (2 * s)) == (ci // (2 * s))) & ((ri // s) % 2 == 1) & ((ci // s) % 2 == 0)
        c = [jnp.where(level, a, 0.0).astype(BF16) for a in a_ab]
        xb = [xi.astype(BF16) for xi in x]
        if s < SUBLANES:
            xc = [jnp.dot(xi, cc, preferred_element_type=F32).astype(BF16) for xi, cc in zip(xb, c)]
            x = [xi + jnp.dot(xci, xbi, preferred_element_type=F32) for xi, xci, xbi in zip(x, xc, xb)]
        else:
            odd = lambda m: jnp.concatenate([m[q * s:(q + 1) * s] for q in range(1, n // s, 2)], axis=0)
            xo = [odd(xi) for xi in x]
            xc = [jnp.dot(xoi.astype(BF16), cc, preferred_element_type=F32).astype(BF16)
                  for xoi, cc in zip(xo, c)]
            xo = [xoi + jnp.dot(xci, xbi, preferred_element_type=F32) for xoi, xci, xbi in zip(xo, xc, xb)]
            x = [jnp.concatenate([xoi[(q // 2) * s:(q // 2 + 1) * s] if q % 2 else xi[q * s:(q + 1) * s]
                                  for q in range(n // s)], axis=0) for xi, xoi in zip(x, xo)]
        between()
        s *= 2

    akv = [jnp.dot(a, v.astype(BF16), preferred_element_type=F32) for a, v in zip(a_ak, v_s)]
    tg = [_dot(xi, jnp.concatenate([a, kv], axis=1)) for xi, a, kv in zip(x, a_s, akv)]
    between()
    low = [jnp.concatenate([g, jnp.concatenate([jnp.zeros_like(v), v], axis=1)], axis=0).astype(BF16)
           for g, v in zip(tg, v_s)]
    out1 = [jnp.dot(a, lo, preferred_element_type=F32) for a, lo in zip(a_r, low)]
    out2 = [lax.dot_general(bb, lo, (((0,), (0,)), ((), ())), preferred_element_type=F32)
            for bb, lo in zip(bk, low)]
    between()
    ki = lax.broadcasted_iota(jnp.int32, (LANES, LANES), 0)
    kj = lax.broadcasted_iota(jnp.int32, (LANES, LANES), 1)
    terms = []
    for rs, o1, o2, we in zip(r_s, out1, out2, w_end):
        r_hat = rs + o1[:, :LANES]
        m = o2[:, :LANES] + jnp.where(ki == kj, we, 0.0)
        terms.append((r_hat, o1[:, LANES:], m, o2[:, LANES:]))
    return terms


def _wkv_kernel(r_ref, k_ref, v_ref, ld_ref, kn_ref, b_ref, y_ref, s_ref, mr_ref, nc_ref, yv_ref,
                *, rows, group):
    t = WKV_CHUNK
    npair = RWKV_WIDTH // LANES

    @pl.when(pl.program_id(1) == 0)
    def _():
        s_ref[...] = jnp.zeros_like(s_ref)

    def chain_step(c):
        rs = pl.ds(_aligned(c * STACK, STACK), STACK)
        outs = [jnp.dot(mr_ref[p, pl.ds(_aligned(c * 2 * STACK, 2 * STACK), 2 * STACK), :],
                        s_ref[p].astype(BF16), preferred_element_type=F32) for p in range(npair)]
        for p, out in enumerate(outs):
            s_ref[p] = out[:LANES] + nc_ref[p, rs, :]
            ys = out[LANES:] + yv_ref[p, rs, :]
            y_ref[pl.ds(_aligned(c * t, t), t), p * LANES:(p + 1) * LANES] = ys[:t] + ys[t:]

    def precompute(g, side):
        chains, where = [], []
        for j in range(group):
            c = g * group + j
            sl = pl.ds(_aligned(c * t, t), t)
            tiles = [ref[sl, :] for ref in (r_ref, k_ref, v_ref, ld_ref, kn_ref, b_ref)]
            cl = _cumsum_rows(tiles[3])
            for p in range(npair):
                ls = slice(p * LANES, (p + 1) * LANES)
                chains.append(tuple(a[:, ls] for a in tiles) + (cl[:, ls],))
                where.append((c, p))
        for (c, p), (r_hat, y_v, m, n_c) in zip(where, _wkv_terms(chains, side)):
            mr_ref[p, pl.ds(_aligned(c * 2 * STACK, 2 * STACK), 2 * STACK), :] = (
                jnp.concatenate([m, r_hat], axis=0).astype(BF16))
            rs = pl.ds(_aligned(c * STACK, STACK), STACK)
            nc_ref[p, rs, :] = n_c
            yv_ref[p, rs, :] = y_v

    ngroup = rows // (t * group)
    precompute(0, ())

    def body(g, _):
        precompute(g, [functools.partial(chain_step, (g - 1) * group + j) for j in range(group)])
        return 0

    lax.fori_loop(1, ngroup, body, 0)
    for j in range(group):
        chain_step((ngroup - 1) * group + j)


def _wkv(r, k, v, ld, kn, b, *, bsz, seq, rows):
    n = bsz * seq
    nt = seq // rows
    npair = RWKV_WIDTH // LANES
    nchunk = rows // WKV_CHUNK
    spec = pl.BlockSpec((rows, RWKV_WIDTH), lambda bi, ti: (bi * nt + ti, 0))
    return pl.pallas_call(
        functools.partial(_wkv_kernel, rows=rows, group=4),
        grid=(bsz, nt),
        in_specs=[spec] * 6,
        out_specs=spec,
        out_shape=jax.ShapeDtypeStruct((n, RWKV_WIDTH), F32),
        scratch_shapes=[pltpu.VMEM((npair, LANES, LANES), F32),
                        pltpu.VMEM((npair, nchunk * 2 * STACK, LANES), BF16),
                        pltpu.VMEM((npair, nchunk * STACK, LANES), F32),
                        pltpu.VMEM((npair, nchunk * STACK, LANES), F32)],
        compiler_params=_params("arbitrary", "arbitrary"),
        name="wkv",
    )(r, k, v, ld, kn, b)


def _gelu_tanh(x):
    return 0.5 * x * (1.0 + jnp.tanh(math.sqrt(2.0 / math.pi) * (x + 0.044715 * x * x * x)))


def _s5_kernel(u_ref, wre_ref, wim_ref, cc_ref, ar_ref, ai_ref, d_ref, y_ref,
               xre, xim, st_re, st_im, *, steps):
    @pl.when(pl.program_id(1) == 0)
    def _():
        st_re[...] = jnp.zeros_like(st_re)
        st_im[...] = jnp.zeros_like(st_im)

    rows = steps * SUBLANES
    u = u_ref[...].reshape(rows, S5_WIDTH)
    ub = u.astype(BF16)
    ns = S5_SLAB_STATE
    for j in range(S5_SLABS):
        uj = ub[:, j * LANES:(j + 1) * LANES]
        xre[:, j * ns:(j + 1) * ns] = jnp.dot(uj, wre_ref[j], preferred_element_type=F32)
        xim[:, j * ns:(j + 1) * ns] = jnp.dot(uj, wim_ref[j], preferred_element_type=F32)

    half = S5_GROUPS * S5_STATE // 2
    for c in range(2):
        cs = slice(c * half, (c + 1) * half)
        ar = jnp.broadcast_to(ar_ref[:, cs], (SUBLANES, half))
        ai = jnp.broadcast_to(ai_ref[:, cs], (SUBLANES, half))

        def step(i, state, cs=cs, ar=ar, ai=ai):
            x_re, x_im = state
            rs = pl.ds(pl.multiple_of(i * SUBLANES, SUBLANES), SUBLANES)
            n_re = ar * x_re - ai * x_im + xre[rs, cs]
            n_im = ar * x_im + ai * x_re + xim[rs, cs]
            xre[rs, cs] = n_re
            xim[rs, cs] = n_im
            return n_re, n_im

        x_re, x_im = lax.fori_loop(0, steps, step, (st_re[:, cs], st_im[:, cs]), unroll=2)
        st_re[:, cs] = x_re
        st_im[:, cs] = x_im

    for j in range(S5_SLABS):
        xs = jnp.concatenate([xre[:, j * ns:(j + 1) * ns], xim[:, j * ns:(j + 1) * ns]], axis=1)
        ls = slice(j * LANES, (j + 1) * LANES)
        yj = jnp.dot(xs.astype(BF16), cc_ref[j], preferred_element_type=F32) + d_ref[:, ls] * u[:, ls]
        y_ref[:, :, ls] = _gelu_tanh(yj).reshape(steps, SUBLANES, LANES)


def _s5(u_t, wre, wim, cc, ar, ai, d, *, bsz, seq, steps):
    nstate = S5_GROUPS * S5_STATE
    rows = steps * SUBLANES
    spec = pl.BlockSpec((steps, SUBLANES, S5_WIDTH), lambda g, t: (t, g, 0))
    return pl.pallas_call(
        functools.partial(_s5_kernel, steps=steps),
        grid=(bsz // SUBLANES, seq // steps),
        in_specs=[spec,
                  _const_spec((S5_SLABS, LANES, S5_SLAB_STATE)),
                  _const_spec((S5_SLABS, LANES, S5_SLAB_STATE)),
                  _const_spec((S5_SLABS, 2 * S5_SLAB_STATE, LANES)),
                  _const_spec((1, nstate)), _const_spec((1, nstate)),
                  _const_spec((1, S5_WIDTH))],
        out_specs=spec,
        out_shape=jax.ShapeDtypeStruct((seq, bsz, S5_WIDTH), F32),
        scratch_shapes=[pltpu.VMEM((rows, nstate), F32), pltpu.VMEM((rows, nstate), F32),
                        pltpu.VMEM((SUBLANES, nstate), F32), pltpu.VMEM((SUBLANES, nstate), F32)],
        compiler_params=_params("arbitrary", "arbitrary"),
        name="s5",
    )(u_t, wre, wim, cc, ar, ai, d)


def _merge_kernel(y_ref, g_ref, bg_ref, ys5_ref, gates_ref, x_ref, mod_ref, lng_ref, lnb_ref,
                  gmean_ref, wrw_ref, wglu_ref, wout_ref, o_ref):
    gm = gmean_ref[...]
    y = y_ref[...]
    mean = _dot_split(y, gm)
    d = y - mean
    var = _dot_split(d * d, gm)
    yn = d * lax.rsqrt(var + GN_EPS) * lng_ref[...] + lnb_ref[...]
    ya = _dot(yn * g_ref[...] + bg_ref[...], wrw_ref[...])
    z = _dot(ys5_ref[...], wglu_ref[...])
    yb = z[:, :D_MODEL] * jax.nn.sigmoid(z[:, D_MODEL:])
    gates = gates_ref[...]
    mixed = _dot(gates[:, :D_MODEL] * ya + gates[:, D_MODEL:] * yb, wout_ref[...])
    gt1 = mod_ref[...][2:3]
    o_ref[...] = x_ref[...] + gt1 * mixed


def _merge(y, g, bg, ys5, gates, x2, mod3, lng, lnb, gmean, wrw, wglu, wout, *, bsz, seq, rows):
    n = bsz * seq
    nt = seq // rows
    w = RWKV_WIDTH
    row_spec = lambda cols: pl.BlockSpec((rows, cols), lambda i: (i, 0))
    return pl.pallas_call(
        _merge_kernel,
        grid=(n // rows,),
        in_specs=[row_spec(w), row_spec(w), row_spec(w), row_spec(w), row_spec(2 * D_MODEL),
                  row_spec(D_MODEL),
                  pl.BlockSpec((None, 6, D_MODEL), lambda i: (i // nt, 0, 0)),
                  _const_spec((1, w)), _const_spec((1, w)), _const_spec((w, w)),
                  _const_spec((w, D_MODEL)), _const_spec((S5_WIDTH, 2 * D_MODEL)),
                  _const_spec((D_MODEL, D_MODEL))],
        out_specs=row_spec(D_MODEL),
        out_shape=jax.ShapeDtypeStruct((n, D_MODEL), F32),
        compiler_params=_params("arbitrary"),
        name="merge",
    )(y, g, bg, ys5, gates, x2, mod3, lng, lnb, gmean, wrw, wglu, wout)


def _ffn_kernel(x_ref, mod_ref, g2_ref, wup_ref, cw_ref, cb_ref, wdn_ref, gf_ref, o_ref, buf, *, rows):
    pad = SUBLANES

    @pl.when(pl.program_id(1) == 0)
    def _():
        buf[:, 0:pad, :] = jnp.zeros((FFN_SPLIT, pad, 2 * FFN_CH), F32)

    x = x_ref[...]
    mod = mod_ref[...]
    sh2, sc2, gt2 = mod[3:4], mod[4:5], mod[5:6]
    ms = jnp.mean(x * x, axis=-1, keepdims=True)
    h = x * lax.rsqrt(ms + RMS_EPS) * g2_ref[...]
    hb = (h * (1.0 + sc2) + sh2).astype(BF16)

    acc = jnp.zeros((rows, D_MODEL), F32)
    for c in range(FFN_SPLIT):
        buf[c, pad:pad + rows, :] = jnp.dot(hb, wup_ref[c], preferred_element_type=F32)
        cw = cw_ref[c]
        hid = cb_ref[c]
        for j in range(CONV_W):
            off = pad - (CONV_W - 1) + j
            hid = hid + cw[j:j + 1, :] * buf[c, off:off + rows, :]
        buf[c, pad - (CONV_W - 1):pad, :] = buf[c, pad + rows - (CONV_W - 1):pad + rows, :]
        gate, up = hid[:, :FFN_CH], hid[:, FFN_CH:]
        act = gate * jax.nn.sigmoid(gate) * up
        acc = acc + jnp.dot(act.astype(BF16), wdn_ref[c], preferred_element_type=F32)

    x2 = x + gt2 * acc
    ms2 = jnp.mean(x2 * x2, axis=-1, keepdims=True)
    o_ref[...] = x2 * lax.rsqrt(ms2 + RMS_EPS) * gf_ref[...]


def _ffn(x1, mod3, g2, wup, cw, cb, wdn, gf, *, bsz, seq, rows):
    n = bsz * seq
    nt = seq // rows
    row_spec = pl.BlockSpec((rows, D_MODEL), lambda b, t: (b * nt + t, 0))
    return pl.pallas_call(
        functools.partial(_ffn_kernel, rows=rows),
        grid=(bsz, nt),
        in_specs=[row_spec,
                  pl.BlockSpec((None, 6, D_MODEL), lambda b, t: (b, 0, 0)),
                  _const_spec((1, D_MODEL)),
                  _const_spec((FFN_SPLIT, D_MODEL, 2 * FFN_CH)),
                  _const_spec((FFN_SPLIT, CONV_W, 2 * FFN_CH)),
                  _const_spec((FFN_SPLIT, 1, 2 * FFN_CH)),
                  _const_spec((FFN_SPLIT, FFN_CH, D_MODEL)),
                  _const_spec((1, D_MODEL))],
        out_specs=row_spec,
        out_shape=jax.ShapeDtypeStruct((n, D_MODEL), F32),
        scratch_shapes=[pltpu.VMEM((FFN_SPLIT, rows + SUBLANES, 2 * FFN_CH), F32)],
        compiler_params=_params("arbitrary", "arbitrary"),
        name="ffn",
    )(x1, mod3, g2, wup, cw, cb, wdn, gf)


def _split_cols(a):
    parts = [jnp.concatenate([a[..., c * FFN_CH:(c + 1) * FFN_CH],
                              a[..., D_FF + c * FFN_CH:D_FF + (c + 1) * FFN_CH]], axis=-1)
             for c in range(FFN_SPLIT)]
    return jnp.stack(parts, axis=0)


def _block_diag_ones(n, blk, value, dtype):
    i = jnp.arange(n) // blk
    return jnp.where(i[:, None] == i[None, :], value, 0.0).astype(dtype)


def kernel(x, c, w_ada, b_ada, norm1_g, w_in, mu_shift, rwkv_w0, rwkv_w_up, rwkv_a0, rwkv_a_up,
           rwkv_g_up, rwkv_k_k, rwkv_k_a, rwkv_r_k, rwkv_ln_g, rwkv_ln_b, w_out_rwkv, s5_a_re,
           s5_a_im, s5_log_dt, s5_b_re, s5_b_im, s5_c_re, s5_c_im, s5_d, w_glu, w_out, norm2_g,
           w_ffn_up, ffn_conv_w, ffn_conv_b, w_ffn_down, norm_f_g):
    bsz, seq, _ = x.shape
    depth = w_ada.shape[0]
    n = bsz * seq
    w = RWKV_WIDTH
    rows_a = min(256, seq)
    rows_m = min(512, seq)
    rows_f = min(256, seq)
    s5_steps = min(64, seq)

    g64 = _block_diag_ones(w, RWKV_HEAD, 1.0, BF16)
    gmean = _block_diag_ones(w, RWKV_HEAD, 1.0 / RWKV_HEAD, BF16)
    eye8 = jnp.eye(LANES // S5_GROUP, dtype=F32)
    row = lambda a: a.reshape(1, -1)

    assert depth == 1, depth
    x2 = x.reshape(n, D_MODEL)
    for l in range(depth):
        mod3 = _adaln(c, w_ada[l], b_ada[l]).reshape(bsz, 6, D_MODEL)

        zero = jnp.zeros((LORA_W, w), F32)
        waup = jnp.concatenate([jnp.concatenate([rwkv_w_up[l], zero], axis=1),
                                jnp.concatenate([zero, rwkv_a_up[l]], axis=1)], axis=0).astype(BF16)
        r, k, v, ld, kn, b, g, bg, u, gates = _inproj(
            x2, mod3, row(norm1_g[l]), w_in[l].astype(BF16), row(mu_shift[l]), row(rwkv_w0[l]), waup,
            row(rwkv_a0[l]), rwkv_g_up[l].astype(BF16), row(rwkv_k_k[l]), row(rwkv_k_a[l]),
            row(rwkv_r_k[l]), g64, bsz=bsz, seq=seq, rows=rows_a)

        y = _wkv(r, k, v, ld, kn, b, bsz=bsz, seq=seq, rows=min(1024, seq))

        ab_re, ab_im, bb_re, bb_im = _s5prep(s5_a_re[l], s5_a_im[l], s5_log_dt[l], s5_b_re[l], s5_b_im[l])
        gl = LANES // S5_GROUP
        slab_in = lambda bb: jnp.einsum('jgcp,gh->jgchp', bb.reshape(S5_SLABS, gl, S5_GROUP, S5_STATE),
                                        eye8).reshape(S5_SLABS, LANES, S5_SLAB_STATE).astype(BF16)
        slab_out = lambda cc: jnp.einsum('jgcp,gh->jgphc', cc.reshape(S5_SLABS, gl, S5_GROUP, S5_STATE),
                                         eye8).reshape(S5_SLABS, S5_SLAB_STATE, LANES)
        cc = jnp.concatenate([slab_out(s5_c_re[l]), -slab_out(s5_c_im[l])], axis=1).astype(BF16)
        u_t = jnp.transpose(u.reshape(bsz, seq, S5_WIDTH), (1, 0, 2))
        ys5_t = _s5(u_t, slab_in(bb_re), slab_in(bb_im), cc, row(ab_re), row(ab_im), row(s5_d[l]),
                    bsz=bsz, seq=seq, steps=s5_steps)
        ys5 = jnp.transpose(ys5_t, (1, 0, 2)).reshape(n, S5_WIDTH)

        x1 = _merge(y, g, bg, ys5, gates, x2, mod3, row(rwkv_ln_g[l]), row(rwkv_ln_b[l]), gmean,
                    w_out_rwkv[l].astype(BF16), w_glu[l].astype(BF16), w_out[l].astype(BF16),
                    bsz=bsz, seq=seq, rows=rows_m)

        gf = row(norm_f_g)
        wdn = w_ffn_down[l].astype(BF16).reshape(FFN_SPLIT, FFN_CH, D_MODEL)
        x2 = _ffn(x1, mod3, row(norm2_g[l]), _split_cols(w_ffn_up[l]).astype(BF16),
                  _split_cols(ffn_conv_w[l]), _split_cols(ffn_conv_b[l].reshape(1, -1)), wdn,
                  gf, bsz=bsz, seq=seq, rows=rows_f)
    return x2.reshape(bsz, seq, D_MODEL)
```

```python
import functools
import math

import jax
import jax.numpy as jnp
from jax import lax
from jax.experimental import pallas as pl
from jax.experimental.pallas import tpu as pltpu

F32 = jnp.float32
BF16 = jnp.bfloat16
HIGHEST = lax.Precision.HIGHEST

D_MODEL = 1024
RWKV_WIDTH = 512
RWKV_HEAD = 64
LORA_W = 64
LORA_A = 64
LORA_G = 128
S5_WIDTH = 512
S5_GROUP = 16
S5_GROUPS = 32
S5_STATE = 64
D_FF = 2816
CONV_W = 3
RMS_EPS = 1e-6
GN_EPS = 64e-5
L2_EPS = 1e-12
SHIFT_COLS = 3 * RWKV_WIDTH + LORA_W + LORA_A + LORA_G
IN_COLS = SHIFT_COLS + S5_WIDTH + 2 * D_MODEL

LANES = 128
SUBLANES = 8
MXU_TILE = 256
WKV_CHUNK = 64
HEADS_PER_TILE = LANES // RWKV_HEAD
STACK = HEADS_PER_TILE * WKV_CHUNK
S5_SLABS = S5_WIDTH // LANES
S5_SLAB_STATE = (LANES // S5_GROUP) * S5_STATE
FFN_SPLIT = 2
FFN_CH = D_FF // FFN_SPLIT
VMEM_LIMIT = 56 * 1024 * 1024


def _dot(a, b):
    return jnp.dot(a.astype(BF16), b.astype(BF16), preferred_element_type=F32)


def _dot_split(a, g):
    hi = a.astype(BF16)
    lo = (a - hi.astype(F32)).astype(BF16)
    return (jnp.dot(hi, g, preferred_element_type=F32)
            + jnp.dot(lo, g, preferred_element_type=F32))


def _head_sum(a, g, split=False):
    f = _dot_split if split else _dot
    return jnp.concatenate([f(a[:, i:i + MXU_TILE], g) for i in range(0, a.shape[1], MXU_TILE)], axis=1)


def _aligned(i, m):
    return i if isinstance(i, int) else pl.multiple_of(i, m)


def _const_spec(shape):
    nd = len(shape)
    return pl.BlockSpec(shape, lambda *_: (0,) * nd, pipeline_mode=pl.Buffered(1))


def _params(*sem):
    return pltpu.CompilerParams(dimension_semantics=sem, vmem_limit_bytes=VMEM_LIMIT)


def _adaln_kernel(c_ref, w_ref, b_ref, o_ref):
    c = c_ref[...]
    s = c * jax.nn.sigmoid(c)
    o_ref[...] = jnp.dot(s, w_ref[...], preferred_element_type=F32, precision=HIGHEST) + b_ref[...]


def _adaln(c, w_ada, b_ada):
    bsz = c.shape[0]
    n = w_ada.shape[1]
    bn = D_MODEL
    return pl.pallas_call(
        _adaln_kernel,
        grid=(n // bn,),
        in_specs=[pl.BlockSpec((bsz, D_MODEL), lambda j: (0, 0)),
                  pl.BlockSpec((D_MODEL, bn), lambda j: (0, j)),
                  pl.BlockSpec((1, bn), lambda j: (0, j))],
        out_specs=pl.BlockSpec((bsz, bn), lambda j: (0, j)),
        out_shape=jax.ShapeDtypeStruct((bsz, n), F32),
        compiler_params=_params("arbitrary"),
        name="adaln",
    )(c, w_ada, b_ada.reshape(1, n))


def _s5prep_kernel(are_ref, aim_ref, ldt_ref, bre_ref, bim_ref, abre_o, abim_o, bbre_o, bbim_o):
    a_re = are_ref[...]
    a_im = aim_ref[...]
    dt = jnp.exp(ldt_ref[...])
    z_re, z_im = a_re * dt, a_im * dt
    mag = jnp.exp(z_re)
    ab_re, ab_im = mag * jnp.cos(z_im), mag * jnp.sin(z_im)
    den = a_re * a_re + a_im * a_im
    q_re = ((ab_re - 1.0) * a_re + ab_im * a_im) / den
    q_im = (ab_im * a_re - (ab_re - 1.0) * a_im) / den
    abre_o[...] = ab_re
    abim_o[...] = ab_im
    b_re = bre_ref[...]
    b_im = bim_ref[...]
    bbre_o[...] = q_re[:, None, :] * b_re - q_im[:, None, :] * b_im
    bbim_o[...] = q_re[:, None, :] * b_im + q_im[:, None, :] * b_re


def _s5prep(a_re, a_im, log_dt, b_re, b_im):
    g, p = a_re.shape
    c = b_re.shape[-1]
    b_re_t = jnp.transpose(b_re, (0, 2, 1))
    b_im_t = jnp.transpose(b_im, (0, 2, 1))
    return pl.pallas_call(
        _s5prep_kernel,
        out_shape=(jax.ShapeDtypeStruct((g, p), F32), jax.ShapeDtypeStruct((g, p), F32),
                   jax.ShapeDtypeStruct((g, c, p), F32), jax.ShapeDtypeStruct((g, c, p), F32)),
        name="s5prep",
    )(a_re, a_im, log_dt.reshape(g, 1), b_re_t, b_im_t)


def _inproj_kernel(x_ref, mod_ref, g1_ref, win_ref, mu_ref, w0_ref, waup_ref, a0_ref, gup_ref,
                   kk_ref, ka_ref, rk_ref, g64_ref,
                   r_o, k_o, v_o, ld_o, kn_o, b_o, g_o, bg_o, u_o, gates_o, carry, *, rows):
    @pl.when(pl.program_id(1) == 0)
    def _():
        carry[...] = jnp.zeros_like(carry)

    x = x_ref[...]
    mod = mod_ref[...]
    sh1, sc1 = mod[0:1], mod[1:2]
    ms = jnp.mean(x * x, axis=-1, keepdims=True)
    h = x * lax.rsqrt(ms + RMS_EPS) * g1_ref[...]
    h = h * (1.0 + sc1) + sh1
    proj = jnp.dot(h.astype(BF16), win_ref[...], preferred_element_type=F32)

    p = proj[:, :SHIFT_COLS]
    row = lax.broadcasted_iota(jnp.int32, p.shape, 0)
    prev = jnp.where(row == 0, carry[...], pltpu.roll(p, 1, axis=0))
    carry[...] = p[rows - 1:rows, :]
    ps = p + (prev - p) * mu_ref[...]

    w = RWKV_WIDTH
    r, k, v = ps[:, 0:w], ps[:, w:2 * w], ps[:, 2 * w:3 * w]
    wa = ps[:, 3 * w:3 * w + LORA_W + LORA_A]
    lane = lax.broadcasted_iota(jnp.int32, wa.shape, 1)
    wa = jnp.where(lane < LORA_W, jnp.tanh(wa), wa)
    delta = _dot(wa, waup_ref[...])
    w_raw = w0_ref[...] + delta[:, :w]
    ld = -math.exp(-0.5) * jax.nn.sigmoid(w_raw)
    eta = jax.nn.sigmoid(a0_ref[...] + delta[:, w:])
    gd = ps[:, 3 * w + LORA_W + LORA_A:SHIFT_COLS]
    g = _dot(jax.nn.sigmoid(gd), gup_ref[...])

    g64 = g64_ref[...]
    kk = k * kk_ref[...]
    kn = kk * lax.rsqrt(_head_sum(kk * kk, g64) + L2_EPS)
    k2 = k * (1.0 + (eta - 1.0) * ka_ref[...])
    bonus = _head_sum(r * k2 * rk_ref[...], g64) * v

    r_o[...] = r
    k_o[...] = k2
    v_o[...] = v
    ld_o[...] = ld
    kn_o[...] = kn
    b_o[...] = kn * eta
    g_o[...] = g
    bg_o[...] = bonus * g
    u_o[...] = proj[:, SHIFT_COLS:SHIFT_COLS + S5_WIDTH]
    gates_o[...] = jax.nn.sigmoid(proj[:, SHIFT_COLS + S5_WIDTH:])


def _inproj(x2, mod3, g1, win, mu, w0, waup, a0, gup, k_k, k_a, r_k, g64, *, bsz, seq, rows):
    n = bsz * seq
    nt = seq // rows
    w = RWKV_WIDTH
    row_spec = lambda cols: pl.BlockSpec((rows, cols), lambda b, t: (b * nt + t, 0))
    outs = [jax.ShapeDtypeStruct((n, w), F32)] * 9 + [jax.ShapeDtypeStruct((n, 2 * D_MODEL), F32)]
    return pl.pallas_call(
        functools.partial(_inproj_kernel, rows=rows),
        grid=(bsz, nt),
        in_specs=[row_spec(D_MODEL),
                  pl.BlockSpec((None, 6, D_MODEL), lambda b, t: (b, 0, 0)),
                  _const_spec((1, D_MODEL)),
                  _const_spec((D_MODEL, IN_COLS)),
                  _const_spec((1, SHIFT_COLS)),
                  _const_spec((1, w)),
                  _const_spec((LORA_W + LORA_A, 2 * w)),
                  _const_spec((1, w)),
                  _const_spec((LORA_G, w)),
                  _const_spec((1, w)), _const_spec((1, w)), _const_spec((1, w)),
                  _const_spec((MXU_TILE, MXU_TILE))],
        out_specs=[row_spec(w)] * 9 + [row_spec(2 * D_MODEL)],
        out_shape=outs,
        scratch_shapes=[pltpu.VMEM((1, SHIFT_COLS), F32)],
        compiler_params=_params("arbitrary", "arbitrary"),
        name="inproj",
    )(x2, mod3, g1, win, mu, w0, waup, a0, gup, k_k, k_a, r_k, g64)


def _stack_heads(z):
    lane = lax.broadcasted_iota(jnp.int32, z.shape, 1)
    first = lane < RWKV_HEAD
    return jnp.concatenate([jnp.where(first, z, 0.0), jnp.where(first, 0.0, z)], axis=0)


def _cumsum_rows(x):
    n = x.shape[0]
    row = lax.broadcasted_iota(jnp.int32, x.shape, 0)
    d = 1
    while d < n:
        if d < SUBLANES:
            shifted = jnp.where(row >= d, pltpu.roll(x, d, axis=0), 0.0)
        else:
            shifted = jnp.concatenate([jnp.zeros((d, x.shape[1]), x.dtype), x[:n - d]], axis=0)
        x = x + shifted
        d *= 2
    return x


_WKV_STAGE_BOUNDARIES = 8


def _wkv_terms(chains, side=()):
    t = WKV_CHUNK
    n = STACK
    jobs = list(side)
    calls = [0]

    def between():
        k = calls[0]
        calls[0] += 1
        for _ in range((k + 1) * len(side) // _WKV_STAGE_BOUNDARIES - k * len(side) // _WKV_STAGE_BOUNDARIES):
            jobs.pop(0)()

    ri = lax.broadcasted_iota(jnp.int32, (n, n), 0)
    ci = lax.broadcasted_iota(jnp.int32, (n, n), 1)
    same_head = (ri // t) == (ci // t)
    strict = same_head & (ri > ci)
    incl = same_head & (ri >= ci)
    eye = jnp.where(ri == ci, 1.0, 0.0)

    a_s, r_s, v_s, lhs, rhs, bk, w_end = [], [], [], [], [], [], []
    for r, k, v, ld, kn, b, cl in chains:
        cl_end = cl[t - 1:t, :]
        e_incl = jnp.exp(cl)
        e_excl = jnp.exp(cl - ld)
        e_neg = jnp.exp(-cl)
        e_end = jnp.exp(cl_end - cl)
        a_s.append(_stack_heads(-kn * e_excl))
        r_s.append(_stack_heads(r * e_incl))
        v_s.append(_stack_heads(v))
        lhs.append(jnp.concatenate([a_s[-1], r_s[-1]], axis=0).astype(BF16))
        rhs.append(jnp.concatenate([_stack_heads(b * e_neg), _stack_heads(k * e_neg)], axis=0).astype(BF16))
        bk.append(jnp.concatenate([_stack_heads(b * e_end), _stack_heads(k * e_end)], axis=0).astype(BF16))
        w_end.append(jnp.exp(cl_end))

    amat = [lax.dot_general(l, rr, (((1,), (1,)), ((), ())), preferred_element_type=F32)
            for l, rr in zip(lhs, rhs)]
    a_ab = [jnp.where(strict, a[:n, :n], 0.0) for a in amat]
    a_ak = [jnp.where(strict, a[:n, n:], 0.0).astype(BF16) for a in amat]
    a_r = [jnp.concatenate([jnp.where(incl, a[n:, :n], 0.0), jnp.where(incl, a[n:, n:], 0.0)],
                           axis=1).astype(BF16) for a in amat]

    x = [eye + jnp.where(ri // 2 == ci // 2, a, 0.0) for a in a_ab]
    between()
    s = 2
    while s < t:
        level = ((ri // (2 * s)) == (ci // (2 * s))) & ((ri // s) % 2 == 1) & ((ci // s) % 2 == 0)
        c = [jnp.where(level, a, 0.0).astype(BF16) for a in a_ab]
        xb = [xi.astype(BF16) for xi in x]
        if s < SUBLANES:
            xc = [jnp.dot(xi, cc, preferred_element_type=F32).astype(BF16) for xi, cc in zip(xb, c)]
            x = [xi + jnp.dot(xci, xbi, preferred_element_type=F32) for xi, xci, xbi in zip(x, xc, xb)]
        else:
            odd = lambda m: jnp.concatenate([m[q * s:(q + 1) * s] for q in range(1, n // s, 2)], axis=0)
            xo = [odd(xi) for xi in x]
            xc = [jnp.dot(xoi.astype(BF16), cc, preferred_element_type=F32).astype(BF16)
                  for xoi, cc in zip(xo, c)]
            xo = [xoi + jnp.dot(xci, xbi, preferred_element_type=F32) for xoi, xci, xbi in zip(xo, xc, xb)]
            x = [jnp.concatenate([xoi[(q // 2) * s:(q // 2 + 1) * s] if q % 2 else xi[q * s:(q + 1) * s]
                                  for q in range(n // s)], axis=0) for xi, xoi in zip(x, xo)]
        between()
        s *= 2

    akv = [jnp.dot(a, v.astype(BF16), preferred_element_type=F32) for a, v in zip(a_ak, v_s)]
    tg = [_dot(xi, jnp.concatenate([a, kv], axis=1)) for xi, a, kv in zip(x, a_s, akv)]
    between()
    low = [jnp.concatenate([g, jnp.concatenate([jnp.zeros_like(v), v], axis=1)], axis=0).astype(BF16)
           for g, v in zip(tg, v_s)]
    out1 = [jnp.dot(a, lo, preferred_element_type=F32) for a, lo in zip(a_r, low)]
    out2 = [lax.dot_general(bb, lo, (((0,), (0,)), ((), ())), preferred_element_type=F32)
            for bb, lo in zip(bk, low)]
    between()
    ki = lax.broadcasted_iota(jnp.int32, (LANES, LANES), 0)
    kj = lax.broadcasted_iota(jnp.int32, (LANES, LANES), 1)
    terms = []
    for rs, o1, o2, we in zip(r_s, out1, out2, w_end):
        r_hat = rs + o1[:, :LANES]
        m = o2[:, :LANES] + jnp.where(ki == kj, we, 0.0)
        terms.append((r_hat, o1[:, LANES:], m, o2[:, LANES:]))
    return terms


def _wkv_kernel(r_ref, k_ref, v_ref, ld_ref, kn_ref, b_ref, y_ref, s_ref, mr_ref, nc_ref, yv_ref,
                *, rows, group):
    t = WKV_CHUNK
    npair = RWKV_WIDTH // LANES

    @pl.when(pl.program_id(1) == 0)
    def _():
        s_ref[...] = jnp.zeros_like(s_ref)

    def chain_step(c):
        rs = pl.ds(_aligned(c * STACK, STACK), STACK)
        outs = [jnp.dot(mr_ref[p, pl.ds(_aligned(c * 2 * STACK, 2 * STACK), 2 * STACK), :],
                        s_ref[p].astype(BF16), preferred_element_type=F32) for p in range(npair)]
        for p, out in enumerate(outs):
            s_ref[p] = out[:LANES] + nc_ref[p, rs, :]
            ys = out[LANES:] + yv_ref[p, rs, :]
            y_ref[pl.ds(_aligned(c * t, t), t), p * LANES:(p + 1) * LANES] = ys[:t] + ys[t:]

    def precompute(g, side):
        chains, where = [], []
        for j in range(group):
            c = g * group + j
            sl = pl.ds(_aligned(c * t, t), t)
            tiles = [ref[sl, :] for ref in (r_ref, k_ref, v_ref, ld_ref, kn_ref, b_ref)]
            cl = _cumsum_rows(tiles[3])
            for p in range(npair):
                ls = slice(p * LANES, (p + 1) * LANES)
                chains.append(tuple(a[:, ls] for a in tiles) + (cl[:, ls],))
                where.append((c, p))
        for (c, p), (r_hat, y_v, m, n_c) in zip(where, _wkv_terms(chains, side)):
            mr_ref[p, pl.ds(_aligned(c * 2 * STACK, 2 * STACK), 2 * STACK), :] = (
                jnp.concatenate([m, r_hat], axis=0).astype(BF16))
            rs = pl.ds(_aligned(c * STACK, STACK), STACK)
            nc_ref[p, rs, :] = n_c
            yv_ref[p, rs, :] = y_v

    ngroup = rows // (t * group)
    precompute(0, ())

    def body(g, _):
        precompute(g, [functools.partial(chain_step, (g - 1) * group + j) for j in range(group)])
        return 0

    lax.fori_loop(1, ngroup, body, 0)
    for j in range(group):
        chain_step((ngroup - 1) * group + j)


def _wkv(r, k, v, ld, kn, b, *, bsz, seq, rows):
    n = bsz * seq
    nt = seq // rows
    npair = RWKV_WIDTH // LANES
    nchunk = rows // WKV_CHUNK
    spec = pl.BlockSpec((rows, RWKV_WIDTH), lambda bi, ti: (bi * nt + ti, 0))
    return pl.pallas_call(
        functools.partial(_wkv_kernel, rows=rows, group=4),
        grid=(bsz, nt),
        in_specs=[spec] * 6,
        out_specs=spec,
        out_shape=jax.ShapeDtypeStruct((n, RWKV_WIDTH), F32),
        scratch_shapes=[pltpu.VMEM((npair, LANES, LANES), F32),
                        pltpu.VMEM((npair, nchunk * 2 * STACK, LANES), BF16),
                        pltpu.VMEM((npair, nchunk * STACK, LANES), F32),
                        pltpu.VMEM((npair, nchunk * STACK, LANES), F32)],
        compiler_params=_params("arbitrary", "arbitrary"),
        name="wkv",
    )(r, k, v, ld, kn, b)


def _gelu_tanh(x):
    return 0.5 * x * (1.0 + jnp.tanh(math.sqrt(2.0 / math.pi) * (x + 0.044715 * x * x * x)))


def _s5_kernel(u_ref, wre_ref, wim_ref, cc_ref, ar_ref, ai_ref, d_ref, y_ref,
               xre, xim, st_re, st_im, *, steps):
    @pl.when(pl.program_id(1) == 0)
    def _():
        st_re[...] = jnp.zeros_like(st_re)
        st_im[...] = jnp.zeros_like(st_im)

    rows = steps * SUBLANES
    u = u_ref[...].reshape(rows, S5_WIDTH)
    ub = u.astype(BF16)
    ns = S5_SLAB_STATE
    for j in range(S5_SLABS):
        uj = ub[:, j * LANES:(j + 1) * LANES]
        xre[:, j * ns:(j + 1) * ns] = jnp.dot(uj, wre_ref[j], preferred_element_type=F32)
        xim[:, j * ns:(j + 1) * ns] = jnp.dot(uj, wim_ref[j], preferred_element_type=F32)

    half = S5_GROUPS * S5_STATE // 2
    for c in range(2):
        cs = slice(c * half, (c + 1) * half)
        ar = jnp.broadcast_to(ar_ref[:, cs], (SUBLANES, half))
        ai = jnp.broadcast_to(ai_ref[:, cs], (SUBLANES, half))

        def step(i, state, cs=cs, ar=ar, ai=ai):
            x_re, x_im = state
            rs = pl.ds(pl.multiple_of(i * SUBLANES, SUBLANES), SUBLANES)
            n_re = ar * x_re - ai * x_im + xre[rs, cs]
            n_im = ar * x_im + ai * x_re + xim[rs, cs]
            xre[rs, cs] = n_re
            xim[rs, cs] = n_im
            return n_re, n_im

        x_re, x_im = lax.fori_loop(0, steps, step, (st_re[:, cs], st_im[:, cs]), unroll=2)
        st_re[:, cs] = x_re
        st_im[:, cs] = x_im

    for j in range(S5_SLABS):
        xs = jnp.concatenate([xre[:, j * ns:(j + 1) * ns], xim[:, j * ns:(j + 1) * ns]], axis=1)
        ls = slice(j * LANES, (j + 1) * LANES)
        yj = jnp.dot(xs.astype(BF16), cc_ref[j], preferred_element_type=F32) + d_ref[:, ls] * u[:, ls]
        y_ref[:, :, ls] = _gelu_tanh(yj).reshape(steps, SUBLANES, LANES)


def _s5(u_t, wre, wim, cc, ar, ai, d, *, bsz, seq, steps):
    nstate = S5_GROUPS * S5_STATE
    rows = steps * SUBLANES
    spec = pl.BlockSpec((steps, SUBLANES, S5_WIDTH), lambda g, t: (t, g, 0))
    return pl.pallas_call(
        functools.partial(_s5_kernel, steps=steps),
        grid=(bsz // SUBLANES, seq // steps),
        in_specs=[spec,
                  _const_spec((S5_SLABS, LANES, S5_SLAB_STATE)),
                  _const_spec((S5_SLABS, LANES, S5_SLAB_STATE)),
                  _const_spec((S5_SLABS, 2 * S5_SLAB_STATE, LANES)),
                  _const_spec((1, nstate)), _const_spec((1, nstate)),
                  _const_spec((1, S5_WIDTH))],
        out_specs=spec,
        out_shape=jax.ShapeDtypeStruct((seq, bsz, S5_WIDTH), F32),
        scratch_shapes=[pltpu.VMEM((rows, nstate), F32), pltpu.VMEM((rows, nstate), F32),
                        pltpu.VMEM((SUBLANES, nstate), F32), pltpu.VMEM((SUBLANES, nstate), F32)],
        compiler_params=_params("arbitrary", "arbitrary"),
        name="s5",
    )(u_t, wre, wim, cc, ar, ai, d)


def _merge_kernel(y_ref, g_ref, bg_ref, ys5_ref, gates_ref, x_ref, mod_ref, lng_ref, lnb_ref,
                  gmean_ref, wrw_ref, wglu_ref, wout_ref, o_ref):
    gm = gmean_ref[...]
    y = y_ref[...]
    mean = _head_sum(y, gm, split=True)
    d = y - mean
    var = _head_sum(d * d, gm)
    yn = d * lax.rsqrt(var + GN_EPS) * lng_ref[...] + lnb_ref[...]
    ya = _dot(yn * g_ref[...] + bg_ref[...], wrw_ref[...])
    z = _dot(ys5_ref[...], wglu_ref[...])
    yb = z[:, :D_MODEL] * jax.nn.sigmoid(z[:, D_MODEL:])
    gates = gates_ref[...]
    mixed = _dot(gates[:, :D_MODEL] * ya + gates[:, D_MODEL:] * yb, wout_ref[...])
    gt1 = mod_ref[...][2:3]
    o_ref[...] = x_ref[...] + gt1 * mixed


def _merge(y, g, bg, ys5, gates, x2, mod3, lng, lnb, gmean, wrw, wglu, wout, *, bsz, seq, rows):
    n = bsz * seq
    nt = seq // rows
    w = RWKV_WIDTH
    row_spec = lambda cols: pl.BlockSpec((rows, cols), lambda i: (i, 0))
    return pl.pallas_call(
        _merge_kernel,
        grid=(n // rows,),
        in_specs=[row_spec(w), row_spec(w), row_spec(w), row_spec(w), row_spec(2 * D_MODEL),
                  row_spec(D_MODEL),
                  pl.BlockSpec((None, 6, D_MODEL), lambda i: (i // nt, 0, 0)),
                  _const_spec((1, w)), _const_spec((1, w)), _const_spec((MXU_TILE, MXU_TILE)),
                  _const_spec((w, D_MODEL)), _const_spec((S5_WIDTH, 2 * D_MODEL)),
                  _const_spec((D_MODEL, D_MODEL))],
        out_specs=row_spec(D_MODEL),
        out_shape=jax.ShapeDtypeStruct((n, D_MODEL), F32),
        compiler_params=_params("arbitrary"),
        name="merge",
    )(y, g, bg, ys5, gates, x2, mod3, lng, lnb, gmean, wrw, wglu, wout)


def _ffn_kernel(x_ref, mod_ref, g2_ref, wup_ref, cw_ref, cb_ref, wdn_ref, gf_ref, o_ref, buf, *, rows):
    pad = SUBLANES

    @pl.when(pl.program_id(1) == 0)
    def _():
        buf[:, 0:pad, :] = jnp.zeros((FFN_SPLIT, pad, 2 * FFN_CH), F32)

    mod = mod_ref[...]
    sh2, sc2, gt2 = mod[3:4], mod[4:5], mod[5:6]

    def normed(xh):
        ms = jnp.mean(xh * xh, axis=-1, keepdims=True)
        h = xh * lax.rsqrt(ms + RMS_EPS) * g2_ref[...]
        return (h * (1.0 + sc2) + sh2).astype(BF16)

    halves = 2
    hr = rows // halves
    x = [x_ref[h * hr:(h + 1) * hr, :] for h in range(halves)]
    hb = [normed(x[0]), None]
    acc = [jnp.zeros((hr, D_MODEL), F32) for _ in range(halves)]
    for c in range(FFN_SPLIT):
        for h in range(halves):
            if hb[h] is None:
                hb[h] = normed(x[h])
            r0 = pad + h * hr
            buf[c, r0:r0 + hr, :] = jnp.dot(hb[h], wup_ref[c], preferred_element_type=F32)
        cw = cw_ref[c]
        for h in range(halves):
            hid = cb_ref[c]
            for j in range(CONV_W):
                off = pad + h * hr - (CONV_W - 1) + j
                hid = hid + cw[j:j + 1, :] * buf[c, off:off + hr, :]
            gate, up = hid[:, :FFN_CH], hid[:, FFN_CH:]
            act = (0.5 * gate) * (1.0 + jnp.tanh(0.5 * gate)) * up
            acc[h] = acc[h] + jnp.dot(act.astype(BF16), wdn_ref[c], preferred_element_type=F32)
        buf[c, pad - (CONV_W - 1):pad, :] = buf[c, pad + rows - (CONV_W - 1):pad + rows, :]

    for h in range(halves):
        x2 = x[h] + gt2 * acc[h]
        ms2 = jnp.mean(x2 * x2, axis=-1, keepdims=True)
        o_ref[h * hr:(h + 1) * hr, :] = x2 * lax.rsqrt(ms2 + RMS_EPS) * gf_ref[...]


def _ffn(x1, mod3, g2, wup, cw, cb, wdn, gf, *, bsz, seq, rows):
    n = bsz * seq
    nt = seq // rows
    row_spec = pl.BlockSpec((rows, D_MODEL), lambda b, t: (b * nt + t, 0))
    return pl.pallas_call(
        functools.partial(_ffn_kernel, rows=rows),
        grid=(bsz, nt),
        in_specs=[row_spec,
                  pl.BlockSpec((None, 6, D_MODEL), lambda b, t: (b, 0, 0)),
                  _const_spec((1, D_MODEL)),
                  _const_spec((FFN_SPLIT, D_MODEL, 2 * FFN_CH)),
                  _const_spec((FFN_SPLIT, CONV_W, 2 * FFN_CH)),
                  _const_spec((FFN_SPLIT, 1, 2 * FFN_CH)),
                  _const_spec((FFN_SPLIT, FFN_CH, D_MODEL)),
                  _const_spec((1, D_MODEL))],
        out_specs=row_spec,
        out_shape=jax.ShapeDtypeStruct((n, D_MODEL), F32),
        scratch_shapes=[pltpu.VMEM((FFN_SPLIT, rows + SUBLANES, 2 * FFN_CH), F32)],
        compiler_params=_params("arbitrary", "arbitrary"),
        name="ffn",
    )(x1, mod3, g2, wup, cw, cb, wdn, gf)


def _split_cols(a):
    parts = [jnp.concatenate([a[..., c * FFN_CH:(c + 1) * FFN_CH],
                              a[..., D_FF + c * FFN_CH:D_FF + (c + 1) * FFN_CH]], axis=-1)
             for c in range(FFN_SPLIT)]
    return jnp.stack(parts, axis=0)


def _block_diag_ones(n, blk, value, dtype):
    i = jnp.arange(n) // blk
    return jnp.where(i[:, None] == i[None, :], value, 0.0).astype(dtype)


def kernel(x, c, w_ada, b_ada, norm1_g, w_in, mu_shift, rwkv_w0, rwkv_w_up, rwkv_a0, rwkv_a_up,
           rwkv_g_up, rwkv_k_k, rwkv_k_a, rwkv_r_k, rwkv_ln_g, rwkv_ln_b, w_out_rwkv, s5_a_re,
           s5_a_im, s5_log_dt, s5_b_re, s5_b_im, s5_c_re, s5_c_im, s5_d, w_glu, w_out, norm2_g,
           w_ffn_up, ffn_conv_w, ffn_conv_b, w_ffn_down, norm_f_g):
    bsz, seq, _ = x.shape
    depth = w_ada.shape[0]
    n = bsz * seq
    w = RWKV_WIDTH
    rows_a = min(256, seq)
    rows_m = min(512, seq)
    rows_f = min(512, seq)
    s5_steps = min(64, seq)

    g64 = _block_diag_ones(MXU_TILE, RWKV_HEAD, 1.0, BF16)
    gmean = _block_diag_ones(MXU_TILE, RWKV_HEAD, 1.0 / RWKV_HEAD, BF16)
    eye8 = jnp.eye(LANES // S5_GROUP, dtype=F32)
    row = lambda a: a.reshape(1, -1)

    assert depth == 1, depth
    x2 = x.reshape(n, D_MODEL)
    for l in range(depth):
        mod3 = _adaln(c, w_ada[l], b_ada[l]).reshape(bsz, 6, D_MODEL)

        zero = jnp.zeros((LORA_W, w), F32)
        waup = jnp.concatenate([jnp.concatenate([rwkv_w_up[l], zero], axis=1),
                                jnp.concatenate([zero, rwkv_a_up[l]], axis=1)], axis=0).astype(BF16)
        r, k, v, ld, kn, b, g, bg, u, gates = _inproj(
            x2, mod3, row(norm1_g[l]), w_in[l].astype(BF16), row(mu_shift[l]), row(rwkv_w0[l]), waup,
            row(rwkv_a0[l]), rwkv_g_up[l].astype(BF16), row(rwkv_k_k[l]), row(rwkv_k_a[l]),
            row(rwkv_r_k[l]), g64, bsz=bsz, seq=seq, rows=rows_a)

        y = _wkv(r, k, v, ld, kn, b, bsz=bsz, seq=seq, rows=min(1024, seq))

        ab_re, ab_im, bb_re, bb_im = _s5prep(s5_a_re[l], s5_a_im[l], s5_log_dt[l], s5_b_re[l], s5_b_im[l])
        gl = LANES // S5_GROUP
        slab_in = lambda bb: jnp.einsum('jgcp,gh->jgchp', bb.reshape(S5_SLABS, gl, S5_GROUP, S5_STATE),
                                        eye8).reshape(S5_SLABS, LANES, S5_SLAB_STATE).astype(BF16)
        slab_out = lambda cc: jnp.einsum('jgcp,gh->jgphc', cc.reshape(S5_SLABS, gl, S5_GROUP, S5_STATE),
                                         eye8).reshape(S5_SLABS, S5_SLAB_STATE, LANES)
        cc = jnp.concatenate([slab_out(s5_c_re[l]), -slab_out(s5_c_im[l])], axis=1).astype(BF16)
        u_t = jnp.transpose(u.reshape(bsz, seq, S5_WIDTH), (1, 0, 2))
        ys5_t = _s5(u_t, slab_in(bb_re), slab_in(bb_im), cc, row(ab_re), row(ab_im), row(s5_d[l]),
                    bsz=bsz, seq=seq, steps=s5_steps)
        ys5 = jnp.transpose(ys5_t, (1, 0, 2)).reshape(n, S5_WIDTH)

        x1 = _merge(y, g, bg, ys5, gates, x2, mod3, row(rwkv_ln_g[l]), row(rwkv_ln_b[l]), gmean,
                    w_out_rwkv[l].astype(BF16), w_glu[l].astype(BF16), w_out[l].astype(BF16),
                    bsz=bsz, seq=seq, rows=rows_m)

        gf = row(norm_f_g)
        wdn = w_ffn_down[l].astype(BF16).reshape(FFN_SPLIT, FFN_CH, D_MODEL)
        x2 = _ffn(x1, mod3, row(norm2_g[l]), _split_cols(w_ffn_up[l]).astype(BF16),
                  _split_cols(ffn_conv_w[l]), _split_cols(ffn_conv_b[l].reshape(1, -1)), wdn,
                  gf, bsz=bsz, seq=seq, rows=rows_f)
    return x2.reshape(bsz, seq, D_MODEL)
```

```python
import functools
import math

import jax
import jax.numpy as jnp
from jax import lax
from jax.experimental import pallas as pl
from jax.experimental.pallas import tpu as pltpu

F32 = jnp.float32
BF16 = jnp.bfloat16
HIGHEST = lax.Precision.HIGHEST

D_MODEL = 1024
RWKV_WIDTH = 512
RWKV_HEAD = 64
LORA_W = 64
LORA_A = 64
LORA_G = 128
S5_WIDTH = 512
S5_GROUP = 16
S5_GROUPS = 32
S5_STATE = 64
D_FF = 2816
CONV_W = 3
RMS_EPS = 1e-6
GN_EPS = 64e-5
L2_EPS = 1e-12
SHIFT_COLS = 3 * RWKV_WIDTH + LORA_W + LORA_A + LORA_G
IN_COLS = SHIFT_COLS + S5_WIDTH + 2 * D_MODEL

LANES = 128
SUBLANES = 8
MXU_TILE = 256
WKV_CHUNK = 64
HEADS_PER_TILE = LANES // RWKV_HEAD
STACK = HEADS_PER_TILE * WKV_CHUNK
S5_SLABS = S5_WIDTH // LANES
S5_SLAB_STATE = (LANES // S5_GROUP) * S5_STATE
FFN_SPLIT = 2
FFN_CH = D_FF // FFN_SPLIT
VMEM_LIMIT = 56 * 1024 * 1024


def _dot(a, b):
    return jnp.dot(a.astype(BF16), b.astype(BF16), preferred_element_type=F32)


def _dot_split(a, g):
    hi = a.astype(BF16)
    lo = (a - hi.astype(F32)).astype(BF16)
    return (jnp.dot(hi, g, preferred_element_type=F32)
            + jnp.dot(lo, g, preferred_element_type=F32))


def _head_sum(a, g, split=False):
    f = _dot_split if split else _dot
    return jnp.concatenate([f(a[:, i:i + MXU_TILE], g) for i in range(0, a.shape[1], MXU_TILE)], axis=1)


def _aligned(i, m):
    return i if isinstance(i, int) else pl.multiple_of(i, m)


def _const_spec(shape):
    nd = len(shape)
    return pl.BlockSpec(shape, lambda *_: (0,) * nd, pipeline_mode=pl.Buffered(1))


def _params(*sem):
    return pltpu.CompilerParams(dimension_semantics=sem, vmem_limit_bytes=VMEM_LIMIT)


def _adaln_kernel(c_ref, w_ref, b_ref, o_ref):
    c = c_ref[...]
    s = c * jax.nn.sigmoid(c)
    o_ref[...] = jnp.dot(s, w_ref[...], preferred_element_type=F32, precision=HIGHEST) + b_ref[...]


def _adaln(c, w_ada, b_ada):
    bsz = c.shape[0]
    n = w_ada.shape[1]
    bn = D_MODEL
    return pl.pallas_call(
        _adaln_kernel,
        grid=(n // bn,),
        in_specs=[pl.BlockSpec((bsz, D_MODEL), lambda j: (0, 0)),
                  pl.BlockSpec((D_MODEL, bn), lambda j: (0, j)),
                  pl.BlockSpec((1, bn), lambda j: (0, j))],
        out_specs=pl.BlockSpec((bsz, bn), lambda j: (0, j)),
        out_shape=jax.ShapeDtypeStruct((bsz, n), F32),
        compiler_params=_params("arbitrary"),
        name="adaln",
    )(c, w_ada, b_ada.reshape(1, n))


def _s5prep_kernel(are_ref, aim_ref, ldt_ref, bre_ref, bim_ref, abre_o, abim_o, bbre_o, bbim_o):
    a_re = are_ref[...]
    a_im = aim_ref[...]
    dt = jnp.exp(ldt_ref[...])
    z_re, z_im = a_re * dt, a_im * dt
    mag = jnp.exp(z_re)
    ab_re, ab_im = mag * jnp.cos(z_im), mag * jnp.sin(z_im)
    den = a_re * a_re + a_im * a_im
    q_re = ((ab_re - 1.0) * a_re + ab_im * a_im) / den
    q_im = (ab_im * a_re - (ab_re - 1.0) * a_im) / den
    abre_o[...] = ab_re
    abim_o[...] = ab_im
    b_re = bre_ref[...]
    b_im = bim_ref[...]
    bbre_o[...] = q_re[:, None, :] * b_re - q_im[:, None, :] * b_im
    bbim_o[...] = q_re[:, None, :] * b_im + q_im[:, None, :] * b_re


def _s5prep(a_re, a_im, log_dt, b_re, b_im):
    g, p = a_re.shape
    c = b_re.shape[-1]
    b_re_t = jnp.transpose(b_re, (0, 2, 1))
    b_im_t = jnp.transpose(b_im, (0, 2, 1))
    return pl.pallas_call(
        _s5prep_kernel,
        out_shape=(jax.ShapeDtypeStruct((g, p), F32), jax.ShapeDtypeStruct((g, p), F32),
                   jax.ShapeDtypeStruct((g, c, p), F32), jax.ShapeDtypeStruct((g, c, p), F32)),
        name="s5prep",
    )(a_re, a_im, log_dt.reshape(g, 1), b_re_t, b_im_t)


def _inproj_kernel(x_ref, mod_ref, g1_ref, win_ref, mu_ref, w0_ref, waup_ref, a0_ref, gup_ref,
                   kk_ref, ka_ref, rk_ref, g64_ref,
                   r_o, k_o, v_o, ld_o, kn_o, b_o, g_o, bg_o, u_o, gates_o, carry, *, rows):
    @pl.when(pl.program_id(1) == 0)
    def _():
        carry[...] = jnp.zeros_like(carry)

    x = x_ref[...]
    mod = mod_ref[...]
    sh1, sc1 = mod[0:1], mod[1:2]
    ms = jnp.mean(x * x, axis=-1, keepdims=True)
    h = x * lax.rsqrt(ms + RMS_EPS) * g1_ref[...]
    h = h * (1.0 + sc1) + sh1
    proj = jnp.dot(h.astype(BF16), win_ref[...], preferred_element_type=F32)

    p = proj[:, :SHIFT_COLS]
    row = lax.broadcasted_iota(jnp.int32, p.shape, 0)
    prev = jnp.where(row == 0, carry[...], pltpu.roll(p, 1, axis=0))
    carry[...] = p[rows - 1:rows, :]
    ps = p + (prev - p) * mu_ref[...]

    w = RWKV_WIDTH
    r, k, v = ps[:, 0:w], ps[:, w:2 * w], ps[:, 2 * w:3 * w]
    wa = ps[:, 3 * w:3 * w + LORA_W + LORA_A]
    lane = lax.broadcasted_iota(jnp.int32, wa.shape, 1)
    wa = jnp.where(lane < LORA_W, jnp.tanh(wa), wa)
    delta = _dot(wa, waup_ref[...])
    w_raw = w0_ref[...] + delta[:, :w]
    ld = -math.exp(-0.5) * jax.nn.sigmoid(w_raw)
    eta = jax.nn.sigmoid(a0_ref[...] + delta[:, w:])
    gd = ps[:, 3 * w + LORA_W + LORA_A:SHIFT_COLS]
    g = _dot(jax.nn.sigmoid(gd), gup_ref[...])

    g64 = g64_ref[...]
    kk = k * kk_ref[...]
    kn = kk * lax.rsqrt(_head_sum(kk * kk, g64) + L2_EPS)
    k2 = k * (1.0 + (eta - 1.0) * ka_ref[...])
    bonus = _head_sum(r * k2 * rk_ref[...], g64) * v

    r_o[...] = r
    k_o[...] = k2
    v_o[...] = v
    ld_o[...] = ld
    kn_o[...] = kn
    b_o[...] = kn * eta
    g_o[...] = g
    bg_o[...] = bonus * g
    u_o[...] = proj[:, SHIFT_COLS:SHIFT_COLS + S5_WIDTH]
    gates_o[...] = jax.nn.sigmoid(proj[:, SHIFT_COLS + S5_WIDTH:])


def _inproj(x2, mod3, g1, win, mu, w0, waup, a0, gup, k_k, k_a, r_k, g64, *, bsz, seq, rows):
    n = bsz * seq
    nt = seq // rows
    w = RWKV_WIDTH
    row_spec = lambda cols: pl.BlockSpec((rows, cols), lambda b, t: (b * nt + t, 0))
    outs = [jax.ShapeDtypeStruct((n, w), F32)] * 9 + [jax.ShapeDtypeStruct((n, 2 * D_MODEL), F32)]
    return pl.pallas_call(
        functools.partial(_inproj_kernel, rows=rows),
        grid=(bsz, nt),
        in_specs=[row_spec(D_MODEL),
                  pl.BlockSpec((None, 6, D_MODEL), lambda b, t: (b, 0, 0)),
                  _const_spec((1, D_MODEL)),
                  _const_spec((D_MODEL, IN_COLS)),
                  _const_spec((1, SHIFT_COLS)),
                  _const_spec((1, w)),
                  _const_spec((LORA_W + LORA_A, 2 * w)),
                  _const_spec((1, w)),
                  _const_spec((LORA_G, w)),
                  _const_spec((1, w)), _const_spec((1, w)), _const_spec((1, w)),
                  _const_spec((MXU_TILE, MXU_TILE))],
        out_specs=[row_spec(w)] * 9 + [row_spec(2 * D_MODEL)],
        out_shape=outs,
        scratch_shapes=[pltpu.VMEM((1, SHIFT_COLS), F32)],
        compiler_params=_params("arbitrary", "arbitrary"),
        name="inproj",
    )(x2, mod3, g1, win, mu, w0, waup, a0, gup, k_k, k_a, r_k, g64)


def _stack_heads(z):
    lane = lax.broadcasted_iota(jnp.int32, z.shape, 1)
    first = lane < RWKV_HEAD
    return jnp.concatenate([jnp.where(first, z, 0.0), jnp.where(first, 0.0, z)], axis=0)


def _cumsum_rows(x):
    n = x.shape[0]
    row = lax.broadcasted_iota(jnp.int32, x.shape, 0)
    d = 1
    while d < n:
        if d < SUBLANES:
            shifted = jnp.where(row >= d, pltpu.roll(x, d, axis=0), 0.0)
        else:
            shifted = jnp.concatenate([jnp.zeros((d, x.shape[1]), x.dtype), x[:n - d]], axis=0)
        x = x + shifted
        d *= 2
    return x


_WKV_STAGE_BOUNDARIES = 6


def _spread_matrix():
    src = jnp.arange(LANES)
    dst = jnp.arange((SUBLANES - 1) * LANES)
    j, lane = dst // LANES, dst % LANES
    hit = (src[:, None] // SUBLANES == lane[None, :] // SUBLANES) & (src[:, None] % SUBLANES == j[None, :])
    return hit.astype(BF16)


def _wkv_terms(chains, spread, side=()):
    t = WKV_CHUNK
    n = STACK
    jobs = list(side)
    calls = [0]

    def between():
        k = calls[0]
        calls[0] += 1
        for _ in range((k + 1) * len(side) // _WKV_STAGE_BOUNDARIES - k * len(side) // _WKV_STAGE_BOUNDARIES):
            jobs.pop(0)()

    ri = lax.broadcasted_iota(jnp.int32, (n, n), 0)
    ci = lax.broadcasted_iota(jnp.int32, (n, n), 1)
    same_head = (ri // t) == (ci // t)
    strict = same_head & (ri > ci)
    incl = same_head & (ri >= ci)

    a_s, r_s, v_s, lhs, rhs, bk, w_end = [], [], [], [], [], [], []
    for r, k, v, ld, kn, b, cl in chains:
        cl_end = cl[t - 1:t, :]
        e_incl = jnp.exp(cl)
        e_excl = jnp.exp(cl - ld)
        e_neg = jnp.exp(-cl)
        e_end = jnp.exp(cl_end - cl)
        a_s.append(_stack_heads(-kn * e_excl))
        r_s.append(_stack_heads(r * e_incl))
        v_s.append(_stack_heads(v))
        lhs.append(jnp.concatenate([a_s[-1], r_s[-1]], axis=0).astype(BF16))
        bd, kd = (b * e_neg).astype(BF16), (k * e_neg).astype(BF16)
        rhs.append(jnp.concatenate([bd, bd, kd, kd], axis=0))
        bk.append(jnp.concatenate([_stack_heads(b * e_end), _stack_heads(k * e_end)], axis=0).astype(BF16))
        w_end.append(jnp.exp(cl_end))

    amat = [lax.dot_general(l, rr, (((1,), (1,)), ((), ())), preferred_element_type=F32)
            for l, rr in zip(lhs, rhs)]
    a_ab = [jnp.where(strict, a[:n, :n], 0.0) for a in amat]
    a_ak = [jnp.where(strict, a[:n, n:], 0.0).astype(BF16) for a in amat]
    a_r = [jnp.concatenate([jnp.where(incl, a[n:, :n], 0.0), jnp.where(incl, a[n:, n:], 0.0)],
                           axis=1).astype(BF16) for a in amat]

    base = SUBLANES
    blk = (ri // base) == (ci // base)
    packed = []
    for a in a_ab:
        d = jnp.where(blk, a, 0.0)
        acc = d[0:base]
        for q in range(1, n // base):
            acc = acc + d[q * base:(q + 1) * base]
        packed.append(acc)
    cols = jnp.dot(jnp.concatenate(packed, axis=0).astype(BF16), spread,
                   preferred_element_type=F32)
    sub = lax.broadcasted_iota(jnp.int32, (base, LANES), 0)
    lane = lax.broadcasted_iota(jnp.int32, (base, LANES), 1)
    eye_packed = jnp.where(sub == lane % base, 1.0, 0.0)
    x = []
    for i in range(len(a_ab)):
        xp = eye_packed
        for j in range(base - 1):
            lj = cols[i * base:(i + 1) * base, j * LANES:(j + 1) * LANES]
            xp = xp + lj * jnp.broadcast_to(xp[j:j + 1, :], (base, LANES))
        x.append(jnp.where(blk, jnp.concatenate([xp] * (n // base), axis=0), 0.0))
    between()
    s = base
    while s < t:
        level = ((ri // (2 * s)) == (ci // (2 * s))) & ((ri // s) % 2 == 1) & ((ci // s) % 2 == 0)
        c = [jnp.where(level, a, 0.0).astype(BF16) for a in a_ab]
        xb = [xi.astype(BF16) for xi in x]
        odd = lambda m: jnp.concatenate([m[q * s:(q + 1) * s] for q in range(1, n // s, 2)], axis=0)
        xo = [odd(xi) for xi in x]
        xc = [jnp.dot(xoi.astype(BF16), cc, preferred_element_type=F32).astype(BF16)
              for xoi, cc in zip(xo, c)]
        xo = [xoi + jnp.dot(xci, xbi, preferred_element_type=F32) for xoi, xci, xbi in zip(xo, xc, xb)]
        x = [jnp.concatenate([xoi[(q // 2) * s:(q // 2 + 1) * s] if q % 2 else xi[q * s:(q + 1) * s]
                              for q in range(n // s)], axis=0) for xi, xoi in zip(x, xo)]
        between()
        s *= 2

    akv = [jnp.dot(a, v.astype(BF16), preferred_element_type=F32) for a, v in zip(a_ak, v_s)]
    tg = [_dot(xi, jnp.concatenate([a, kv], axis=1)) for xi, a, kv in zip(x, a_s, akv)]
    between()
    low = [jnp.concatenate([g, jnp.concatenate([jnp.zeros_like(v), v], axis=1)], axis=0).astype(BF16)
           for g, v in zip(tg, v_s)]
    out1 = [jnp.dot(a, lo, preferred_element_type=F32) for a, lo in zip(a_r, low)]
    out2 = [lax.dot_general(bb, lo, (((0,), (0,)), ((), ())), preferred_element_type=F32)
            for bb, lo in zip(bk, low)]
    between()
    ki = lax.broadcasted_iota(jnp.int32, (LANES, LANES), 0)
    kj = lax.broadcasted_iota(jnp.int32, (LANES, LANES), 1)
    terms = []
    for rs, o1, o2, we in zip(r_s, out1, out2, w_end):
        r_hat = rs + o1[:, :LANES]
        m = o2[:, :LANES] + jnp.where(ki == kj, we, 0.0)
        terms.append((r_hat, o1[:, LANES:], m, o2[:, LANES:]))
    return terms


def _wkv_kernel(r_ref, k_ref, v_ref, ld_ref, kn_ref, b_ref, spread_ref, y_ref, s_ref, mr_ref, nc_ref,
                yv_ref, *, rows, group):
    t = WKV_CHUNK
    npair = RWKV_WIDTH // LANES

    @pl.when(pl.program_id(1) == 0)
    def _():
        s_ref[...] = jnp.zeros_like(s_ref)

    def chain_step(c):
        rs = pl.ds(_aligned(c * STACK, STACK), STACK)
        outs = [jnp.dot(mr_ref[p, pl.ds(_aligned(c * 2 * STACK, 2 * STACK), 2 * STACK), :],
                        s_ref[p].astype(BF16), preferred_element_type=F32) for p in range(npair)]
        for p, out in enumerate(outs):
            s_ref[p] = out[:LANES] + nc_ref[p, rs, :]
            ys = out[LANES:] + yv_ref[p, rs, :]
            y_ref[pl.ds(_aligned(c * t, t), t), p * LANES:(p + 1) * LANES] = ys[:t] + ys[t:]

    def precompute(g, side):
        chains, where = [], []
        for j in range(group):
            c = g * group + j
            sl = pl.ds(_aligned(c * t, t), t)
            tiles = [ref[sl, :] for ref in (r_ref, k_ref, v_ref, ld_ref, kn_ref, b_ref)]
            cl = _cumsum_rows(tiles[3])
            for p in range(npair):
                ls = slice(p * LANES, (p + 1) * LANES)
                chains.append(tuple(a[:, ls] for a in tiles) + (cl[:, ls],))
                where.append((c, p))
        for (c, p), (r_hat, y_v, m, n_c) in zip(where, _wkv_terms(chains, spread_ref[...], side)):
            mr_ref[p, pl.ds(_aligned(c * 2 * STACK, 2 * STACK), 2 * STACK), :] = (
                jnp.concatenate([m, r_hat], axis=0).astype(BF16))
            rs = pl.ds(_aligned(c * STACK, STACK), STACK)
            nc_ref[p, rs, :] = n_c
            yv_ref[p, rs, :] = y_v

    ngroup = rows // (t * group)
    precompute(0, ())

    def body(g, _):
        precompute(g, [functools.partial(chain_step, (g - 1) * group + j) for j in range(group)])
        return 0

    lax.fori_loop(1, ngroup, body, 0)
    for j in range(group):
        chain_step((ngroup - 1) * group + j)


def _wkv(r, k, v, ld, kn, b, *, bsz, seq, rows):
    n = bsz * seq
    nt = seq // rows
    npair = RWKV_WIDTH // LANES
    nchunk = rows // WKV_CHUNK
    spec = pl.BlockSpec((rows, RWKV_WIDTH), lambda bi, ti: (bi * nt + ti, 0))
    return pl.pallas_call(
        functools.partial(_wkv_kernel, rows=rows, group=4),
        grid=(bsz, nt),
        in_specs=[spec] * 6 + [_const_spec((LANES, (SUBLANES - 1) * LANES))],
        out_specs=spec,
        out_shape=jax.ShapeDtypeStruct((n, RWKV_WIDTH), F32),
        scratch_shapes=[pltpu.VMEM((npair, LANES, LANES), F32),
                        pltpu.VMEM((npair, nchunk * 2 * STACK, LANES), BF16),
                        pltpu.VMEM((npair, nchunk * STACK, LANES), F32),
                        pltpu.VMEM((npair, nchunk * STACK, LANES), F32)],
        compiler_params=_params("arbitrary", "arbitrary"),
        name="wkv",
    )(r, k, v, ld, kn, b, _spread_matrix())


def _gelu_tanh(x):
    return 0.5 * x * (1.0 + jnp.tanh(math.sqrt(2.0 / math.pi) * (x + 0.044715 * x * x * x)))


def _s5_kernel(u_ref, wre_ref, wim_ref, cc_ref, ar_ref, ai_ref, d_ref, y_ref,
               xre, xim, st_re, st_im, *, steps):
    @pl.when(pl.program_id(1) == 0)
    def _():
        st_re[...] = jnp.zeros_like(st_re)
        st_im[...] = jnp.zeros_like(st_im)

    rows = steps * SUBLANES
    u = u_ref[...].reshape(rows, S5_WIDTH)
    ub = u.astype(BF16)
    ns = S5_SLAB_STATE
    for j in range(S5_SLABS):
        uj = ub[:, j * LANES:(j + 1) * LANES]
        xre[:, j * ns:(j + 1) * ns] = jnp.dot(uj, wre_ref[j], preferred_element_type=F32)
        xim[:, j * ns:(j + 1) * ns] = jnp.dot(uj, wim_ref[j], preferred_element_type=F32)

    half = S5_GROUPS * S5_STATE // 2
    for c in range(2):
        cs = slice(c * half, (c + 1) * half)
        ar = jnp.broadcast_to(ar_ref[:, cs], (SUBLANES, half))
        ai = jnp.broadcast_to(ai_ref[:, cs], (SUBLANES, half))

        def step(i, state, cs=cs, ar=ar, ai=ai):
            x_re, x_im = state
            rs = pl.ds(pl.multiple_of(i * SUBLANES, SUBLANES), SUBLANES)
            n_re = ar * x_re - ai * x_im + xre[rs, cs]
            n_im = ar * x_im + ai * x_re + xim[rs, cs]
            xre[rs, cs] = n_re
            xim[rs, cs] = n_im
            return n_re, n_im

        x_re, x_im = lax.fori_loop(0, steps, step, (st_re[:, cs], st_im[:, cs]), unroll=2)
        st_re[:, cs] = x_re
        st_im[:, cs] = x_im

    for j in range(S5_SLABS):
        xs = jnp.concatenate([xre[:, j * ns:(j + 1) * ns], xim[:, j * ns:(j + 1) * ns]], axis=1)
        ls = slice(j * LANES, (j + 1) * LANES)
        yj = jnp.dot(xs.astype(BF16), cc_ref[j], preferred_element_type=F32) + d_ref[:, ls] * u[:, ls]
        y_ref[:, :, ls] = _gelu_tanh(yj).reshape(steps, SUBLANES, LANES)


def _s5(u_t, wre, wim, cc, ar, ai, d, *, bsz, seq, steps):
    nstate = S5_GROUPS * S5_STATE
    rows = steps * SUBLANES
    spec = pl.BlockSpec((steps, SUBLANES, S5_WIDTH), lambda g, t: (t, g, 0))
    return pl.pallas_call(
        functools.partial(_s5_kernel, steps=steps),
        grid=(bsz // SUBLANES, seq // steps),
        in_specs=[spec,
                  _const_spec((S5_SLABS, LANES, S5_SLAB_STATE)),
                  _const_spec((S5_SLABS, LANES, S5_SLAB_STATE)),
                  _const_spec((S5_SLABS, 2 * S5_SLAB_STATE, LANES)),
                  _const_spec((1, nstate)), _const_spec((1, nstate)),
                  _const_spec((1, S5_WIDTH))],
        out_specs=spec,
        out_shape=jax.ShapeDtypeStruct((seq, bsz, S5_WIDTH), F32),
        scratch_shapes=[pltpu.VMEM((rows, nstate), F32), pltpu.VMEM((rows, nstate), F32),
                        pltpu.VMEM((SUBLANES, nstate), F32), pltpu.VMEM((SUBLANES, nstate), F32)],
        compiler_params=_params("arbitrary", "arbitrary"),
        name="s5",
    )(u_t, wre, wim, cc, ar, ai, d)


def _merge_kernel(y_ref, g_ref, bg_ref, ys5_ref, gates_ref, x_ref, mod_ref, lng_ref, lnb_ref,
                  gmean_ref, wrw_ref, wglu_ref, wout_ref, o_ref):
    gm = gmean_ref[...]
    y = y_ref[...]
    mean = _head_sum(y, gm, split=True)
    d = y - mean
    var = _head_sum(d * d, gm)
    yn = d * lax.rsqrt(var + GN_EPS) * lng_ref[...] + lnb_ref[...]
    ya = _dot(yn * g_ref[...] + bg_ref[...], wrw_ref[...])
    z = _dot(ys5_ref[...], wglu_ref[...])
    yb = z[:, :D_MODEL] * jax.nn.sigmoid(z[:, D_MODEL:])
    gates = gates_ref[...]
    mixed = _dot(gates[:, :D_MODEL] * ya + gates[:, D_MODEL:] * yb, wout_ref[...])
    gt1 = mod_ref[...][2:3]
    o_ref[...] = x_ref[...] + gt1 * mixed


def _merge(y, g, bg, ys5, gates, x2, mod3, lng, lnb, gmean, wrw, wglu, wout, *, bsz, seq, rows):
    n = bsz * seq
    nt = seq // rows
    w = RWKV_WIDTH
    row_spec = lambda cols: pl.BlockSpec((rows, cols), lambda i: (i, 0))
    return pl.pallas_call(
        _merge_kernel,
        grid=(n // rows,),
        in_specs=[row_spec(w), row_spec(w), row_spec(w), row_spec(w), row_spec(2 * D_MODEL),
                  row_spec(D_MODEL),
                  pl.BlockSpec((None, 6, D_MODEL), lambda i: (i // nt, 0, 0)),
                  _const_spec((1, w)), _const_spec((1, w)), _const_spec((MXU_TILE, MXU_TILE)),
                  _const_spec((w, D_MODEL)), _const_spec((S5_WIDTH, 2 * D_MODEL)),
                  _const_spec((D_MODEL, D_MODEL))],
        out_specs=row_spec(D_MODEL),
        out_shape=jax.ShapeDtypeStruct((n, D_MODEL), F32),
        compiler_params=_params("arbitrary"),
        name="merge",
    )(y, g, bg, ys5, gates, x2, mod3, lng, lnb, gmean, wrw, wglu, wout)


def _ffn_kernel(x_ref, mod_ref, g2_ref, wup_ref, cw_ref, cb_ref, wdn_ref, gf_ref, o_ref, buf, *, rows):
    pad = SUBLANES

    @pl.when(pl.program_id(1) == 0)
    def _():
        buf[:, 0:pad, :] = jnp.zeros((FFN_SPLIT, pad, 2 * FFN_CH), F32)

    mod = mod_ref[...]
    sh2, sc2, gt2 = mod[3:4], mod[4:5], mod[5:6]

    def normed(xh):
        ms = jnp.mean(xh * xh, axis=-1, keepdims=True)
        h = xh * lax.rsqrt(ms + RMS_EPS) * g2_ref[...]
        return (h * (1.0 + sc2) + sh2).astype(BF16)

    halves = 2
    hr = rows // halves
    x = [x_ref[h * hr:(h + 1) * hr, :] for h in range(halves)]
    hb = [normed(x[0]), None]
    acc = [jnp.zeros((hr, D_MODEL), F32) for _ in range(halves)]
    for c in range(FFN_SPLIT):
        for h in range(halves):
            if hb[h] is None:
                hb[h] = normed(x[h])
            r0 = pad + h * hr
            buf[c, r0:r0 + hr, :] = jnp.dot(hb[h], wup_ref[c], preferred_element_type=F32)
        cw = cw_ref[c]
        for h in range(halves):
            hid = cb_ref[c]
            for j in range(CONV_W):
                off = pad + h * hr - (CONV_W - 1) + j
                hid = hid + cw[j:j + 1, :] * buf[c, off:off + hr, :]
            gate, up = hid[:, :FFN_CH], hid[:, FFN_CH:]
            act = (0.5 * gate) * (1.0 + jnp.tanh(0.5 * gate)) * up
            acc[h] = acc[h] + jnp.dot(act.astype(BF16), wdn_ref[c], preferred_element_type=F32)
        buf[c, pad - (CONV_W - 1):pad, :] = buf[c, pad + rows - (CONV_W - 1):pad + rows, :]

    for h in range(halves):
        x2 = x[h] + gt2 * acc[h]
        ms2 = jnp.mean(x2 * x2, axis=-1, keepdims=True)
        o_ref[h * hr:(h + 1) * hr, :] = x2 * lax.rsqrt(ms2 + RMS_EPS) * gf_ref[...]


def _ffn(x1, mod3, g2, wup, cw, cb, wdn, gf, *, bsz, seq, rows):
    n = bsz * seq
    nt = seq // rows
    row_spec = pl.BlockSpec((rows, D_MODEL), lambda b, t: (b * nt + t, 0))
    return pl.pallas_call(
        functools.partial(_ffn_kernel, rows=rows),
        grid=(bsz, nt),
        in_specs=[row_spec,
                  pl.BlockSpec((None, 6, D_MODEL), lambda b, t: (b, 0, 0)),
                  _const_spec((1, D_MODEL)),
                  _const_spec((FFN_SPLIT, D_MODEL, 2 * FFN_CH)),
                  _const_spec((FFN_SPLIT, CONV_W, 2 * FFN_CH)),
                  _const_spec((FFN_SPLIT, 1, 2 * FFN_CH)),
                  _const_spec((FFN_SPLIT, FFN_CH, D_MODEL)),
                  _const_spec((1, D_MODEL))],
        out_specs=row_spec,
        out_shape=jax.ShapeDtypeStruct((n, D_MODEL), F32),
        scratch_shapes=[pltpu.VMEM((FFN_SPLIT, rows + SUBLANES, 2 * FFN_CH), F32)],
        compiler_params=_params("arbitrary", "arbitrary"),
        name="ffn",
    )(x1, mod3, g2, wup, cw, cb, wdn, gf)


def _split_cols(a):
    parts = [jnp.concatenate([a[..., c * FFN_CH:(c + 1) * FFN_CH],
                              a[..., D_FF + c * FFN_CH:D_FF + (c + 1) * FFN_CH]], axis=-1)
             for c in range(FFN_SPLIT)]
    return jnp.stack(parts, axis=0)


def _block_diag_ones(n, blk, value, dtype):
    i = jnp.arange(n) // blk
    return jnp.where(i[:, None] == i[None, :], value, 0.0).astype(dtype)


def kernel(x, c, w_ada, b_ada, norm1_g, w_in, mu_shift, rwkv_w0, rwkv_w_up, rwkv_a0, rwkv_a_up,
           rwkv_g_up, rwkv_k_k, rwkv_k_a, rwkv_r_k, rwkv_ln_g, rwkv_ln_b, w_out_rwkv, s5_a_re,
           s5_a_im, s5_log_dt, s5_b_re, s5_b_im, s5_c_re, s5_c_im, s5_d, w_glu, w_out, norm2_g,
           w_ffn_up, ffn_conv_w, ffn_conv_b, w_ffn_down, norm_f_g):
    bsz, seq, _ = x.shape
    depth = w_ada.shape[0]
    n = bsz * seq
    w = RWKV_WIDTH
    rows_a = min(256, seq)
    rows_m = min(512, seq)
    rows_f = min(512, seq)
    s5_steps = min(64, seq)

    g64 = _block_diag_ones(MXU_TILE, RWKV_HEAD, 1.0, BF16)
    gmean = _block_diag_ones(MXU_TILE, RWKV_HEAD, 1.0 / RWKV_HEAD, BF16)
    eye8 = jnp.eye(LANES // S5_GROUP, dtype=F32)
    row = lambda a: a.reshape(1, -1)

    assert depth == 1, depth
    x2 = x.reshape(n, D_MODEL)
    for l in range(depth):
        mod3 = _adaln(c, w_ada[l], b_ada[l]).reshape(bsz, 6, D_MODEL)

        zero = jnp.zeros((LORA_W, w), F32)
        waup = jnp.concatenate([jnp.concatenate([rwkv_w_up[l], zero], axis=1),
                                jnp.concatenate([zero, rwkv_a_up[l]], axis=1)], axis=0).astype(BF16)
        r, k, v, ld, kn, b, g, bg, u, gates = _inproj(
            x2, mod3, row(norm1_g[l]), w_in[l].astype(BF16), row(mu_shift[l]), row(rwkv_w0[l]), waup,
            row(rwkv_a0[l]), rwkv_g_up[l].astype(BF16), row(rwkv_k_k[l]), row(rwkv_k_a[l]),
            row(rwkv_r_k[l]), g64, bsz=bsz, seq=seq, rows=rows_a)

        y = _wkv(r, k, v, ld, kn, b, bsz=bsz, seq=seq, rows=min(1024, seq))

        ab_re, ab_im, bb_re, bb_im = _s5prep(s5_a_re[l], s5_a_im[l], s5_log_dt[l], s5_b_re[l], s5_b_im[l])
        gl = LANES // S5_GROUP
        slab_in = lambda bb: jnp.einsum('jgcp,gh->jgchp', bb.reshape(S5_SLABS, gl, S5_GROUP, S5_STATE),
                                        eye8).reshape(S5_SLABS, LANES, S5_SLAB_STATE).astype(BF16)
        slab_out = lambda cc: jnp.einsum('jgcp,gh->jgphc', cc.reshape(S5_SLABS, gl, S5_GROUP, S5_STATE),
                                         eye8).reshape(S5_SLABS, S5_SLAB_STATE, LANES)
        cc = jnp.concatenate([slab_out(s5_c_re[l]), -slab_out(s5_c_im[l])], axis=1).astype(BF16)
        u_t = jnp.transpose(u.reshape(bsz, seq, S5_WIDTH), (1, 0, 2))
        ys5_t = _s5(u_t, slab_in(bb_re), slab_in(bb_im), cc, row(ab_re), row(ab_im), row(s5_d[l]),
                    bsz=bsz, seq=seq, steps=s5_steps)
        ys5 = jnp.transpose(ys5_t, (1, 0, 2)).reshape(n, S5_WIDTH)

        x1 = _merge(y, g, bg, ys5, gates, x2, mod3, row(rwkv_ln_g[l]), row(rwkv_ln_b[l]), gmean,
                    w_out_rwkv[l].astype(BF16), w_glu[l].astype(BF16), w_out[l].astype(BF16),
                    bsz=bsz, seq=seq, rows=rows_m)

        gf = row(norm_f_g)
        wdn = w_ffn_down[l].astype(BF16).reshape(FFN_SPLIT, FFN_CH, D_MODEL)
        x2 = _ffn(x1, mod3, row(norm2_g[l]), _split_cols(w_ffn_up[l]).astype(BF16),
                  _split_cols(ffn_conv_w[l]), _split_cols(ffn_conv_b[l].reshape(1, -1)), wdn,
                  gf, bsz=bsz, seq=seq, rows=rows_f)
    return x2.reshape(bsz, seq, D_MODEL)
```

```python
import functools
import math

import jax
import jax.numpy as jnp
from jax import lax
from jax.experimental import pallas as pl
from jax.experimental.pallas import tpu as pltpu

F32 = jnp.float32
BF16 = jnp.bfloat16
HIGHEST = lax.Precision.HIGHEST

D_MODEL = 1024
RWKV_WIDTH = 512
RWKV_HEAD = 64
LORA_W = 64
LORA_A = 64
LORA_G = 128
S5_WIDTH = 512
S5_GROUP = 16
S5_GROUPS = 32
S5_STATE = 64
D_FF = 2816
CONV_W = 3
RMS_EPS = 1e-6
GN_EPS = 64e-5
L2_EPS = 1e-12
SHIFT_COLS = 3 * RWKV_WIDTH + LORA_W + LORA_A + LORA_G
IN_COLS = SHIFT_COLS + S5_WIDTH + 2 * D_MODEL

LANES = 128
SUBLANES = 8
MXU_TILE = 256
WKV_CHUNK = 64
HEADS_PER_TILE = LANES // RWKV_HEAD
STACK = HEADS_PER_TILE * WKV_CHUNK
S5_SLABS = S5_WIDTH // LANES
S5_SLAB_STATE = (LANES // S5_GROUP) * S5_STATE
S5_TILES = 2
FFN_SPLIT = 2
FFN_CH = D_FF // FFN_SPLIT
VMEM_LIMIT = 56 * 1024 * 1024


def _dot(a, b):
    return jnp.dot(a.astype(BF16), b.astype(BF16), preferred_element_type=F32)


def _dot_split(a, g):
    hi = a.astype(BF16)
    lo = (a - hi.astype(F32)).astype(BF16)
    return (jnp.dot(hi, g, preferred_element_type=F32)
            + jnp.dot(lo, g, preferred_element_type=F32))


def _head_sum(a, g, split=False):
    f = _dot_split if split else _dot
    return jnp.concatenate([f(a[:, i:i + MXU_TILE], g) for i in range(0, a.shape[1], MXU_TILE)], axis=1)


def _aligned(i, m):
    return i if isinstance(i, int) else pl.multiple_of(i, m)


def _const_spec(shape):
    nd = len(shape)
    return pl.BlockSpec(shape, lambda *_: (0,) * nd, pipeline_mode=pl.Buffered(1))


def _params(*sem):
    return pltpu.CompilerParams(dimension_semantics=sem, vmem_limit_bytes=VMEM_LIMIT)


def _adaln_kernel(c_ref, w_ref, b_ref, o_ref):
    c = c_ref[...]
    s = c * jax.nn.sigmoid(c)
    o_ref[...] = jnp.dot(s, w_ref[...], preferred_element_type=F32, precision=HIGHEST) + b_ref[...]


def _adaln(c, w_ada, b_ada):
    bsz = c.shape[0]
    n = w_ada.shape[1]
    bn = D_MODEL
    return pl.pallas_call(
        _adaln_kernel,
        grid=(n // bn,),
        in_specs=[pl.BlockSpec((bsz, D_MODEL), lambda j: (0, 0)),
                  pl.BlockSpec((D_MODEL, bn), lambda j: (0, j)),
                  pl.BlockSpec((1, bn), lambda j: (0, j))],
        out_specs=pl.BlockSpec((bsz, bn), lambda j: (0, j)),
        out_shape=jax.ShapeDtypeStruct((bsz, n), F32),
        compiler_params=_params("arbitrary"),
        name="adaln",
    )(c, w_ada, b_ada.reshape(1, n))


def _s5prep_kernel(are_ref, aim_ref, ldt_ref, bre_ref, bim_ref, abre_o, abim_o, bbre_o, bbim_o):
    a_re = are_ref[...]
    a_im = aim_ref[...]
    dt = jnp.exp(ldt_ref[...])
    z_re, z_im = a_re * dt, a_im * dt
    mag = jnp.exp(z_re)
    ab_re, ab_im = mag * jnp.cos(z_im), mag * jnp.sin(z_im)
    den = a_re * a_re + a_im * a_im
    q_re = ((ab_re - 1.0) * a_re + ab_im * a_im) / den
    q_im = (ab_im * a_re - (ab_re - 1.0) * a_im) / den
    abre_o[...] = ab_re
    abim_o[...] = ab_im
    b_re = bre_ref[...]
    b_im = bim_ref[...]
    bbre_o[...] = q_re[:, None, :] * b_re - q_im[:, None, :] * b_im
    bbim_o[...] = q_re[:, None, :] * b_im + q_im[:, None, :] * b_re


def _s5prep(a_re, a_im, log_dt, b_re, b_im):
    g, p = a_re.shape
    c = b_re.shape[-1]
    b_re_t = jnp.transpose(b_re, (0, 2, 1))
    b_im_t = jnp.transpose(b_im, (0, 2, 1))
    return pl.pallas_call(
        _s5prep_kernel,
        out_shape=(jax.ShapeDtypeStruct((g, p), F32), jax.ShapeDtypeStruct((g, p), F32),
                   jax.ShapeDtypeStruct((g, c, p), F32), jax.ShapeDtypeStruct((g, c, p), F32)),
        name="s5prep",
    )(a_re, a_im, log_dt.reshape(g, 1), b_re_t, b_im_t)


def _inproj_kernel(x_ref, mod_ref, g1_ref, win_ref, mu_ref, w0_ref, waup_ref, a0_ref, gup_ref,
                   kk_ref, ka_ref, rk_ref, g64_ref,
                   r_o, k_o, v_o, ld_o, kn_o, b_o, g_o, bg_o, u_o, gates_o, carry, *, rows):
    @pl.when(pl.program_id(1) == 0)
    def _():
        carry[...] = jnp.zeros_like(carry)

    x = x_ref[...]
    mod = mod_ref[...]
    sh1, sc1 = mod[0:1], mod[1:2]
    ms = jnp.mean(x * x, axis=-1, keepdims=True)
    h = x * lax.rsqrt(ms + RMS_EPS) * g1_ref[...]
    h = h * (1.0 + sc1) + sh1
    proj = jnp.dot(h.astype(BF16), win_ref[...], preferred_element_type=F32)

    p = proj[:, :SHIFT_COLS]
    row = lax.broadcasted_iota(jnp.int32, p.shape, 0)
    prev = jnp.where(row == 0, carry[...], pltpu.roll(p, 1, axis=0))
    carry[...] = p[rows - 1:rows, :]
    ps = p + (prev - p) * mu_ref[...]

    w = RWKV_WIDTH
    r, k, v = ps[:, 0:w], ps[:, w:2 * w], ps[:, 2 * w:3 * w]
    wa = ps[:, 3 * w:3 * w + LORA_W + LORA_A]
    lane = lax.broadcasted_iota(jnp.int32, wa.shape, 1)
    wa = jnp.where(lane < LORA_W, jnp.tanh(wa), wa)
    delta = _dot(wa, waup_ref[...])
    w_raw = w0_ref[...] + delta[:, :w]
    ld = -math.exp(-0.5) * jax.nn.sigmoid(w_raw)
    eta = jax.nn.sigmoid(a0_ref[...] + delta[:, w:])
    gd = ps[:, 3 * w + LORA_W + LORA_A:SHIFT_COLS]
    g = _dot(jax.nn.sigmoid(gd), gup_ref[...])

    g64 = g64_ref[...]
    kk = k * kk_ref[...]
    kn = kk * lax.rsqrt(_head_sum(kk * kk, g64) + L2_EPS)
    k2 = k * (1.0 + (eta - 1.0) * ka_ref[...])
    bonus = _head_sum(r * k2 * rk_ref[...], g64) * v

    r_o[...] = r
    k_o[...] = k2
    v_o[...] = v
    ld_o[...] = ld
    kn_o[...] = kn
    b_o[...] = kn * eta
    g_o[...] = g
    bg_o[...] = bonus * g
    u_o[...] = proj[:, SHIFT_COLS:SHIFT_COLS + S5_WIDTH]
    gates_o[...] = jax.nn.sigmoid(proj[:, SHIFT_COLS + S5_WIDTH:])


def _inproj(x2, mod3, g1, win, mu, w0, waup, a0, gup, k_k, k_a, r_k, g64, *, bsz, seq, rows):
    n = bsz * seq
    nt = seq // rows
    w = RWKV_WIDTH
    row_spec = lambda cols: pl.BlockSpec((rows, cols), lambda b, t: (b * nt + t, 0))
    outs = [jax.ShapeDtypeStruct((n, w), F32)] * 9 + [jax.ShapeDtypeStruct((n, 2 * D_MODEL), F32)]
    return pl.pallas_call(
        functools.partial(_inproj_kernel, rows=rows),
        grid=(bsz, nt),
        in_specs=[row_spec(D_MODEL),
                  pl.BlockSpec((None, 6, D_MODEL), lambda b, t: (b, 0, 0)),
                  _const_spec((1, D_MODEL)),
                  _const_spec((D_MODEL, IN_COLS)),
                  _const_spec((1, SHIFT_COLS)),
                  _const_spec((1, w)),
                  _const_spec((LORA_W + LORA_A, 2 * w)),
                  _const_spec((1, w)),
                  _const_spec((LORA_G, w)),
                  _const_spec((1, w)), _const_spec((1, w)), _const_spec((1, w)),
                  _const_spec((MXU_TILE, MXU_TILE))],
        out_specs=[row_spec(w)] * 9 + [row_spec(2 * D_MODEL)],
        out_shape=outs,
        scratch_shapes=[pltpu.VMEM((1, SHIFT_COLS), F32)],
        compiler_params=_params("arbitrary", "arbitrary"),
        name="inproj",
    )(x2, mod3, g1, win, mu, w0, waup, a0, gup, k_k, k_a, r_k, g64)


def _stack_heads(z):
    lane = lax.broadcasted_iota(jnp.int32, z.shape, 1)
    first = lane < RWKV_HEAD
    return jnp.concatenate([jnp.where(first, z, 0.0), jnp.where(first, 0.0, z)], axis=0)


def _cumsum_rows(x):
    n = x.shape[0]
    row = lax.broadcasted_iota(jnp.int32, x.shape, 0)
    d = 1
    while d < n:
        if d < SUBLANES:
            shifted = jnp.where(row >= d, pltpu.roll(x, d, axis=0), 0.0)
        else:
            shifted = jnp.concatenate([jnp.zeros((d, x.shape[1]), x.dtype), x[:n - d]], axis=0)
        x = x + shifted
        d *= 2
    return x


_WKV_STAGE_BOUNDARIES = 6


def _spread_matrix():
    src = jnp.arange(LANES)
    dst = jnp.arange((SUBLANES - 1) * LANES)
    j, lane = dst // LANES, dst % LANES
    hit = (src[:, None] // SUBLANES == lane[None, :] // SUBLANES) & (src[:, None] % SUBLANES == j[None, :])
    return hit.astype(BF16)


def _wkv_terms(chains, spread, side=()):
    t = WKV_CHUNK
    n = STACK
    jobs = list(side)
    calls = [0]

    def between():
        k = calls[0]
        calls[0] += 1
        for _ in range((k + 1) * len(side) // _WKV_STAGE_BOUNDARIES - k * len(side) // _WKV_STAGE_BOUNDARIES):
            jobs.pop(0)()

    ri = lax.broadcasted_iota(jnp.int32, (n, n), 0)
    ci = lax.broadcasted_iota(jnp.int32, (n, n), 1)
    same_head = (ri // t) == (ci // t)
    strict = same_head & (ri > ci)
    incl = same_head & (ri >= ci)

    a_s, r_s, v_s, lhs, rhs, bk, w_end = [], [], [], [], [], [], []
    for r, k, v, ld, kn, b, cl in chains:
        cl_end = cl[t - 1:t, :]
        e_incl = jnp.exp(cl)
        e_excl = jnp.exp(cl - ld)
        e_neg = jnp.exp(-cl)
        e_end = jnp.exp(cl_end - cl)
        a_s.append(_stack_heads(-kn * e_excl))
        r_s.append(_stack_heads(r * e_incl))
        v_s.append(_stack_heads(v))
        lhs.append(jnp.concatenate([a_s[-1], r_s[-1]], axis=0).astype(BF16))
        bd, kd = (b * e_neg).astype(BF16), (k * e_neg).astype(BF16)
        rhs.append(jnp.concatenate([bd, bd, kd, kd], axis=0))
        bk.append(jnp.concatenate([_stack_heads(b * e_end), _stack_heads(k * e_end)], axis=0).astype(BF16))
        w_end.append(jnp.exp(cl_end))

    amat = [lax.dot_general(l, rr, (((1,), (1,)), ((), ())), preferred_element_type=F32)
            for l, rr in zip(lhs, rhs)]
    a_ab = [jnp.where(strict, a[:n, :n], 0.0) for a in amat]
    a_ak = [jnp.where(strict, a[:n, n:], 0.0).astype(BF16) for a in amat]
    a_r = [jnp.concatenate([jnp.where(incl, a[n:, :n], 0.0), jnp.where(incl, a[n:, n:], 0.0)],
                           axis=1).astype(BF16) for a in amat]

    base = SUBLANES
    blk = (ri // base) == (ci // base)
    packed = []
    for a in a_ab:
        d = jnp.where(blk, a, 0.0)
        acc = d[0:base]
        for q in range(1, n // base):
            acc = acc + d[q * base:(q + 1) * base]
        packed.append(acc)
    cols = jnp.dot(jnp.concatenate(packed, axis=0).astype(BF16), spread,
                   preferred_element_type=F32)
    sub = lax.broadcasted_iota(jnp.int32, (base, LANES), 0)
    lane = lax.broadcasted_iota(jnp.int32, (base, LANES), 1)
    eye_packed = jnp.where(sub == lane % base, 1.0, 0.0)
    x = []
    for i in range(len(a_ab)):
        xp = eye_packed
        for j in range(base - 1):
            lj = cols[i * base:(i + 1) * base, j * LANES:(j + 1) * LANES]
            xp = xp + lj * jnp.broadcast_to(xp[j:j + 1, :], (base, LANES))
        x.append(jnp.where(blk, jnp.concatenate([xp] * (n // base), axis=0), 0.0))
    between()
    s = base
    while s < t:
        level = ((ri // (2 * s)) == (ci // (2 * s))) & ((ri // s) % 2 == 1) & ((ci // s) % 2 == 0)
        c = [jnp.where(level, a, 0.0).astype(BF16) for a in a_ab]
        xb = [xi.astype(BF16) for xi in x]
        odd = lambda m: jnp.concatenate([m[q * s:(q + 1) * s] for q in range(1, n // s, 2)], axis=0)
        xo = [odd(xi) for xi in x]
        xc = [jnp.dot(xoi.astype(BF16), cc, preferred_element_type=F32).astype(BF16)
              for xoi, cc in zip(xo, c)]
        xo = [xoi + jnp.dot(xci, xbi, preferred_element_type=F32) for xoi, xci, xbi in zip(xo, xc, xb)]
        x = [jnp.concatenate([xoi[(q // 2) * s:(q // 2 + 1) * s] if q % 2 else xi[q * s:(q + 1) * s]
                              for q in range(n // s)], axis=0) for xi, xoi in zip(x, xo)]
        between()
        s *= 2

    akv = [jnp.dot(a, v.astype(BF16), preferred_element_type=F32) for a, v in zip(a_ak, v_s)]
    tg = [_dot(xi, jnp.concatenate([a, kv], axis=1)) for xi, a, kv in zip(x, a_s, akv)]
    between()
    low = [jnp.concatenate([g, jnp.concatenate([jnp.zeros_like(v), v], axis=1)], axis=0).astype(BF16)
           for g, v in zip(tg, v_s)]
    out1 = [jnp.dot(a, lo, preferred_element_type=F32) for a, lo in zip(a_r, low)]
    out2 = [lax.dot_general(bb, lo, (((0,), (0,)), ((), ())), preferred_element_type=F32)
            for bb, lo in zip(bk, low)]
    between()
    ki = lax.broadcasted_iota(jnp.int32, (LANES, LANES), 0)
    kj = lax.broadcasted_iota(jnp.int32, (LANES, LANES), 1)
    terms = []
    for rs, o1, o2, we in zip(r_s, out1, out2, w_end):
        r_hat = rs + o1[:, :LANES]
        m = o2[:, :LANES] + jnp.where(ki == kj, we, 0.0)
        terms.append((r_hat, o1[:, LANES:], m, o2[:, LANES:]))
    return terms


def _wkv_kernel(r_ref, k_ref, v_ref, ld_ref, kn_ref, b_ref, spread_ref, y_ref, s_ref, mr_ref, nc_ref,
                yv_ref, *, rows, group):
    t = WKV_CHUNK
    npair = RWKV_WIDTH // LANES

    @pl.when(pl.program_id(1) == 0)
    def _():
        s_ref[...] = jnp.zeros_like(s_ref)

    def chain_step(c):
        rs = pl.ds(_aligned(c * STACK, STACK), STACK)
        outs = [jnp.dot(mr_ref[p, pl.ds(_aligned(c * 2 * STACK, 2 * STACK), 2 * STACK), :],
                        s_ref[p].astype(BF16), preferred_element_type=F32) for p in range(npair)]
        for p, out in enumerate(outs):
            s_ref[p] = out[:LANES] + nc_ref[p, rs, :]
            ys = out[LANES:] + yv_ref[p, rs, :]
            y_ref[pl.ds(_aligned(c * t, t), t), p * LANES:(p + 1) * LANES] = ys[:t] + ys[t:]

    def precompute(g, side):
        chains, where = [], []
        for j in range(group):
            c = g * group + j
            sl = pl.ds(_aligned(c * t, t), t)
            tiles = [ref[sl, :] for ref in (r_ref, k_ref, v_ref, ld_ref, kn_ref, b_ref)]
            cl = _cumsum_rows(tiles[3])
            for p in range(npair):
                ls = slice(p * LANES, (p + 1) * LANES)
                chains.append(tuple(a[:, ls] for a in tiles) + (cl[:, ls],))
                where.append((c, p))
        for (c, p), (r_hat, y_v, m, n_c) in zip(where, _wkv_terms(chains, spread_ref[...], side)):
            mr_ref[p, pl.ds(_aligned(c * 2 * STACK, 2 * STACK), 2 * STACK), :] = (
                jnp.concatenate([m, r_hat], axis=0).astype(BF16))
            rs = pl.ds(_aligned(c * STACK, STACK), STACK)
            nc_ref[p, rs, :] = n_c
            yv_ref[p, rs, :] = y_v

    ngroup = rows // (t * group)
    precompute(0, ())

    def body(g, _):
        precompute(g, [functools.partial(chain_step, (g - 1) * group + j) for j in range(group)])
        return 0

    lax.fori_loop(1, ngroup, body, 0)
    for j in range(group):
        chain_step((ngroup - 1) * group + j)


def _wkv(r, k, v, ld, kn, b, *, bsz, seq, rows):
    n = bsz * seq
    nt = seq // rows
    npair = RWKV_WIDTH // LANES
    nchunk = rows // WKV_CHUNK
    spec = pl.BlockSpec((rows, RWKV_WIDTH), lambda bi, ti: (bi * nt + ti, 0))
    return pl.pallas_call(
        functools.partial(_wkv_kernel, rows=rows, group=4),
        grid=(bsz, nt),
        in_specs=[spec] * 6 + [_const_spec((LANES, (SUBLANES - 1) * LANES))],
        out_specs=spec,
        out_shape=jax.ShapeDtypeStruct((n, RWKV_WIDTH), F32),
        scratch_shapes=[pltpu.VMEM((npair, LANES, LANES), F32),
                        pltpu.VMEM((npair, nchunk * 2 * STACK, LANES), BF16),
                        pltpu.VMEM((npair, nchunk * STACK, LANES), F32),
                        pltpu.VMEM((npair, nchunk * STACK, LANES), F32)],
        compiler_params=_params("arbitrary", "arbitrary"),
        name="wkv",
    )(r, k, v, ld, kn, b, _spread_matrix())


def _gelu_tanh(x):
    return 0.5 * x * (1.0 + jnp.tanh(math.sqrt(2.0 / math.pi) * (x + 0.044715 * x * x * x)))


def _s5_kernel(u_ref, wre_ref, wim_ref, cc_ref, ar_ref, ai_ref, d_ref, y_ref,
               xre, xim, st_re, st_im, *, steps):
    @pl.when(pl.program_id(1) == 0)
    def _():
        st_re[...] = jnp.zeros_like(st_re)
        st_im[...] = jnp.zeros_like(st_im)

    rows = steps * SUBLANES
    ns = S5_SLAB_STATE
    half = S5_GROUPS * S5_STATE // 2
    u = [u_ref[:, g * SUBLANES:(g + 1) * SUBLANES, :].reshape(rows, S5_WIDTH) for g in range(S5_TILES)]
    ub = [a.astype(BF16) for a in u]

    def in_proj_jobs(g):
        def job(j, ref, w_ref):
            ref[g, :, j * ns:(j + 1) * ns] = jnp.dot(ub[g][:, j * LANES:(j + 1) * LANES], w_ref[j],
                                                     preferred_element_type=F32)
        return [functools.partial(job, j, ref, w_ref)
                for j in range(S5_SLABS) for ref, w_ref in ((xre, wre_ref), (xim, wim_ref))]

    def out_proj_jobs(g):
        def job(j):
            xs = jnp.concatenate([xre[g, :, j * ns:(j + 1) * ns], xim[g, :, j * ns:(j + 1) * ns]], axis=1)
            ls = slice(j * LANES, (j + 1) * LANES)
            yj = (jnp.dot(xs.astype(BF16), cc_ref[j], preferred_element_type=F32)
                  + d_ref[:, ls] * u[g][:, ls])
            y_ref[:, g * SUBLANES:(g + 1) * SUBLANES, ls] = _gelu_tanh(yj).reshape(steps, SUBLANES, LANES)
        return [functools.partial(job, j) for j in range(S5_SLABS)]

    def scan(g, jobs):
        total = 2 * steps
        done = 0
        for c in range(2):
            cs = slice(c * half, (c + 1) * half)
            ar = jnp.broadcast_to(ar_ref[:, cs], (SUBLANES, half))
            ai = jnp.broadcast_to(ai_ref[:, cs], (SUBLANES, half))
            x_re, x_im = st_re[g, :, cs], st_im[g, :, cs]
            for i in range(steps):
                k = c * steps + i
                for _ in range((k + 1) * len(jobs) // total - done):
                    jobs[done]()
                    done += 1
                rs = slice(i * SUBLANES, (i + 1) * SUBLANES)
                n_re = ar * x_re - ai * x_im + xre[g, rs, cs]
                n_im = ar * x_im + ai * x_re + xim[g, rs, cs]
                xre[g, rs, cs] = n_re
                xim[g, rs, cs] = n_im
                x_re, x_im = n_re, n_im
            st_re[g, :, cs] = x_re
            st_im[g, :, cs] = x_im

    for job in in_proj_jobs(0):
        job()
    scan(0, in_proj_jobs(1))
    scan(1, out_proj_jobs(0))
    for job in out_proj_jobs(1):
        job()


def _s5(u_t, wre, wim, cc, ar, ai, d, *, bsz, seq, steps):
    nstate = S5_GROUPS * S5_STATE
    rows = steps * SUBLANES
    nb = S5_TILES * SUBLANES
    spec = pl.BlockSpec((steps, nb, S5_WIDTH), lambda g, t: (t, g, 0))
    return pl.pallas_call(
        functools.partial(_s5_kernel, steps=steps),
        grid=(bsz // nb, seq // steps),
        in_specs=[spec,
                  _const_spec((S5_SLABS, LANES, S5_SLAB_STATE)),
                  _const_spec((S5_SLABS, LANES, S5_SLAB_STATE)),
                  _const_spec((S5_SLABS, 2 * S5_SLAB_STATE, LANES)),
                  _const_spec((1, nstate)), _const_spec((1, nstate)),
                  _const_spec((1, S5_WIDTH))],
        out_specs=spec,
        out_shape=jax.ShapeDtypeStruct((seq, bsz, S5_WIDTH), F32),
        scratch_shapes=[pltpu.VMEM((S5_TILES, rows, nstate), F32), pltpu.VMEM((S5_TILES, rows, nstate), F32),
                        pltpu.VMEM((S5_TILES, SUBLANES, nstate), F32),
                        pltpu.VMEM((S5_TILES, SUBLANES, nstate), F32)],
        compiler_params=_params("arbitrary", "arbitrary"),
        name="s5",
    )(u_t, wre, wim, cc, ar, ai, d)


def _merge_kernel(y_ref, g_ref, bg_ref, ys5_ref, gates_ref, x_ref, mod_ref, lng_ref, lnb_ref,
                  gmean_ref, wrw_ref, wglu_ref, wout_ref, o_ref):
    gm = gmean_ref[...]
    y = y_ref[...]
    mean = _head_sum(y, gm, split=True)
    d = y - mean
    var = _head_sum(d * d, gm)
    yn = d * lax.rsqrt(var + GN_EPS) * lng_ref[...] + lnb_ref[...]
    ya = _dot(yn * g_ref[...] + bg_ref[...], wrw_ref[...])
    z = _dot(ys5_ref[...], wglu_ref[...])
    yb = z[:, :D_MODEL] * jax.nn.sigmoid(z[:, D_MODEL:])
    gates = gates_ref[...]
    mixed = _dot(gates[:, :D_MODEL] * ya + gates[:, D_MODEL:] * yb, wout_ref[...])
    gt1 = mod_ref[...][2:3]
    o_ref[...] = x_ref[...] + gt1 * mixed


def _merge(y, g, bg, ys5, gates, x2, mod3, lng, lnb, gmean, wrw, wglu, wout, *, bsz, seq, rows):
    n = bsz * seq
    nt = seq // rows
    w = RWKV_WIDTH
    row_spec = lambda cols: pl.BlockSpec((rows, cols), lambda i: (i, 0))
    return pl.pallas_call(
        _merge_kernel,
        grid=(n // rows,),
        in_specs=[row_spec(w), row_spec(w), row_spec(w), row_spec(w), row_spec(2 * D_MODEL),
                  row_spec(D_MODEL),
                  pl.BlockSpec((None, 6, D_MODEL), lambda i: (i // nt, 0, 0)),
                  _const_spec((1, w)), _const_spec((1, w)), _const_spec((MXU_TILE, MXU_TILE)),
                  _const_spec((w, D_MODEL)), _const_spec((S5_WIDTH, 2 * D_MODEL)),
                  _const_spec((D_MODEL, D_MODEL))],
        out_specs=row_spec(D_MODEL),
        out_shape=jax.ShapeDtypeStruct((n, D_MODEL), F32),
        compiler_params=_params("arbitrary"),
        name="merge",
    )(y, g, bg, ys5, gates, x2, mod3, lng, lnb, gmean, wrw, wglu, wout)


def _ffn_kernel(x_ref, mod_ref, g2_ref, wup_ref, cw_ref, cb_ref, wdn_ref, gf_ref, o_ref, buf, *, rows):
    pad = SUBLANES

    @pl.when(pl.program_id(1) == 0)
    def _():
        buf[:, 0:pad, :] = jnp.zeros((FFN_SPLIT, pad, 2 * FFN_CH), F32)

    mod = mod_ref[...]
    sh2, sc2, gt2 = mod[3:4], mod[4:5], mod[5:6]

    def normed(xh):
        ms = jnp.mean(xh * xh, axis=-1, keepdims=True)
        h = xh * lax.rsqrt(ms + RMS_EPS) * g2_ref[...]
        return (h * (1.0 + sc2) + sh2).astype(BF16)

    halves = 2
    hr = rows // halves
    x = [x_ref[h * hr:(h + 1) * hr, :] for h in range(halves)]
    hb = [normed(x[0]), None]
    acc = [jnp.zeros((hr, D_MODEL), F32) for _ in range(halves)]
    for c in range(FFN_SPLIT):
        for h in range(halves):
            if hb[h] is None:
                hb[h] = normed(x[h])
            r0 = pad + h * hr
            buf[c, r0:r0 + hr, :] = jnp.dot(hb[h], wup_ref[c], preferred_element_type=F32)
        cw = cw_ref[c]
        for h in range(halves):
            hid = cb_ref[c]
            for j in range(CONV_W):
                off = pad + h * hr - (CONV_W - 1) + j
                hid = hid + cw[j:j + 1, :] * buf[c, off:off + hr, :]
            gate, up = hid[:, :FFN_CH], hid[:, FFN_CH:]
            act = (0.5 * gate) * (1.0 + jnp.tanh(0.5 * gate)) * up
            acc[h] = acc[h] + jnp.dot(act.astype(BF16), wdn_ref[c], preferred_element_type=F32)
        buf[c, pad - (CONV_W - 1):pad, :] = buf[c, pad + rows - (CONV_W - 1):pad + rows, :]

    for h in range(halves):
        x2 = x[h] + gt2 * acc[h]
        ms2 = jnp.mean(x2 * x2, axis=-1, keepdims=True)
        o_ref[h * hr:(h + 1) * hr, :] = x2 * lax.rsqrt(ms2 + RMS_EPS) * gf_ref[...]


def _ffn(x1, mod3, g2, wup, cw, cb, wdn, gf, *, bsz, seq, rows):
    n = bsz * seq
    nt = seq // rows
    row_spec = pl.BlockSpec((rows, D_MODEL), lambda b, t: (b * nt + t, 0))
    return pl.pallas_call(
        functools.partial(_ffn_kernel, rows=rows),
        grid=(bsz, nt),
        in_specs=[row_spec,
                  pl.BlockSpec((None, 6, D_MODEL), lambda b, t: (b, 0, 0)),
                  _const_spec((1, D_MODEL)),
                  _const_spec((FFN_SPLIT, D_MODEL, 2 * FFN_CH)),
                  _const_spec((FFN_SPLIT, CONV_W, 2 * FFN_CH)),
                  _const_spec((FFN_SPLIT, 1, 2 * FFN_CH)),
                  _const_spec((FFN_SPLIT, FFN_CH, D_MODEL)),
                  _const_spec((1, D_MODEL))],
        out_specs=row_spec,
        out_shape=jax.ShapeDtypeStruct((n, D_MODEL), F32),
        scratch_shapes=[pltpu.VMEM((FFN_SPLIT, rows + SUBLANES, 2 * FFN_CH), F32)],
        compiler_params=_params("arbitrary", "arbitrary"),
        name="ffn",
    )(x1, mod3, g2, wup, cw, cb, wdn, gf)


def _split_cols(a):
    parts = [jnp.concatenate([a[..., c * FFN_CH:(c + 1) * FFN_CH],
                              a[..., D_FF + c * FFN_CH:D_FF + (c + 1) * FFN_CH]], axis=-1)
             for c in range(FFN_SPLIT)]
    return jnp.stack(parts, axis=0)


def _block_diag_ones(n, blk, value, dtype):
    i = jnp.arange(n) // blk
    return jnp.where(i[:, None] == i[None, :], value, 0.0).astype(dtype)


def kernel(x, c, w_ada, b_ada, norm1_g, w_in, mu_shift, rwkv_w0, rwkv_w_up, rwkv_a0, rwkv_a_up,
           rwkv_g_up, rwkv_k_k, rwkv_k_a, rwkv_r_k, rwkv_ln_g, rwkv_ln_b, w_out_rwkv, s5_a_re,
           s5_a_im, s5_log_dt, s5_b_re, s5_b_im, s5_c_re, s5_c_im, s5_d, w_glu, w_out, norm2_g,
           w_ffn_up, ffn_conv_w, ffn_conv_b, w_ffn_down, norm_f_g):
    bsz, seq, _ = x.shape
    depth = w_ada.shape[0]
    n = bsz * seq
    w = RWKV_WIDTH
    rows_a = min(256, seq)
    rows_m = min(512, seq)
    rows_f = min(512, seq)
    s5_steps = min(64, seq)

    g64 = _block_diag_ones(MXU_TILE, RWKV_HEAD, 1.0, BF16)
    gmean = _block_diag_ones(MXU_TILE, RWKV_HEAD, 1.0 / RWKV_HEAD, BF16)
    eye8 = jnp.eye(LANES // S5_GROUP, dtype=F32)
    row = lambda a: a.reshape(1, -1)

    assert depth == 1, depth
    x2 = x.reshape(n, D_MODEL)
    for l in range(depth):
        mod3 = _adaln(c, w_ada[l], b_ada[l]).reshape(bsz, 6, D_MODEL)

        zero = jnp.zeros((LORA_W, w), F32)
        waup = jnp.concatenate([jnp.concatenate([rwkv_w_up[l], zero], axis=1),
                                jnp.concatenate([zero, rwkv_a_up[l]], axis=1)], axis=0).astype(BF16)
        r, k, v, ld, kn, b, g, bg, u, gates = _inproj(
            x2, mod3, row(norm1_g[l]), w_in[l].astype(BF16), row(mu_shift[l]), row(rwkv_w0[l]), waup,
            row(rwkv_a0[l]), rwkv_g_up[l].astype(BF16), row(rwkv_k_k[l]), row(rwkv_k_a[l]),
            row(rwkv_r_k[l]), g64, bsz=bsz, seq=seq, rows=rows_a)

        y = _wkv(r, k, v, ld, kn, b, bsz=bsz, seq=seq, rows=min(1024, seq))

        ab_re, ab_im, bb_re, bb_im = _s5prep(s5_a_re[l], s5_a_im[l], s5_log_dt[l], s5_b_re[l], s5_b_im[l])
        gl = LANES // S5_GROUP
        slab_in = lambda bb: jnp.einsum('jgcp,gh->jgchp', bb.reshape(S5_SLABS, gl, S5_GROUP, S5_STATE),
                                        eye8).reshape(S5_SLABS, LANES, S5_SLAB_STATE).astype(BF16)
        slab_out = lambda cc: jnp.einsum('jgcp,gh->jgphc', cc.reshape(S5_SLABS, gl, S5_GROUP, S5_STATE),
                                         eye8).reshape(S5_SLABS, S5_SLAB_STATE, LANES)
        cc = jnp.concatenate([slab_out(s5_c_re[l]), -slab_out(s5_c_im[l])], axis=1).astype(BF16)
        u_t = jnp.transpose(u.reshape(bsz, seq, S5_WIDTH), (1, 0, 2))
        ys5_t = _s5(u_t, slab_in(bb_re), slab_in(bb_im), cc, row(ab_re), row(ab_im), row(s5_d[l]),
                    bsz=bsz, seq=seq, steps=s5_steps)
        ys5 = jnp.transpose(ys5_t, (1, 0, 2)).reshape(n, S5_WIDTH)

        x1 = _merge(y, g, bg, ys5, gates, x2, mod3, row(rwkv_ln_g[l]), row(rwkv_ln_b[l]), gmean,
                    w_out_rwkv[l].astype(BF16), w_glu[l].astype(BF16), w_out[l].astype(BF16),
                    bsz=bsz, seq=seq, rows=rows_m)

        gf = row(norm_f_g)
        wdn = w_ffn_down[l].astype(BF16).reshape(FFN_SPLIT, FFN_CH, D_MODEL)
        x2 = _ffn(x1, mod3, row(norm2_g[l]), _split_cols(w_ffn_up[l]).astype(BF16),
                  _split_cols(ffn_conv_w[l]), _split_cols(ffn_conv_b[l].reshape(1, -1)), wdn,
                  gf, bsz=bsz, seq=seq, rows=rows_f)
    return x2.reshape(bsz, seq, D_MODEL)
```

```python
import functools
import math

import jax
import jax.numpy as jnp
from jax import lax
from jax.experimental import pallas as pl
from jax.experimental.pallas import tpu as pltpu

F32 = jnp.float32
BF16 = jnp.bfloat16
HIGHEST = lax.Precision.HIGHEST

D_MODEL = 1024
RWKV_WIDTH = 512
RWKV_HEAD = 64
LORA_W = 64
LORA_A = 64
LORA_G = 128
S5_WIDTH = 512
S5_GROUP = 16
S5_GROUPS = 32
S5_STATE = 64
D_FF = 2816
CONV_W = 3
RMS_EPS = 1e-6
GN_EPS = 64e-5
L2_EPS = 1e-12
SHIFT_COLS = 3 * RWKV_WIDTH + LORA_W + LORA_A + LORA_G
IN_COLS = SHIFT_COLS + S5_WIDTH + 2 * D_MODEL

LANES = 128
SUBLANES = 8
MXU_TILE = 256
WKV_CHUNK = 64
HEADS_PER_TILE = LANES // RWKV_HEAD
STACK = HEADS_PER_TILE * WKV_CHUNK
S5_SLABS = S5_WIDTH // LANES
S5_SLAB_STATE = (LANES // S5_GROUP) * S5_STATE
S5_TILES = 2
FFN_SPLIT = 2
FFN_CH = D_FF // FFN_SPLIT
VMEM_LIMIT = 56 * 1024 * 1024


def _dot(a, b):
    return jnp.dot(a.astype(BF16), b.astype(BF16), preferred_element_type=F32)


def _dot_split(a, g):
    hi = a.astype(BF16)
    lo = (a - hi.astype(F32)).astype(BF16)
    return (jnp.dot(hi, g, preferred_element_type=F32)
            + jnp.dot(lo, g, preferred_element_type=F32))


def _head_sum(a, g, split=False):
    f = _dot_split if split else _dot
    return jnp.concatenate([f(a[:, i:i + MXU_TILE], g) for i in range(0, a.shape[1], MXU_TILE)], axis=1)


def _aligned(i, m):
    return i if isinstance(i, int) else pl.multiple_of(i, m)


def _const_spec(shape):
    nd = len(shape)
    return pl.BlockSpec(shape, lambda *_: (0,) * nd, pipeline_mode=pl.Buffered(1))


def _params(*sem):
    return pltpu.CompilerParams(dimension_semantics=sem, vmem_limit_bytes=VMEM_LIMIT)


def _adaln_kernel(c_ref, w_ref, b_ref, o_ref):
    c = c_ref[...]
    s = c * jax.nn.sigmoid(c)
    o_ref[...] = jnp.dot(s, w_ref[...], preferred_element_type=F32, precision=HIGHEST) + b_ref[...]


def _adaln(c, w_ada, b_ada):
    bsz = c.shape[0]
    n = w_ada.shape[1]
    bn = D_MODEL
    return pl.pallas_call(
        _adaln_kernel,
        grid=(n // bn,),
        in_specs=[pl.BlockSpec((bsz, D_MODEL), lambda j: (0, 0)),
                  pl.BlockSpec((D_MODEL, bn), lambda j: (0, j)),
                  pl.BlockSpec((1, bn), lambda j: (0, j))],
        out_specs=pl.BlockSpec((bsz, bn), lambda j: (0, j)),
        out_shape=jax.ShapeDtypeStruct((bsz, n), F32),
        compiler_params=_params("arbitrary"),
        name="adaln",
    )(c, w_ada, b_ada.reshape(1, n))


def _s5prep_kernel(are_ref, aim_ref, ldt_ref, bre_ref, bim_ref, abre_o, abim_o, bbre_o, bbim_o):
    a_re = are_ref[...]
    a_im = aim_ref[...]
    dt = jnp.exp(ldt_ref[...])
    z_re, z_im = a_re * dt, a_im * dt
    mag = jnp.exp(z_re)
    ab_re, ab_im = mag * jnp.cos(z_im), mag * jnp.sin(z_im)
    den = a_re * a_re + a_im * a_im
    q_re = ((ab_re - 1.0) * a_re + ab_im * a_im) / den
    q_im = (ab_im * a_re - (ab_re - 1.0) * a_im) / den
    abre_o[...] = ab_re
    abim_o[...] = ab_im
    b_re = bre_ref[...]
    b_im = bim_ref[...]
    bbre_o[...] = q_re[:, None, :] * b_re - q_im[:, None, :] * b_im
    bbim_o[...] = q_re[:, None, :] * b_im + q_im[:, None, :] * b_re


def _s5prep(a_re, a_im, log_dt, b_re, b_im):
    g, p = a_re.shape
    c = b_re.shape[-1]
    b_re_t = jnp.transpose(b_re, (0, 2, 1))
    b_im_t = jnp.transpose(b_im, (0, 2, 1))
    return pl.pallas_call(
        _s5prep_kernel,
        out_shape=(jax.ShapeDtypeStruct((g, p), F32), jax.ShapeDtypeStruct((g, p), F32),
                   jax.ShapeDtypeStruct((g, c, p), F32), jax.ShapeDtypeStruct((g, c, p), F32)),
        name="s5prep",
    )(a_re, a_im, log_dt.reshape(g, 1), b_re_t, b_im_t)


def _inproj_kernel(x_ref, mod_ref, g1_ref, win_ref, mu_ref, w0_ref, waup_ref, a0_ref, gup_ref,
                   kk_ref, ka_ref, rk_ref, g64_ref,
                   r_o, k_o, v_o, ld_o, kn_o, b_o, g_o, bg_o, u_o, gates_o, carry, *, rows):
    @pl.when(pl.program_id(1) == 0)
    def _():
        carry[...] = jnp.zeros_like(carry)

    x = x_ref[...]
    mod = mod_ref[...]
    sh1, sc1 = mod[0:1], mod[1:2]
    ms = jnp.mean(x * x, axis=-1, keepdims=True)
    h = x * lax.rsqrt(ms + RMS_EPS) * g1_ref[...]
    h = h * (1.0 + sc1) + sh1
    proj = jnp.dot(h.astype(BF16), win_ref[...], preferred_element_type=F32)

    p = proj[:, :SHIFT_COLS]
    row = lax.broadcasted_iota(jnp.int32, p.shape, 0)
    prev = jnp.where(row == 0, carry[...], pltpu.roll(p, 1, axis=0))
    carry[...] = p[rows - 1:rows, :]
    ps = p + (prev - p) * mu_ref[...]

    w = RWKV_WIDTH
    r, k, v = ps[:, 0:w], ps[:, w:2 * w], ps[:, 2 * w:3 * w]
    wa = ps[:, 3 * w:3 * w + LORA_W + LORA_A]
    lane = lax.broadcasted_iota(jnp.int32, wa.shape, 1)
    wa = jnp.where(lane < LORA_W, jnp.tanh(wa), wa)
    delta = _dot(wa, waup_ref[...])
    w_raw = w0_ref[...] + delta[:, :w]
    ld = -math.exp(-0.5) * jax.nn.sigmoid(w_raw)
    eta = jax.nn.sigmoid(a0_ref[...] + delta[:, w:])
    gd = ps[:, 3 * w + LORA_W + LORA_A:SHIFT_COLS]
    g = _dot(jax.nn.sigmoid(gd), gup_ref[...])

    g64 = g64_ref[...]
    kk = k * kk_ref[...]
    kn = kk * lax.rsqrt(_head_sum(kk * kk, g64) + L2_EPS)
    k2 = k * (1.0 + (eta - 1.0) * ka_ref[...])
    bonus = _head_sum(r * k2 * rk_ref[...], g64) * v

    r_o[...] = r
    k_o[...] = k2
    v_o[...] = v
    ld_o[...] = ld
    kn_o[...] = kn
    b_o[...] = kn * eta
    g_o[...] = g
    bg_o[...] = bonus * g
    u_o[...] = proj[:, SHIFT_COLS:SHIFT_COLS + S5_WIDTH]
    gates_o[...] = jax.nn.sigmoid(proj[:, SHIFT_COLS + S5_WIDTH:])


def _inproj(x2, mod3, g1, win, mu, w0, waup, a0, gup, k_k, k_a, r_k, g64, *, bsz, seq, rows):
    n = bsz * seq
    nt = seq // rows
    w = RWKV_WIDTH
    row_spec = lambda cols: pl.BlockSpec((rows, cols), lambda b, t: (b * nt + t, 0))
    outs = [jax.ShapeDtypeStruct((n, w), F32)] * 9 + [jax.ShapeDtypeStruct((n, 2 * D_MODEL), F32)]
    return pl.pallas_call(
        functools.partial(_inproj_kernel, rows=rows),
        grid=(bsz, nt),
        in_specs=[row_spec(D_MODEL),
                  pl.BlockSpec((None, 6, D_MODEL), lambda b, t: (b, 0, 0)),
                  _const_spec((1, D_MODEL)),
                  _const_spec((D_MODEL, IN_COLS)),
                  _const_spec((1, SHIFT_COLS)),
                  _const_spec((1, w)),
                  _const_spec((LORA_W + LORA_A, 2 * w)),
                  _const_spec((1, w)),
                  _const_spec((LORA_G, w)),
                  _const_spec((1, w)), _const_spec((1, w)), _const_spec((1, w)),
                  _const_spec((MXU_TILE, MXU_TILE))],
        out_specs=[row_spec(w)] * 9 + [row_spec(2 * D_MODEL)],
        out_shape=outs,
        scratch_shapes=[pltpu.VMEM((1, SHIFT_COLS), F32)],
        compiler_params=_params("arbitrary", "arbitrary"),
        name="inproj",
    )(x2, mod3, g1, win, mu, w0, waup, a0, gup, k_k, k_a, r_k, g64)


def _stack_heads(z):
    lane = lax.broadcasted_iota(jnp.int32, z.shape, 1)
    first = lane < RWKV_HEAD
    return jnp.concatenate([jnp.where(first, z, 0.0), jnp.where(first, 0.0, z)], axis=0)


def _cumsum_rows(x):
    n = x.shape[0]
    row = lax.broadcasted_iota(jnp.int32, x.shape, 0)
    d = 1
    while d < n:
        if d < SUBLANES:
            shifted = jnp.where(row >= d, pltpu.roll(x, d, axis=0), 0.0)
        else:
            shifted = jnp.concatenate([jnp.zeros((d, x.shape[1]), x.dtype), x[:n - d]], axis=0)
        x = x + shifted
        d *= 2
    return x


WKV_GROUP = 4
WKV_PAIRS = RWKV_WIDTH // LANES
WKV_CHAINS = WKV_GROUP * WKV_PAIRS
_WKV_STAGE_BOUNDARIES = 7


def _spread_matrix():
    src = jnp.arange(LANES)
    dst = jnp.arange((SUBLANES - 1) * LANES)
    j, lane = dst // LANES, dst % LANES
    hit = (src[:, None] // SUBLANES == lane[None, :] // SUBLANES) & (src[:, None] % SUBLANES == j[None, :])
    return hit.astype(BF16)


def _wkv_prep_chunk(tiles, slot, j, ops):
    lhs_ref, rhs_ref, bk_ref, v_ref, r_ref, w_ref = ops
    t = WKV_CHUNK
    cl_all = _cumsum_rows(tiles[3])
    for p in range(WKV_PAIRS):
        ls = slice(p * LANES, (p + 1) * LANES)
        r, k, v, ld, kn, b = (a[:, ls] for a in tiles)
        cl = cl_all[:, ls]
        cl_end = cl[t - 1:t, :]
        e_incl = jnp.exp(cl)
        e_excl = jnp.exp(cl - ld)
        e_neg = jnp.exp(-cl)
        e_end = jnp.exp(cl_end - cl)
        i = j * WKV_PAIRS + p
        r_s = _stack_heads(r * e_incl)
        lhs_ref[slot, i] = jnp.concatenate([_stack_heads(-kn * e_excl), r_s], axis=0).astype(BF16)
        rhs_ref[slot, i] = jnp.concatenate([b * e_neg, k * e_neg], axis=0).astype(BF16)
        bk_ref[slot, i] = jnp.concatenate([_stack_heads(b * e_end), _stack_heads(k * e_end)],
                                          axis=0).astype(BF16)
        v_ref[slot, i] = _stack_heads(v).astype(BF16)
        r_ref[slot, i] = r_s
        w_ref[slot, i] = jnp.broadcast_to(jnp.exp(cl_end), (SUBLANES, LANES))


def _wkv_solve(slot, ops, spread, side):
    lhs_ref, rhs_ref, bk_ref, v_ref, r_ref, w_ref = ops
    t = WKV_CHUNK
    n = STACK
    nch = WKV_CHAINS
    jobs = list(side)
    calls = [0]

    def between():
        k = calls[0]
        calls[0] += 1
        for _ in range((k + 1) * len(side) // _WKV_STAGE_BOUNDARIES - k * len(side) // _WKV_STAGE_BOUNDARIES):
            jobs.pop(0)()

    ri = lax.broadcasted_iota(jnp.int32, (n, n), 0)
    ci = lax.broadcasted_iota(jnp.int32, (n, n), 1)
    same_head = (ri // t) == (ci // t)
    strict = same_head & (ri > ci)
    incl = same_head & (ri >= ci)

    amat = []
    for i in range(nch):
        bd, kd = rhs_ref[slot, i, 0:t], rhs_ref[slot, i, t:2 * t]
        rhs = jnp.concatenate([bd, bd, kd, kd], axis=0)
        amat.append(lax.dot_general(lhs_ref[slot, i], rhs, (((1,), (1,)), ((), ())),
                                    preferred_element_type=F32))
    between()
    a_ab = [jnp.where(strict, a[:n, :n], 0.0) for a in amat]
    a_ak = [jnp.where(strict, a[:n, n:], 0.0).astype(BF16) for a in amat]
    a_r = [jnp.concatenate([jnp.where(incl, a[n:, :n], 0.0), jnp.where(incl, a[n:, n:], 0.0)],
                           axis=1).astype(BF16) for a in amat]

    base = SUBLANES
    blk = (ri // base) == (ci // base)
    packed = []
    for a in a_ab:
        d = jnp.where(blk, a, 0.0)
        acc = d[0:base]
        for q in range(1, n // base):
            acc = acc + d[q * base:(q + 1) * base]
        packed.append(acc)
    cols = jnp.dot(jnp.concatenate(packed, axis=0).astype(BF16), spread,
                   preferred_element_type=F32)
    sub = lax.broadcasted_iota(jnp.int32, (base, LANES), 0)
    lane = lax.broadcasted_iota(jnp.int32, (base, LANES), 1)
    eye_packed = jnp.where(sub == lane % base, 1.0, 0.0)
    x = []
    for i in range(len(a_ab)):
        xp = eye_packed
        for j in range(base - 1):
            lj = cols[i * base:(i + 1) * base, j * LANES:(j + 1) * LANES]
            xp = xp + lj * jnp.broadcast_to(xp[j:j + 1, :], (base, LANES))
        x.append(jnp.where(blk, jnp.concatenate([xp] * (n // base), axis=0), 0.0))
    between()
    s = base
    while s < t:
        level = ((ri // (2 * s)) == (ci // (2 * s))) & ((ri // s) % 2 == 1) & ((ci // s) % 2 == 0)
        c = [jnp.where(level, a, 0.0).astype(BF16) for a in a_ab]
        xb = [xi.astype(BF16) for xi in x]
        odd = lambda m: jnp.concatenate([m[q * s:(q + 1) * s] for q in range(1, n // s, 2)], axis=0)
        xo = [odd(xi) for xi in x]
        xc = [jnp.dot(xoi.astype(BF16), cc, preferred_element_type=F32).astype(BF16)
              for xoi, cc in zip(xo, c)]
        xo = [xoi + jnp.dot(xci, xbi, preferred_element_type=F32) for xoi, xci, xbi in zip(xo, xc, xb)]
        x = [jnp.concatenate([xoi[(q // 2) * s:(q // 2 + 1) * s] if q % 2 else xi[q * s:(q + 1) * s]
                              for q in range(n // s)], axis=0) for xi, xoi in zip(x, xo)]
        between()
        s *= 2

    v_s = [v_ref[slot, i] for i in range(nch)]
    akv = [jnp.dot(a, v, preferred_element_type=F32) for a, v in zip(a_ak, v_s)]
    tg = [jnp.dot(xi.astype(BF16),
                  jnp.concatenate([lhs_ref[slot, i, 0:n], kv.astype(BF16)], axis=1),
                  preferred_element_type=F32) for i, (xi, kv) in enumerate(zip(x, akv))]
    between()
    low = [jnp.concatenate([g.astype(BF16), jnp.concatenate([jnp.zeros_like(v), v], axis=1)], axis=0)
           for g, v in zip(tg, v_s)]
    out1 = [jnp.dot(a, lo, preferred_element_type=F32) for a, lo in zip(a_r, low)]
    out2 = [lax.dot_general(bk_ref[slot, i], lo, (((0,), (0,)), ((), ())), preferred_element_type=F32)
            for i, lo in enumerate(low)]
    between()
    ki = lax.broadcasted_iota(jnp.int32, (LANES, LANES), 0)
    kj = lax.broadcasted_iota(jnp.int32, (LANES, LANES), 1)
    terms = []
    for i, (o1, o2) in enumerate(zip(out1, out2)):
        r_hat = r_ref[slot, i] + o1[:, :LANES]
        m = o2[:, :LANES] + jnp.where(ki == kj, w_ref[slot, i, 0:1], 0.0)
        terms.append((r_hat, o1[:, LANES:], m, o2[:, LANES:]))
    return terms


def _wkv_kernel(r_ref, k_ref, v_ref, ld_ref, kn_ref, b_ref, spread_ref, y_ref, s_ref, mr_ref, nc_ref,
                yv_ref, *ops, rows):
    t = WKV_CHUNK
    group = WKV_GROUP
    ngroup = rows // (t * group)

    @pl.when(pl.program_id(1) == 0)
    def _():
        s_ref[...] = jnp.zeros_like(s_ref)

    def prep_chunk(g, j):
        sl = slice((g * group + j) * t, (g * group + j + 1) * t)
        tiles = [ref[sl, :] for ref in (r_ref, k_ref, v_ref, ld_ref, kn_ref, b_ref)]
        _wkv_prep_chunk(tiles, g % 2, j, ops)

    def chain_step(g, j):
        slot = g % 2
        c = g * group + j
        rs = slice(j * STACK, (j + 1) * STACK)
        outs = [jnp.dot(mr_ref[slot, p, 2 * j * STACK:2 * (j + 1) * STACK, :], s_ref[p].astype(BF16),
                        preferred_element_type=F32) for p in range(WKV_PAIRS)]
        for p, out in enumerate(outs):
            s_ref[p] = out[:LANES] + nc_ref[slot, p, rs, :]
            ys = out[LANES:] + yv_ref[slot, p, rs, :]
            y_ref[c * t:(c + 1) * t, p * LANES:(p + 1) * LANES] = ys[:t] + ys[t:]

    def solve(g):
        slot = g % 2
        side = []
        for j in range(group):
            if g + 1 < ngroup:
                side.append(functools.partial(prep_chunk, g + 1, j))
            if g >= 1:
                side.append(functools.partial(chain_step, g - 1, j))
        terms = _wkv_solve(slot, ops, spread_ref[...], side)
        for i, (r_hat, y_v, m, n_c) in enumerate(terms):
            j, p = divmod(i, WKV_PAIRS)
            mr_ref[slot, p, 2 * j * STACK:2 * (j + 1) * STACK, :] = (
                jnp.concatenate([m, r_hat], axis=0).astype(BF16))
            nc_ref[slot, p, j * STACK:(j + 1) * STACK, :] = n_c
            yv_ref[slot, p, j * STACK:(j + 1) * STACK, :] = y_v

    for j in range(group):
        prep_chunk(0, j)
    for g in range(ngroup):
        solve(g)
    for j in range(group):
        chain_step(ngroup - 1, j)


def _wkv(r, k, v, ld, kn, b, *, bsz, seq, rows):
    n = bsz * seq
    nt = seq // rows
    spec = pl.BlockSpec((rows, RWKV_WIDTH), lambda bi, ti: (bi * nt + ti, 0))
    nch = WKV_CHAINS
    return pl.pallas_call(
        functools.partial(_wkv_kernel, rows=rows),
        grid=(bsz, nt),
        in_specs=[spec] * 6 + [_const_spec((LANES, (SUBLANES - 1) * LANES))],
        out_specs=spec,
        out_shape=jax.ShapeDtypeStruct((n, RWKV_WIDTH), F32),
        scratch_shapes=[pltpu.VMEM((WKV_PAIRS, LANES, LANES), F32),
                        pltpu.VMEM((2, WKV_PAIRS, WKV_GROUP * 2 * STACK, LANES), BF16),
                        pltpu.VMEM((2, WKV_PAIRS, WKV_GROUP * STACK, LANES), F32),
                        pltpu.VMEM((2, WKV_PAIRS, WKV_GROUP * STACK, LANES), F32),
                        pltpu.VMEM((2, nch, 2 * STACK, LANES), BF16),
                        pltpu.VMEM((2, nch, 2 * WKV_CHUNK, LANES), BF16),
                        pltpu.VMEM((2, nch, 2 * STACK, LANES), BF16),
                        pltpu.VMEM((2, nch, STACK, LANES), BF16),
                        pltpu.VMEM((2, nch, STACK, LANES), F32),
                        pltpu.VMEM((2, nch, SUBLANES, LANES), F32)],
        compiler_params=_params("arbitrary", "arbitrary"),
        name="wkv",
    )(r, k, v, ld, kn, b, _spread_matrix())


def _gelu_tanh(x):
    return 0.5 * x * (1.0 + jnp.tanh(math.sqrt(2.0 / math.pi) * (x + 0.044715 * x * x * x)))


def _s5_kernel(u_ref, wre_ref, wim_ref, cc_ref, ar_ref, ai_ref, d_ref, y_ref,
               xre, xim, st_re, st_im, *, steps):
    @pl.when(pl.program_id(1) == 0)
    def _():
        st_re[...] = jnp.zeros_like(st_re)
        st_im[...] = jnp.zeros_like(st_im)

    rows = steps * SUBLANES
    ns = S5_SLAB_STATE
    half = S5_GROUPS * S5_STATE // 2
    u = [u_ref[:, g * SUBLANES:(g + 1) * SUBLANES, :].reshape(rows, S5_WIDTH) for g in range(S5_TILES)]
    ub = [a.astype(BF16) for a in u]

    def in_proj_jobs(g):
        def job(j, ref, w_ref):
            ref[g, :, j * ns:(j + 1) * ns] = jnp.dot(ub[g][:, j * LANES:(j + 1) * LANES], w_ref[j],
                                                     preferred_element_type=F32)
        return [functools.partial(job, j, ref, w_ref)
                for j in range(S5_SLABS) for ref, w_ref in ((xre, wre_ref), (xim, wim_ref))]

    def out_proj_jobs(g):
        def job(j):
            xs = jnp.concatenate([xre[g, :, j * ns:(j + 1) * ns], xim[g, :, j * ns:(j + 1) * ns]], axis=1)
            ls = slice(j * LANES, (j + 1) * LANES)
            yj = (jnp.dot(xs.astype(BF16), cc_ref[j], preferred_element_type=F32)
                  + d_ref[:, ls] * u[g][:, ls])
            y_ref[:, g * SUBLANES:(g + 1) * SUBLANES, ls] = _gelu_tanh(yj).reshape(steps, SUBLANES, LANES)
        return [functools.partial(job, j) for j in range(S5_SLABS)]

    def scan(g, jobs):
        total = 2 * steps
        done = 0
        for c in range(2):
            cs = slice(c * half, (c + 1) * half)
            ar = jnp.broadcast_to(ar_ref[:, cs], (SUBLANES, half))
            ai = jnp.broadcast_to(ai_ref[:, cs], (SUBLANES, half))
            x_re, x_im = st_re[g, :, cs], st_im[g, :, cs]
            for i in range(steps):
                k = c * steps + i
                for _ in range((k + 1) * len(jobs) // total - done):
                    jobs[done]()
                    done += 1
                rs = slice(i * SUBLANES, (i + 1) * SUBLANES)
                n_re = ar * x_re - ai * x_im + xre[g, rs, cs]
                n_im = ar * x_im + ai * x_re + xim[g, rs, cs]
                xre[g, rs, cs] = n_re
                xim[g, rs, cs] = n_im
                x_re, x_im = n_re, n_im
            st_re[g, :, cs] = x_re
            st_im[g, :, cs] = x_im

    for job in in_proj_jobs(0):
        job()
    scan(0, in_proj_jobs(1))
    scan(1, out_proj_jobs(0))
    for job in out_proj_jobs(1):
        job()


def _s5(u_t, wre, wim, cc, ar, ai, d, *, bsz, seq, steps):
    nstate = S5_GROUPS * S5_STATE
    rows = steps * SUBLANES
    nb = S5_TILES * SUBLANES
    spec = pl.BlockSpec((steps, nb, S5_WIDTH), lambda g, t: (t, g, 0))
    return pl.pallas_call(
        functools.partial(_s5_kernel, steps=steps),
        grid=(bsz // nb, seq // steps),
        in_specs=[spec,
                  _const_spec((S5_SLABS, LANES, S5_SLAB_STATE)),
                  _const_spec((S5_SLABS, LANES, S5_SLAB_STATE)),
                  _const_spec((S5_SLABS, 2 * S5_SLAB_STATE, LANES)),
                  _const_spec((1, nstate)), _const_spec((1, nstate)),
                  _const_spec((1, S5_WIDTH))],
        out_specs=spec,
        out_shape=jax.ShapeDtypeStruct((seq, bsz, S5_WIDTH), F32),
        scratch_shapes=[pltpu.VMEM((S5_TILES, rows, nstate), F32), pltpu.VMEM((S5_TILES, rows, nstate), F32),
                        pltpu.VMEM((S5_TILES, SUBLANES, nstate), F32),
                        pltpu.VMEM((S5_TILES, SUBLANES, nstate), F32)],
        compiler_params=_params("arbitrary", "arbitrary"),
        name="s5",
    )(u_t, wre, wim, cc, ar, ai, d)


def _merge_kernel(y_ref, g_ref, bg_ref, ys5_ref, gates_ref, x_ref, mod_ref, lng_ref, lnb_ref,
                  gmean_ref, wrw_ref, wglu_ref, wout_ref, o_ref):
    gm = gmean_ref[...]
    y = y_ref[...]
    mean = _head_sum(y, gm, split=True)
    d = y - mean
    var = _head_sum(d * d, gm)
    yn = d * lax.rsqrt(var + GN_EPS) * lng_ref[...] + lnb_ref[...]
    ya = _dot(yn * g_ref[...] + bg_ref[...], wrw_ref[...])
    z = _dot(ys5_ref[...], wglu_ref[...])
    yb = z[:, :D_MODEL] * jax.nn.sigmoid(z[:, D_MODEL:])
    gates = gates_ref[...]
    mixed = _dot(gates[:, :D_MODEL] * ya + gates[:, D_MODEL:] * yb, wout_ref[...])
    gt1 = mod_ref[...][2:3]
    o_ref[...] = x_ref[...] + gt1 * mixed


def _merge(y, g, bg, ys5, gates, x2, mod3, lng, lnb, gmean, wrw, wglu, wout, *, bsz, seq, rows):
    n = bsz * seq
    nt = seq // rows
    w = RWKV_WIDTH
    row_spec = lambda cols: pl.BlockSpec((rows, cols), lambda i: (i, 0))
    return pl.pallas_call(
        _merge_kernel,
        grid=(n // rows,),
        in_specs=[row_spec(w), row_spec(w), row_spec(w), row_spec(w), row_spec(2 * D_MODEL),
                  row_spec(D_MODEL),
                  pl.BlockSpec((None, 6, D_MODEL), lambda i: (i // nt, 0, 0)),
                  _const_spec((1, w)), _const_spec((1, w)), _const_spec((MXU_TILE, MXU_TILE)),
                  _const_spec((w, D_MODEL)), _const_spec((S5_WIDTH, 2 * D_MODEL)),
                  _const_spec((D_MODEL, D_MODEL))],
        out_specs=row_spec(D_MODEL),
        out_shape=jax.ShapeDtypeStruct((n, D_MODEL), F32),
        compiler_params=_params("arbitrary"),
        name="merge",
    )(y, g, bg, ys5, gates, x2, mod3, lng, lnb, gmean, wrw, wglu, wout)


def _ffn_kernel(x_ref, mod_ref, g2_ref, wup_ref, cw_ref, cb_ref, wdn_ref, gf_ref, o_ref, buf, *, rows):
    pad = SUBLANES

    @pl.when(pl.program_id(1) == 0)
    def _():
        buf[:, 0:pad, :] = jnp.zeros((FFN_SPLIT, pad, 2 * FFN_CH), F32)

    mod = mod_ref[...]
    sh2, sc2, gt2 = mod[3:4], mod[4:5], mod[5:6]

    def normed(xh):
        ms = jnp.mean(xh * xh, axis=-1, keepdims=True)
        h = xh * lax.rsqrt(ms + RMS_EPS) * g2_ref[...]
        return (h * (1.0 + sc2) + sh2).astype(BF16)

    halves = 2
    hr = rows // halves
    x = [x_ref[h * hr:(h + 1) * hr, :] for h in range(halves)]
    hb = [normed(x[0]), None]
    acc = [jnp.zeros((hr, D_MODEL), F32) for _ in range(halves)]
    for c in range(FFN_SPLIT):
        for h in range(halves):
            if hb[h] is None:
                hb[h] = normed(x[h])
            r0 = pad + h * hr
            buf[c, r0:r0 + hr, :] = jnp.dot(hb[h], wup_ref[c], preferred_element_type=F32)
        cw = cw_ref[c]
        for h in range(halves):
            hid = cb_ref[c]
            for j in range(CONV_W):
                off = pad + h * hr - (CONV_W - 1) + j
                hid = hid + cw[j:j + 1, :] * buf[c, off:off + hr, :]
            gate, up = hid[:, :FFN_CH], hid[:, FFN_CH:]
            act = (0.5 * gate) * (1.0 + jnp.tanh(0.5 * gate)) * up
            acc[h] = acc[h] + jnp.dot(act.astype(BF16), wdn_ref[c], preferred_element_type=F32)
        buf[c, pad - (CONV_W - 1):pad, :] = buf[c, pad + rows - (CONV_W - 1):pad + rows, :]

    for h in range(halves):
        x2 = x[h] + gt2 * acc[h]
        ms2 = jnp.mean(x2 * x2, axis=-1, keepdims=True)
        o_ref[h * hr:(h + 1) * hr, :] = x2 * lax.rsqrt(ms2 + RMS_EPS) * gf_ref[...]


def _ffn(x1, mod3, g2, wup, cw, cb, wdn, gf, *, bsz, seq, rows):
    n = bsz * seq
    nt = seq // rows
    row_spec = pl.BlockSpec((rows, D_MODEL), lambda b, t: (b * nt + t, 0))
    return pl.pallas_call(
        functools.partial(_ffn_kernel, rows=rows),
        grid=(bsz, nt),
        in_specs=[row_spec,
                  pl.BlockSpec((None, 6, D_MODEL), lambda b, t: (b, 0, 0)),
                  _const_spec((1, D_MODEL)),
                  _const_spec((FFN_SPLIT, D_MODEL, 2 * FFN_CH)),
                  _const_spec((FFN_SPLIT, CONV_W, 2 * FFN_CH)),
                  _const_spec((FFN_SPLIT, 1, 2 * FFN_CH)),
                  _const_spec((FFN_SPLIT, FFN_CH, D_MODEL)),
                  _const_spec((1, D_MODEL))],
        out_specs=row_spec,
        out_shape=jax.ShapeDtypeStruct((n, D_MODEL), F32),
        scratch_shapes=[pltpu.VMEM((FFN_SPLIT, rows + SUBLANES, 2 * FFN_CH), F32)],
        compiler_params=_params("arbitrary", "arbitrary"),
        name="ffn",
    )(x1, mod3, g2, wup, cw, cb, wdn, gf)


def _split_cols(a):
    parts = [jnp.concatenate([a[..., c * FFN_CH:(c + 1) * FFN_CH],
                              a[..., D_FF + c * FFN_CH:D_FF + (c + 1) * FFN_CH]], axis=-1)
             for c in range(FFN_SPLIT)]
    return jnp.stack(parts, axis=0)


def _block_diag_ones(n, blk, value, dtype):
    i = jnp.arange(n) // blk
    return jnp.where(i[:, None] == i[None, :], value, 0.0).astype(dtype)


def kernel(x, c, w_ada, b_ada, norm1_g, w_in, mu_shift, rwkv_w0, rwkv_w_up, rwkv_a0, rwkv_a_up,
           rwkv_g_up, rwkv_k_k, rwkv_k_a, rwkv_r_k, rwkv_ln_g, rwkv_ln_b, w_out_rwkv, s5_a_re,
           s5_a_im, s5_log_dt, s5_b_re, s5_b_im, s5_c_re, s5_c_im, s5_d, w_glu, w_out, norm2_g,
           w_ffn_up, ffn_conv_w, ffn_conv_b, w_ffn_down, norm_f_g):
    bsz, seq, _ = x.shape
    depth = w_ada.shape[0]
    n = bsz * seq
    w = RWKV_WIDTH
    rows_a = min(256, seq)
    rows_m = min(512, seq)
    rows_f = min(512, seq)
    s5_steps = min(64, seq)

    g64 = _block_diag_ones(MXU_TILE, RWKV_HEAD, 1.0, BF16)
    gmean = _block_diag_ones(MXU_TILE, RWKV_HEAD, 1.0 / RWKV_HEAD, BF16)
    eye8 = jnp.eye(LANES // S5_GROUP, dtype=F32)
    row = lambda a: a.reshape(1, -1)

    assert depth == 1, depth
    x2 = x.reshape(n, D_MODEL)
    for l in range(depth):
        mod3 = _adaln(c, w_ada[l], b_ada[l]).reshape(bsz, 6, D_MODEL)

        zero = jnp.zeros((LORA_W, w), F32)
        waup = jnp.concatenate([jnp.concatenate([rwkv_w_up[l], zero], axis=1),
                                jnp.concatenate([zero, rwkv_a_up[l]], axis=1)], axis=0).astype(BF16)
        r, k, v, ld, kn, b, g, bg, u, gates = _inproj(
            x2, mod3, row(norm1_g[l]), w_in[l].astype(BF16), row(mu_shift[l]), row(rwkv_w0[l]), waup,
            row(rwkv_a0[l]), rwkv_g_up[l].astype(BF16), row(rwkv_k_k[l]), row(rwkv_k_a[l]),
            row(rwkv_r_k[l]), g64, bsz=bsz, seq=seq, rows=rows_a)

        y = _wkv(r, k, v, ld, kn, b, bsz=bsz, seq=seq, rows=min(1024, seq))

        ab_re, ab_im, bb_re, bb_im = _s5prep(s5_a_re[l], s5_a_im[l], s5_log_dt[l], s5_b_re[l], s5_b_im[l])
        gl = LANES // S5_GROUP
        slab_in = lambda bb: jnp.einsum('jgcp,gh->jgchp', bb.reshape(S5_SLABS, gl, S5_GROUP, S5_STATE),
                                        eye8).reshape(S5_SLABS, LANES, S5_SLAB_STATE).astype(BF16)
        slab_out = lambda cc: jnp.einsum('jgcp,gh->jgphc', cc.reshape(S5_SLABS, gl, S5_GROUP, S5_STATE),
                                         eye8).reshape(S5_SLABS, S5_SLAB_STATE, LANES)
        cc = jnp.concatenate([slab_out(s5_c_re[l]), -slab_out(s5_c_im[l])], axis=1).astype(BF16)
        u_t = jnp.transpose(u.reshape(bsz, seq, S5_WIDTH), (1, 0, 2))
        ys5_t = _s5(u_t, slab_in(bb_re), slab_in(bb_im), cc, row(ab_re), row(ab_im), row(s5_d[l]),
                    bsz=bsz, seq=seq, steps=s5_steps)
        ys5 = jnp.transpose(ys5_t, (1, 0, 2)).reshape(n, S5_WIDTH)

        x1 = _merge(y, g, bg, ys5, gates, x2, mod3, row(rwkv_ln_g[l]), row(rwkv_ln_b[l]), gmean,
                    w_out_rwkv[l].astype(BF16), w_glu[l].astype(BF16), w_out[l].astype(BF16),
                    bsz=bsz, seq=seq, rows=rows_m)

        gf = row(norm_f_g)
        wdn = w_ffn_down[l].astype(BF16).reshape(FFN_SPLIT, FFN_CH, D_MODEL)
        x2 = _ffn(x1, mod3, row(norm2_g[l]), _split_cols(w_ffn_up[l]).astype(BF16),
                  _split_cols(ffn_conv_w[l]), _split_cols(ffn_conv_b[l].reshape(1, -1)), wdn,
                  gf, bsz=bsz, seq=seq, rows=rows_f)
    return x2.reshape(bsz, seq, D_MODEL)
```

```python
import functools
import math

import jax
import jax.numpy as jnp
from jax import lax
from jax.experimental import pallas as pl
from jax.experimental.pallas import tpu as pltpu

F32 = jnp.float32
BF16 = jnp.bfloat16
HIGHEST = lax.Precision.HIGHEST

D_MODEL = 1024
RWKV_WIDTH = 512
RWKV_HEAD = 64
LORA_W = 64
LORA_A = 64
LORA_G = 128
S5_WIDTH = 512
S5_GROUP = 16
S5_GROUPS = 32
S5_STATE = 64
D_FF = 2816
CONV_W = 3
RMS_EPS = 1e-6
GN_EPS = 64e-5
L2_EPS = 1e-12
SHIFT_COLS = 3 * RWKV_WIDTH + LORA_W + LORA_A + LORA_G
IN_COLS = SHIFT_COLS + S5_WIDTH + 2 * D_MODEL

LANES = 128
SUBLANES = 8
MXU_TILE = 256
WKV_CHUNK = 64
HEADS_PER_TILE = LANES // RWKV_HEAD
STACK = HEADS_PER_TILE * WKV_CHUNK
S5_SLABS = S5_WIDTH // LANES
S5_SLAB_STATE = (LANES // S5_GROUP) * S5_STATE
S5_TILES = 2
FFN_SPLIT = 2
FFN_CH = D_FF // FFN_SPLIT
VMEM_LIMIT = 56 * 1024 * 1024


def _dot(a, b):
    return jnp.dot(a.astype(BF16), b.astype(BF16), preferred_element_type=F32)


def _dot_split(a, g):
    hi = a.astype(BF16)
    lo = (a - hi.astype(F32)).astype(BF16)
    return (jnp.dot(hi, g, preferred_element_type=F32)
            + jnp.dot(lo, g, preferred_element_type=F32))


def _head_sum(a, g, split=False):
    f = _dot_split if split else _dot
    return jnp.concatenate([f(a[:, i:i + MXU_TILE], g) for i in range(0, a.shape[1], MXU_TILE)], axis=1)


def _aligned(i, m):
    return i if isinstance(i, int) else pl.multiple_of(i, m)


def _const_spec(shape):
    nd = len(shape)
    return pl.BlockSpec(shape, lambda *_: (0,) * nd, pipeline_mode=pl.Buffered(1))


def _params(*sem):
    return pltpu.CompilerParams(dimension_semantics=sem, vmem_limit_bytes=VMEM_LIMIT)


def _adaln_kernel(c_ref, w_ref, b_ref, o_ref):
    c = c_ref[...]
    s = c * jax.nn.sigmoid(c)
    o_ref[...] = jnp.dot(s, w_ref[...], preferred_element_type=F32, precision=HIGHEST) + b_ref[...]


def _adaln(c, w_ada, b_ada):
    bsz = c.shape[0]
    n = w_ada.shape[1]
    bn = D_MODEL
    return pl.pallas_call(
        _adaln_kernel,
        grid=(n // bn,),
        in_specs=[pl.BlockSpec((bsz, D_MODEL), lambda j: (0, 0)),
                  pl.BlockSpec((D_MODEL, bn), lambda j: (0, j)),
                  pl.BlockSpec((1, bn), lambda j: (0, j))],
        out_specs=pl.BlockSpec((bsz, bn), lambda j: (0, j)),
        out_shape=jax.ShapeDtypeStruct((bsz, n), F32),
        compiler_params=_params("arbitrary"),
        name="adaln",
    )(c, w_ada, b_ada.reshape(1, n))


def _s5prep_kernel(are_ref, aim_ref, ldt_ref, bre_ref, bim_ref, abre_o, abim_o, bbre_o, bbim_o):
    a_re = are_ref[...]
    a_im = aim_ref[...]
    dt = jnp.exp(ldt_ref[...])
    z_re, z_im = a_re * dt, a_im * dt
    mag = jnp.exp(z_re)
    ab_re, ab_im = mag * jnp.cos(z_im), mag * jnp.sin(z_im)
    den = a_re * a_re + a_im * a_im
    q_re = ((ab_re - 1.0) * a_re + ab_im * a_im) / den
    q_im = (ab_im * a_re - (ab_re - 1.0) * a_im) / den
    abre_o[...] = ab_re
    abim_o[...] = ab_im
    b_re = bre_ref[...]
    b_im = bim_ref[...]
    bbre_o[...] = q_re[:, None, :] * b_re - q_im[:, None, :] * b_im
    bbim_o[...] = q_re[:, None, :] * b_im + q_im[:, None, :] * b_re


def _s5prep(a_re, a_im, log_dt, b_re, b_im):
    g, p = a_re.shape
    c = b_re.shape[-1]
    b_re_t = jnp.transpose(b_re, (0, 2, 1))
    b_im_t = jnp.transpose(b_im, (0, 2, 1))
    return pl.pallas_call(
        _s5prep_kernel,
        out_shape=(jax.ShapeDtypeStruct((g, p), F32), jax.ShapeDtypeStruct((g, p), F32),
                   jax.ShapeDtypeStruct((g, c, p), F32), jax.ShapeDtypeStruct((g, c, p), F32)),
        name="s5prep",
    )(a_re, a_im, log_dt.reshape(g, 1), b_re_t, b_im_t)


def _inproj_kernel(x_ref, mod_ref, g1_ref, win_ref, mu_ref, w0_ref, waup_ref, a0_ref, gup_ref,
                   kk_ref, ka_ref, rk_ref, g64_ref,
                   r_o, k_o, v_o, ld_o, kn_o, b_o, g_o, bg_o, u_o, gates_o, carry, *, rows):
    @pl.when(pl.program_id(1) == 0)
    def _():
        carry[...] = jnp.zeros_like(carry)

    x = x_ref[...]
    mod = mod_ref[...]
    sh1, sc1 = mod[0:1], mod[1:2]
    ms = jnp.mean(x * x, axis=-1, keepdims=True)
    h = x * lax.rsqrt(ms + RMS_EPS) * g1_ref[...]
    h = h * (1.0 + sc1) + sh1
    proj = jnp.dot(h.astype(BF16), win_ref[...], preferred_element_type=F32)

    p = proj[:, :SHIFT_COLS]
    row = lax.broadcasted_iota(jnp.int32, p.shape, 0)
    prev = jnp.where(row == 0, carry[...], pltpu.roll(p, 1, axis=0))
    carry[...] = p[rows - 1:rows, :]
    ps = p + (prev - p) * mu_ref[...]

    w = RWKV_WIDTH
    r, k, v = ps[:, 0:w], ps[:, w:2 * w], ps[:, 2 * w:3 * w]
    wa = ps[:, 3 * w:3 * w + LORA_W + LORA_A]
    lane = lax.broadcasted_iota(jnp.int32, wa.shape, 1)
    wa = jnp.where(lane < LORA_W, jnp.tanh(wa), wa)
    delta = _dot(wa, waup_ref[...])
    w_raw = w0_ref[...] + delta[:, :w]
    ld = -math.exp(-0.5) * jax.nn.sigmoid(w_raw)
    eta = jax.nn.sigmoid(a0_ref[...] + delta[:, w:])
    gd = ps[:, 3 * w + LORA_W + LORA_A:SHIFT_COLS]
    g = _dot(jax.nn.sigmoid(gd), gup_ref[...])

    g64 = g64_ref[...]
    kk = k * kk_ref[...]
    kn = kk * lax.rsqrt(_head_sum(kk * kk, g64) + L2_EPS)
    k2 = k * (1.0 + (eta - 1.0) * ka_ref[...])
    bonus = _head_sum(r * k2 * rk_ref[...], g64) * v

    r_o[...] = r
    k_o[...] = k2
    v_o[...] = v
    ld_o[...] = ld
    kn_o[...] = kn
    b_o[...] = kn * eta
    g_o[...] = g
    bg_o[...] = bonus * g
    u_o[...] = proj[:, SHIFT_COLS:SHIFT_COLS + S5_WIDTH]
    gates_o[...] = jax.nn.sigmoid(proj[:, SHIFT_COLS + S5_WIDTH:])


def _inproj(x2, mod3, g1, win, mu, w0, waup, a0, gup, k_k, k_a, r_k, g64, *, bsz, seq, rows):
    n = bsz * seq
    nt = seq // rows
    w = RWKV_WIDTH
    row_spec = lambda cols: pl.BlockSpec((rows, cols), lambda b, t: (b * nt + t, 0))
    outs = [jax.ShapeDtypeStruct((n, w), F32)] * 9 + [jax.ShapeDtypeStruct((n, 2 * D_MODEL), F32)]
    return pl.pallas_call(
        functools.partial(_inproj_kernel, rows=rows),
        grid=(bsz, nt),
        in_specs=[row_spec(D_MODEL),
                  pl.BlockSpec((None, 6, D_MODEL), lambda b, t: (b, 0, 0)),
                  _const_spec((1, D_MODEL)),
                  _const_spec((D_MODEL, IN_COLS)),
                  _const_spec((1, SHIFT_COLS)),
                  _const_spec((1, w)),
                  _const_spec((LORA_W + LORA_A, 2 * w)),
                  _const_spec((1, w)),
                  _const_spec((LORA_G, w)),
                  _const_spec((1, w)), _const_spec((1, w)), _const_spec((1, w)),
                  _const_spec((MXU_TILE, MXU_TILE))],
        out_specs=[row_spec(w)] * 9 + [row_spec(2 * D_MODEL)],
        out_shape=outs,
        scratch_shapes=[pltpu.VMEM((1, SHIFT_COLS), F32)],
        compiler_params=_params("arbitrary", "arbitrary"),
        name="inproj",
    )(x2, mod3, g1, win, mu, w0, waup, a0, gup, k_k, k_a, r_k, g64)


def _stack_heads(z):
    lane = lax.broadcasted_iota(jnp.int32, z.shape, 1)
    first = lane < RWKV_HEAD
    return jnp.concatenate([jnp.where(first, z, 0.0), jnp.where(first, 0.0, z)], axis=0)


def _cumsum_rows(x):
    n = x.shape[0]
    row = lax.broadcasted_iota(jnp.int32, x.shape, 0)
    d = 1
    while d < n:
        if d < SUBLANES:
            shifted = jnp.where(row >= d, pltpu.roll(x, d, axis=0), 0.0)
        else:
            shifted = jnp.concatenate([jnp.zeros((d, x.shape[1]), x.dtype), x[:n - d]], axis=0)
        x = x + shifted
        d *= 2
    return x


_WKV_STAGE_BOUNDARIES = 6


def _spread_matrix():
    src = jnp.arange(LANES)
    dst = jnp.arange((SUBLANES - 1) * LANES)
    j, lane = dst // LANES, dst % LANES
    hit = (src[:, None] // SUBLANES == lane[None, :] // SUBLANES) & (src[:, None] % SUBLANES == j[None, :])
    return hit.astype(BF16)


def _wkv_terms(chains, spread, side=()):
    t = WKV_CHUNK
    n = STACK
    jobs = list(side)
    calls = [0]

    def between():
        k = calls[0]
        calls[0] += 1
        for _ in range((k + 1) * len(side) // _WKV_STAGE_BOUNDARIES - k * len(side) // _WKV_STAGE_BOUNDARIES):
            jobs.pop(0)()

    ri = lax.broadcasted_iota(jnp.int32, (n, n), 0)
    ci = lax.broadcasted_iota(jnp.int32, (n, n), 1)
    same_head = (ri // t) == (ci // t)
    strict = same_head & (ri > ci)
    incl = same_head & (ri >= ci)

    a_s, r_s, v_s, lhs, rhs, bk, w_end = [], [], [], [], [], [], []
    for r, k, v, ld, kn, b, cl in chains:
        cl_end = cl[t - 1:t, :]
        e_incl = jnp.exp(cl)
        e_excl = jnp.exp(cl - ld)
        e_neg = jnp.exp(-cl)
        e_end = jnp.exp(cl_end - cl)
        a_s.append(_stack_heads(-kn * e_excl))
        r_s.append(_stack_heads(r * e_incl))
        v_s.append(_stack_heads(v))
        lhs.append(jnp.concatenate([a_s[-1], r_s[-1]], axis=0).astype(BF16))
        bd, kd = (b * e_neg).astype(BF16), (k * e_neg).astype(BF16)
        rhs.append(jnp.concatenate([bd, bd, kd, kd], axis=0))
        bk.append(jnp.concatenate([_stack_heads(b * e_end), _stack_heads(k * e_end)], axis=0).astype(BF16))
        w_end.append(jnp.exp(cl_end))

    amat = [lax.dot_general(l, rr, (((1,), (1,)), ((), ())), preferred_element_type=F32)
            for l, rr in zip(lhs, rhs)]
    a_ab = [jnp.where(strict, a[:n, :n], 0.0) for a in amat]
    a_ak = [jnp.where(strict, a[:n, n:], 0.0).astype(BF16) for a in amat]
    a_r = [jnp.concatenate([jnp.where(incl, a[n:, :n], 0.0), jnp.where(incl, a[n:, n:], 0.0)],
                           axis=1).astype(BF16) for a in amat]

    base = SUBLANES
    blk = (ri // base) == (ci // base)
    packed = []
    for a in a_ab:
        d = jnp.where(blk, a, 0.0)
        acc = d[0:base]
        for q in range(1, n // base):
            acc = acc + d[q * base:(q + 1) * base]
        packed.append(acc)
    cols = jnp.dot(jnp.concatenate(packed, axis=0).astype(BF16), spread,
                   preferred_element_type=F32)
    sub = lax.broadcasted_iota(jnp.int32, (base, LANES), 0)
    lane = lax.broadcasted_iota(jnp.int32, (base, LANES), 1)
    eye_packed = jnp.where(sub == lane % base, 1.0, 0.0)
    x = []
    for i in range(len(a_ab)):
        xp = eye_packed
        for j in range(base - 1):
            lj = cols[i * base:(i + 1) * base, j * LANES:(j + 1) * LANES]
            xp = xp + lj * jnp.broadcast_to(xp[j:j + 1, :], (base, LANES))
        x.append(jnp.where(blk, jnp.concatenate([xp] * (n // base), axis=0), 0.0))
    between()
    s = base
    while s < t:
        level = ((ri // (2 * s)) == (ci // (2 * s))) & ((ri // s) % 2 == 1) & ((ci // s) % 2 == 0)
        c = [jnp.where(level, a, 0.0).astype(BF16) for a in a_ab]
        xb = [xi.astype(BF16) for xi in x]
        odd = lambda m: jnp.concatenate([m[q * s:(q + 1) * s] for q in range(1, n // s, 2)], axis=0)
        xo = [odd(xi) for xi in x]
        xc = [jnp.dot(xoi.astype(BF16), cc, preferred_element_type=F32).astype(BF16)
              for xoi, cc in zip(xo, c)]
        xo = [xoi + jnp.dot(xci, xbi, preferred_element_type=F32) for xoi, xci, xbi in zip(xo, xc, xb)]
        x = [jnp.concatenate([xoi[(q // 2) * s:(q // 2 + 1) * s] if q % 2 else xi[q * s:(q + 1) * s]
                              for q in range(n // s)], axis=0) for xi, xoi in zip(x, xo)]
        between()
        s *= 2

    akv = [jnp.dot(a, v.astype(BF16), preferred_element_type=F32) for a, v in zip(a_ak, v_s)]
    tg = [_dot(xi, jnp.concatenate([a, kv], axis=1)) for xi, a, kv in zip(x, a_s, akv)]
    between()
    low = [jnp.concatenate([g, jnp.concatenate([jnp.zeros_like(v), v], axis=1)], axis=0).astype(BF16)
           for g, v in zip(tg, v_s)]
    out1 = [jnp.dot(a, lo, preferred_element_type=F32) for a, lo in zip(a_r, low)]
    out2 = [lax.dot_general(bb, lo, (((0,), (0,)), ((), ())), preferred_element_type=F32)
            for bb, lo in zip(bk, low)]
    between()
    ki = lax.broadcasted_iota(jnp.int32, (LANES, LANES), 0)
    kj = lax.broadcasted_iota(jnp.int32, (LANES, LANES), 1)
    terms = []
    for rs, o1, o2, we in zip(r_s, out1, out2, w_end):
        r_hat = rs + o1[:, :LANES]
        m = o2[:, :LANES] + jnp.where(ki == kj, we, 0.0)
        terms.append((r_hat, o1[:, LANES:], m, o2[:, LANES:]))
    return terms


def _wkv_kernel(r_ref, k_ref, v_ref, ld_ref, kn_ref, b_ref, spread_ref, y_ref, s_ref, mr_ref, nc_ref,
                yv_ref, *, rows, group):
    t = WKV_CHUNK
    npair = RWKV_WIDTH // LANES

    @pl.when(pl.program_id(1) == 0)
    def _():
        s_ref[...] = jnp.zeros_like(s_ref)

    def chain_step(c):
        rs = pl.ds(_aligned(c * STACK, STACK), STACK)
        outs = [jnp.dot(mr_ref[p, pl.ds(_aligned(c * 2 * STACK, 2 * STACK), 2 * STACK), :],
                        s_ref[p].astype(BF16), preferred_element_type=F32) for p in range(npair)]
        for p, out in enumerate(outs):
            s_ref[p] = out[:LANES] + nc_ref[p, rs, :]
            ys = out[LANES:] + yv_ref[p, rs, :]
            y_ref[pl.ds(_aligned(c * t, t), t), p * LANES:(p + 1) * LANES] = ys[:t] + ys[t:]

    def precompute(g, side):
        chains, where = [], []
        for j in range(group):
            c = g * group + j
            sl = pl.ds(_aligned(c * t, t), t)
            tiles = [ref[sl, :] for ref in (r_ref, k_ref, v_ref, ld_ref, kn_ref, b_ref)]
            cl = _cumsum_rows(tiles[3])
            for p in range(npair):
                ls = slice(p * LANES, (p + 1) * LANES)
                chains.append(tuple(a[:, ls] for a in tiles) + (cl[:, ls],))
                where.append((c, p))
        for (c, p), (r_hat, y_v, m, n_c) in zip(where, _wkv_terms(chains, spread_ref[...], side)):
            mr_ref[p, pl.ds(_aligned(c * 2 * STACK, 2 * STACK), 2 * STACK), :] = (
                jnp.concatenate([m, r_hat], axis=0).astype(BF16))
            rs = pl.ds(_aligned(c * STACK, STACK), STACK)
            nc_ref[p, rs, :] = n_c
            yv_ref[p, rs, :] = y_v

    ngroup = rows // (t * group)
    precompute(0, ())

    def body(g, _):
        precompute(g, [functools.partial(chain_step, (g - 1) * group + j) for j in range(group)])
        return 0

    lax.fori_loop(1, ngroup, body, 0)
    for j in range(group):
        chain_step((ngroup - 1) * group + j)


def _wkv(r, k, v, ld, kn, b, *, bsz, seq, rows):
    n = bsz * seq
    nt = seq // rows
    npair = RWKV_WIDTH // LANES
    nchunk = rows // WKV_CHUNK
    spec = pl.BlockSpec((rows, RWKV_WIDTH), lambda bi, ti: (bi * nt + ti, 0))
    return pl.pallas_call(
        functools.partial(_wkv_kernel, rows=rows, group=4),
        grid=(bsz, nt),
        in_specs=[spec] * 6 + [_const_spec((LANES, (SUBLANES - 1) * LANES))],
        out_specs=spec,
        out_shape=jax.ShapeDtypeStruct((n, RWKV_WIDTH), F32),
        scratch_shapes=[pltpu.VMEM((npair, LANES, LANES), F32),
                        pltpu.VMEM((npair, nchunk * 2 * STACK, LANES), BF16),
                        pltpu.VMEM((npair, nchunk * STACK, LANES), F32),
                        pltpu.VMEM((npair, nchunk * STACK, LANES), F32)],
        compiler_params=_params("arbitrary", "arbitrary"),
        name="wkv",
    )(r, k, v, ld, kn, b, _spread_matrix())


def _gelu_tanh(x):
    return 0.5 * x * (1.0 + jnp.tanh(math.sqrt(2.0 / math.pi) * (x + 0.044715 * x * x * x)))


def _s5_kernel(u_ref, wre_ref, wim_ref, cc_ref, ar_ref, ai_ref, d_ref, y_ref,
               xre, xim, st_re, st_im, *, steps):
    @pl.when(pl.program_id(1) == 0)
    def _():
        st_re[...] = jnp.zeros_like(st_re)
        st_im[...] = jnp.zeros_like(st_im)

    rows = steps * SUBLANES
    ns = S5_SLAB_STATE
    half = S5_GROUPS * S5_STATE // 2
    u = [jnp.swapaxes(u_ref[g * SUBLANES:(g + 1) * SUBLANES, :, :], 0, 1).reshape(rows, S5_WIDTH)
         for g in range(S5_TILES)]
    ub = [a.astype(BF16) for a in u]

    def in_proj_jobs(g):
        def job(j, ref, w_ref):
            ref[g, :, j * ns:(j + 1) * ns] = jnp.dot(ub[g][:, j * LANES:(j + 1) * LANES], w_ref[j],
                                                     preferred_element_type=F32)
        return [functools.partial(job, j, ref, w_ref)
                for j in range(S5_SLABS) for ref, w_ref in ((xre, wre_ref), (xim, wim_ref))]

    def out_proj_jobs(g):
        def job(j):
            xs = jnp.concatenate([xre[g, :, j * ns:(j + 1) * ns], xim[g, :, j * ns:(j + 1) * ns]], axis=1)
            ls = slice(j * LANES, (j + 1) * LANES)
            yj = (jnp.dot(xs.astype(BF16), cc_ref[j], preferred_element_type=F32)
                  + d_ref[:, ls] * u[g][:, ls])
            y_ref[g * SUBLANES:(g + 1) * SUBLANES, :, ls] = jnp.swapaxes(
                _gelu_tanh(yj).reshape(steps, SUBLANES, LANES), 0, 1)
        return [functools.partial(job, j) for j in range(S5_SLABS)]

    def scan(g, jobs):
        total = 2 * steps
        done = 0
        for c in range(2):
            cs = slice(c * half, (c + 1) * half)
            ar = jnp.broadcast_to(ar_ref[:, cs], (SUBLANES, half))
            ai = jnp.broadcast_to(ai_ref[:, cs], (SUBLANES, half))
            x_re, x_im = st_re[g, :, cs], st_im[g, :, cs]
            for i in range(steps):
                k = c * steps + i
                for _ in range((k + 1) * len(jobs) // total - done):
                    jobs[done]()
                    done += 1
                rs = slice(i * SUBLANES, (i + 1) * SUBLANES)
                n_re = ar * x_re - ai * x_im + xre[g, rs, cs]
                n_im = ar * x_im + ai * x_re + xim[g, rs, cs]
                xre[g, rs, cs] = n_re
                xim[g, rs, cs] = n_im
                x_re, x_im = n_re, n_im
            st_re[g, :, cs] = x_re
            st_im[g, :, cs] = x_im

    for job in in_proj_jobs(0):
        job()
    scan(0, in_proj_jobs(1))
    scan(1, out_proj_jobs(0))
    for job in out_proj_jobs(1):
        job()


def _s5(u3, wre, wim, cc, ar, ai, d, *, bsz, seq, steps):
    nstate = S5_GROUPS * S5_STATE
    rows = steps * SUBLANES
    nb = S5_TILES * SUBLANES
    spec = pl.BlockSpec((nb, steps, S5_WIDTH), lambda g, t: (g, t, 0))
    return pl.pallas_call(
        functools.partial(_s5_kernel, steps=steps),
        grid=(bsz // nb, seq // steps),
        in_specs=[spec,
                  _const_spec((S5_SLABS, LANES, S5_SLAB_STATE)),
                  _const_spec((S5_SLABS, LANES, S5_SLAB_STATE)),
                  _const_spec((S5_SLABS, 2 * S5_SLAB_STATE, LANES)),
                  _const_spec((1, nstate)), _const_spec((1, nstate)),
                  _const_spec((1, S5_WIDTH))],
        out_specs=spec,
        out_shape=jax.ShapeDtypeStruct((bsz, seq, S5_WIDTH), F32),
        scratch_shapes=[pltpu.VMEM((S5_TILES, rows, nstate), F32), pltpu.VMEM((S5_TILES, rows, nstate), F32),
                        pltpu.VMEM((S5_TILES, SUBLANES, nstate), F32),
                        pltpu.VMEM((S5_TILES, SUBLANES, nstate), F32)],
        compiler_params=_params("arbitrary", "arbitrary"),
        name="s5",
    )(u3, wre, wim, cc, ar, ai, d)


def _merge_kernel(y_ref, g_ref, bg_ref, ys5_ref, gates_ref, x_ref, mod_ref, lng_ref, lnb_ref,
                  gmean_ref, wrw_ref, wglu_ref, wout_ref, o_ref):
    gm = gmean_ref[...]
    y = y_ref[...]
    mean = _head_sum(y, gm, split=True)
    d = y - mean
    var = _head_sum(d * d, gm)
    yn = d * lax.rsqrt(var + GN_EPS) * lng_ref[...] + lnb_ref[...]
    ya = _dot(yn * g_ref[...] + bg_ref[...], wrw_ref[...])
    z = _dot(ys5_ref[...], wglu_ref[...])
    yb = z[:, :D_MODEL] * jax.nn.sigmoid(z[:, D_MODEL:])
    gates = gates_ref[...]
    mixed = _dot(gates[:, :D_MODEL] * ya + gates[:, D_MODEL:] * yb, wout_ref[...])
    gt1 = mod_ref[...][2:3]
    o_ref[...] = x_ref[...] + gt1 * mixed


def _merge(y, g, bg, ys5, gates, x2, mod3, lng, lnb, gmean, wrw, wglu, wout, *, bsz, seq, rows):
    n = bsz * seq
    nt = seq // rows
    w = RWKV_WIDTH
    row_spec = lambda cols: pl.BlockSpec((rows, cols), lambda i: (i, 0))
    return pl.pallas_call(
        _merge_kernel,
        grid=(n // rows,),
        in_specs=[row_spec(w), row_spec(w), row_spec(w), row_spec(w), row_spec(2 * D_MODEL),
                  row_spec(D_MODEL),
                  pl.BlockSpec((None, 6, D_MODEL), lambda i: (i // nt, 0, 0)),
                  _const_spec((1, w)), _const_spec((1, w)), _const_spec((MXU_TILE, MXU_TILE)),
                  _const_spec((w, D_MODEL)), _const_spec((S5_WIDTH, 2 * D_MODEL)),
                  _const_spec((D_MODEL, D_MODEL))],
        out_specs=row_spec(D_MODEL),
        out_shape=jax.ShapeDtypeStruct((n, D_MODEL), F32),
        compiler_params=_params("arbitrary"),
        name="merge",
    )(y, g, bg, ys5, gates, x2, mod3, lng, lnb, gmean, wrw, wglu, wout)


def _ffn_kernel(x_ref, mod_ref, g2_ref, wup_ref, cw_ref, cb_ref, wdn_ref, gf_ref, o_ref, buf, *, rows):
    pad = SUBLANES

    @pl.when(pl.program_id(1) == 0)
    def _():
        buf[:, 0:pad, :] = jnp.zeros((FFN_SPLIT, pad, 2 * FFN_CH), F32)

    mod = mod_ref[...]
    sh2, sc2, gt2 = mod[3:4], mod[4:5], mod[5:6]

    def normed(xh):
        ms = jnp.mean(xh * xh, axis=-1, keepdims=True)
        h = xh * lax.rsqrt(ms + RMS_EPS) * g2_ref[...]
        return (h * (1.0 + sc2) + sh2).astype(BF16)

    halves = 2
    hr = rows // halves
    x = [x_ref[h * hr:(h + 1) * hr, :] for h in range(halves)]
    hb = [normed(x[0]), None]
    acc = [jnp.zeros((hr, D_MODEL), F32) for _ in range(halves)]
    for c in range(FFN_SPLIT):
        for h in range(halves):
            if hb[h] is None:
                hb[h] = normed(x[h])
            r0 = pad + h * hr
            buf[c, r0:r0 + hr, :] = jnp.dot(hb[h], wup_ref[c], preferred_element_type=F32)
        cw = cw_ref[c]
        for h in range(halves):
            hid = cb_ref[c]
            for j in range(CONV_W):
                off = pad + h * hr - (CONV_W - 1) + j
                hid = hid + cw[j:j + 1, :] * buf[c, off:off + hr, :]
            gate, up = hid[:, :FFN_CH], hid[:, FFN_CH:]
            act = (0.5 * gate) * (1.0 + jnp.tanh(0.5 * gate)) * up
            acc[h] = acc[h] + jnp.dot(act.astype(BF16), wdn_ref[c], preferred_element_type=F32)
        buf[c, pad - (CONV_W - 1):pad, :] = buf[c, pad + rows - (CONV_W - 1):pad + rows, :]

    for h in range(halves):
        x2 = x[h] + gt2 * acc[h]
        ms2 = jnp.mean(x2 * x2, axis=-1, keepdims=True)
        o_ref[h * hr:(h + 1) * hr, :] = x2 * lax.rsqrt(ms2 + RMS_EPS) * gf_ref[...]


def _ffn(x1, mod3, g2, wup, cw, cb, wdn, gf, *, bsz, seq, rows):
    n = bsz * seq
    nt = seq // rows
    row_spec = pl.BlockSpec((rows, D_MODEL), lambda b, t: (b * nt + t, 0))
    return pl.pallas_call(
        functools.partial(_ffn_kernel, rows=rows),
        grid=(bsz, nt),
        in_specs=[row_spec,
                  pl.BlockSpec((None, 6, D_MODEL), lambda b, t: (b, 0, 0)),
                  _const_spec((1, D_MODEL)),
                  _const_spec((FFN_SPLIT, D_MODEL, 2 * FFN_CH)),
                  _const_spec((FFN_SPLIT, CONV_W, 2 * FFN_CH)),
                  _const_spec((FFN_SPLIT, 1, 2 * FFN_CH)),
                  _const_spec((FFN_SPLIT, FFN_CH, D_MODEL)),
                  _const_spec((1, D_MODEL))],
        out_specs=row_spec,
        out_shape=jax.ShapeDtypeStruct((n, D_MODEL), F32),
        scratch_shapes=[pltpu.VMEM((FFN_SPLIT, rows + SUBLANES, 2 * FFN_CH), F32)],
        compiler_params=_params("arbitrary", "arbitrary"),
        name="ffn",
    )(x1, mod3, g2, wup, cw, cb, wdn, gf)


def _split_cols(a):
    parts = [jnp.concatenate([a[..., c * FFN_CH:(c + 1) * FFN_CH],
                              a[..., D_FF + c * FFN_CH:D_FF + (c + 1) * FFN_CH]], axis=-1)
             for c in range(FFN_SPLIT)]
    return jnp.stack(parts, axis=0)


def _block_diag_ones(n, blk, value, dtype):
    i = jnp.arange(n) // blk
    return jnp.where(i[:, None] == i[None, :], value, 0.0).astype(dtype)


def kernel(x, c, w_ada, b_ada, norm1_g, w_in, mu_shift, rwkv_w0, rwkv_w_up, rwkv_a0, rwkv_a_up,
           rwkv_g_up, rwkv_k_k, rwkv_k_a, rwkv_r_k, rwkv_ln_g, rwkv_ln_b, w_out_rwkv, s5_a_re,
           s5_a_im, s5_log_dt, s5_b_re, s5_b_im, s5_c_re, s5_c_im, s5_d, w_glu, w_out, norm2_g,
           w_ffn_up, ffn_conv_w, ffn_conv_b, w_ffn_down, norm_f_g):
    bsz, seq, _ = x.shape
    depth = w_ada.shape[0]
    n = bsz * seq
    w = RWKV_WIDTH
    rows_a = min(256, seq)
    rows_m = min(512, seq)
    rows_f = min(512, seq)
    s5_steps = min(64, seq)

    g64 = _block_diag_ones(MXU_TILE, RWKV_HEAD, 1.0, BF16)
    gmean = _block_diag_ones(MXU_TILE, RWKV_HEAD, 1.0 / RWKV_HEAD, BF16)
    eye8 = jnp.eye(LANES // S5_GROUP, dtype=F32)
    row = lambda a: a.reshape(1, -1)

    assert depth == 1, depth
    x2 = x.reshape(n, D_MODEL)
    for l in range(depth):
        mod3 = _adaln(c, w_ada[l], b_ada[l]).reshape(bsz, 6, D_MODEL)

        zero = jnp.zeros((LORA_W, w), F32)
        waup = jnp.concatenate([jnp.concatenate([rwkv_w_up[l], zero], axis=1),
                                jnp.concatenate([zero, rwkv_a_up[l]], axis=1)], axis=0).astype(BF16)
        r, k, v, ld, kn, b, g, bg, u, gates = _inproj(
            x2, mod3, row(norm1_g[l]), w_in[l].astype(BF16), row(mu_shift[l]), row(rwkv_w0[l]), waup,
            row(rwkv_a0[l]), rwkv_g_up[l].astype(BF16), row(rwkv_k_k[l]), row(rwkv_k_a[l]),
            row(rwkv_r_k[l]), g64, bsz=bsz, seq=seq, rows=rows_a)

        y = _wkv(r, k, v, ld, kn, b, bsz=bsz, seq=seq, rows=min(1024, seq))

        ab_re, ab_im, bb_re, bb_im = _s5prep(s5_a_re[l], s5_a_im[l], s5_log_dt[l], s5_b_re[l], s5_b_im[l])
        gl = LANES // S5_GROUP
        slab_in = lambda bb: jnp.einsum('jgcp,gh->jgchp', bb.reshape(S5_SLABS, gl, S5_GROUP, S5_STATE),
                                        eye8).reshape(S5_SLABS, LANES, S5_SLAB_STATE).astype(BF16)
        slab_out = lambda cc: jnp.einsum('jgcp,gh->jgphc', cc.reshape(S5_SLABS, gl, S5_GROUP, S5_STATE),
                                         eye8).reshape(S5_SLABS, S5_SLAB_STATE, LANES)
        cc = jnp.concatenate([slab_out(s5_c_re[l]), -slab_out(s5_c_im[l])], axis=1).astype(BF16)
        ys5 = _s5(u.reshape(bsz, seq, S5_WIDTH), slab_in(bb_re), slab_in(bb_im), cc, row(ab_re),
                  row(ab_im), row(s5_d[l]), bsz=bsz, seq=seq, steps=s5_steps).reshape(n, S5_WIDTH)

        x1 = _merge(y, g, bg, ys5, gates, x2, mod3, row(rwkv_ln_g[l]), row(rwkv_ln_b[l]), gmean,
                    w_out_rwkv[l].astype(BF16), w_glu[l].astype(BF16), w_out[l].astype(BF16),
                    bsz=bsz, seq=seq, rows=rows_m)

        gf = row(norm_f_g)
        wdn = w_ffn_down[l].astype(BF16).reshape(FFN_SPLIT, FFN_CH, D_MODEL)
        x2 = _ffn(x1, mod3, row(norm2_g[l]), _split_cols(w_ffn_up[l]).astype(BF16),
                  _split_cols(ffn_conv_w[l]), _split_cols(ffn_conv_b[l].reshape(1, -1)), wdn,
                  gf, bsz=bsz, seq=seq, rows=rows_f)
    return x2.reshape(bsz, seq, D_MODEL)
```

```python
import functools
import math

import jax
import jax.numpy as jnp
from jax import lax
from jax.experimental import pallas as pl
from jax.experimental.pallas import tpu as pltpu

F32 = jnp.float32
BF16 = jnp.bfloat16
HIGHEST = lax.Precision.HIGHEST

D_MODEL = 1024
RWKV_WIDTH = 512
RWKV_HEAD = 64
LORA_W = 64
LORA_A = 64
LORA_G = 128
S5_WIDTH = 512
S5_GROUP = 16
S5_GROUPS = 32
S5_STATE = 64
D_FF = 2816
CONV_W = 3
RMS_EPS = 1e-6
GN_EPS = 64e-5
L2_EPS = 1e-12
SHIFT_COLS = 3 * RWKV_WIDTH + LORA_W + LORA_A + LORA_G
IN_COLS = SHIFT_COLS + S5_WIDTH + 2 * D_MODEL

LANES = 128
SUBLANES = 8
MXU_TILE = 256
WKV_CHUNK = 64
HEADS_PER_TILE = LANES // RWKV_HEAD
STACK = HEADS_PER_TILE * WKV_CHUNK
S5_SLABS = S5_WIDTH // LANES
S5_SLAB_STATE = (LANES // S5_GROUP) * S5_STATE
S5_TILES = 2
FFN_SPLIT = 2
FFN_CH = D_FF // FFN_SPLIT
VMEM_LIMIT = 56 * 1024 * 1024


def _dot(a, b):
    return jnp.dot(a.astype(BF16), b.astype(BF16), preferred_element_type=F32)


def _dot_split(a, g):
    hi = a.astype(BF16)
    lo = (a - hi.astype(F32)).astype(BF16)
    return (jnp.dot(hi, g, preferred_element_type=F32)
            + jnp.dot(lo, g, preferred_element_type=F32))


def _head_sum(a, g, split=False):
    f = _dot_split if split else _dot
    return jnp.concatenate([f(a[:, i:i + MXU_TILE], g) for i in range(0, a.shape[1], MXU_TILE)], axis=1)


def _aligned(i, m):
    return i if isinstance(i, int) else pl.multiple_of(i, m)


def _const_spec(shape):
    nd = len(shape)
    return pl.BlockSpec(shape, lambda *_: (0,) * nd, pipeline_mode=pl.Buffered(1))


def _params(*sem):
    return pltpu.CompilerParams(dimension_semantics=sem, vmem_limit_bytes=VMEM_LIMIT)


def _adaln_kernel(c_ref, w_ref, b_ref, o_ref):
    c = c_ref[...]
    s = c * jax.nn.sigmoid(c)
    o_ref[...] = jnp.dot(s, w_ref[...], preferred_element_type=F32, precision=HIGHEST) + b_ref[...]


def _adaln(c, w_ada, b_ada):
    bsz = c.shape[0]
    n = w_ada.shape[1]
    bn = D_MODEL
    return pl.pallas_call(
        _adaln_kernel,
        grid=(n // bn,),
        in_specs=[pl.BlockSpec((bsz, D_MODEL), lambda j: (0, 0)),
                  pl.BlockSpec((D_MODEL, bn), lambda j: (0, j)),
                  pl.BlockSpec((1, bn), lambda j: (0, j))],
        out_specs=pl.BlockSpec((bsz, bn), lambda j: (0, j)),
        out_shape=jax.ShapeDtypeStruct((bsz, n), F32),
        compiler_params=_params("arbitrary"),
        name="adaln",
    )(c, w_ada, b_ada.reshape(1, n))


def _s5prep_kernel(are_ref, aim_ref, ldt_ref, bre_ref, bim_ref, abre_o, abim_o, bbre_o, bbim_o):
    a_re = are_ref[...]
    a_im = aim_ref[...]
    dt = jnp.exp(ldt_ref[...])
    z_re, z_im = a_re * dt, a_im * dt
    mag = jnp.exp(z_re)
    ab_re, ab_im = mag * jnp.cos(z_im), mag * jnp.sin(z_im)
    den = a_re * a_re + a_im * a_im
    q_re = ((ab_re - 1.0) * a_re + ab_im * a_im) / den
    q_im = (ab_im * a_re - (ab_re - 1.0) * a_im) / den
    abre_o[...] = ab_re
    abim_o[...] = ab_im
    b_re = bre_ref[...]
    b_im = bim_ref[...]
    bbre_o[...] = q_re[:, None, :] * b_re - q_im[:, None, :] * b_im
    bbim_o[...] = q_re[:, None, :] * b_im + q_im[:, None, :] * b_re


def _s5prep(a_re, a_im, log_dt, b_re, b_im):
    g, p = a_re.shape
    c = b_re.shape[-1]
    b_re_t = jnp.transpose(b_re, (0, 2, 1))
    b_im_t = jnp.transpose(b_im, (0, 2, 1))
    return pl.pallas_call(
        _s5prep_kernel,
        out_shape=(jax.ShapeDtypeStruct((g, p), F32), jax.ShapeDtypeStruct((g, p), F32),
                   jax.ShapeDtypeStruct((g, c, p), F32), jax.ShapeDtypeStruct((g, c, p), F32)),
        name="s5prep",
    )(a_re, a_im, log_dt.reshape(g, 1), b_re_t, b_im_t)


def _inproj_kernel(x_ref, mod_ref, g1_ref, win_ref, mu_ref, w0_ref, waup_ref, a0_ref, gup_ref,
                   kk_ref, ka_ref, rk_ref, g64_ref,
                   r_o, k_o, v_o, ld_o, kn_o, b_o, g_o, bg_o, u_o, gates_o, carry, *, rows):
    @pl.when(pl.program_id(1) == 0)
    def _():
        carry[...] = jnp.zeros_like(carry)

    x = x_ref[...]
    mod = mod_ref[...]
    sh1, sc1 = mod[0:1], mod[1:2]
    ms = jnp.mean(x * x, axis=-1, keepdims=True)
    h = x * lax.rsqrt(ms + RMS_EPS) * g1_ref[...]
    h = h * (1.0 + sc1) + sh1
    proj = jnp.dot(h.astype(BF16), win_ref[...], preferred_element_type=F32)

    p = proj[:, :SHIFT_COLS]
    row = lax.broadcasted_iota(jnp.int32, p.shape, 0)
    prev = jnp.where(row == 0, carry[...], pltpu.roll(p, 1, axis=0))
    carry[...] = p[rows - 1:rows, :]
    ps = p + (prev - p) * mu_ref[...]

    w = RWKV_WIDTH
    r, k, v = ps[:, 0:w], ps[:, w:2 * w], ps[:, 2 * w:3 * w]
    wa = ps[:, 3 * w:3 * w + LORA_W + LORA_A]
    lane = lax.broadcasted_iota(jnp.int32, wa.shape, 1)
    wa = jnp.where(lane < LORA_W, jnp.tanh(wa), wa)
    delta = _dot(wa, waup_ref[...])
    w_raw = w0_ref[...] + delta[:, :w]
    ld = -math.exp(-0.5) * jax.nn.sigmoid(w_raw)
    eta = jax.nn.sigmoid(a0_ref[...] + delta[:, w:])
    gd = ps[:, 3 * w + LORA_W + LORA_A:SHIFT_COLS]
    g = _dot(jax.nn.sigmoid(gd), gup_ref[...])

    g64 = g64_ref[...]
    kk = k * kk_ref[...]
    kn = kk * lax.rsqrt(_head_sum(kk * kk, g64) + L2_EPS)
    k2 = k * (1.0 + (eta - 1.0) * ka_ref[...])
    bonus = _head_sum(r * k2 * rk_ref[...], g64) * v

    r_o[...] = r
    k_o[...] = k2
    v_o[...] = v
    ld_o[...] = ld
    kn_o[...] = kn
    b_o[...] = kn * eta
    g_o[...] = g
    bg_o[...] = bonus * g
    u_o[...] = proj[:, SHIFT_COLS:SHIFT_COLS + S5_WIDTH]
    gates_o[...] = jax.nn.sigmoid(proj[:, SHIFT_COLS + S5_WIDTH:])


def _inproj(x2, mod3, g1, win, mu, w0, waup, a0, gup, k_k, k_a, r_k, g64, *, bsz, seq, rows):
    n = bsz * seq
    nt = seq // rows
    w = RWKV_WIDTH
    row_spec = lambda cols: pl.BlockSpec((rows, cols), lambda b, t: (b * nt + t, 0))
    outs = [jax.ShapeDtypeStruct((n, w), F32)] * 9 + [jax.ShapeDtypeStruct((n, 2 * D_MODEL), F32)]
    return pl.pallas_call(
        functools.partial(_inproj_kernel, rows=rows),
        grid=(bsz, nt),
        in_specs=[row_spec(D_MODEL),
                  pl.BlockSpec((None, 6, D_MODEL), lambda b, t: (b, 0, 0)),
                  _const_spec((1, D_MODEL)),
                  _const_spec((D_MODEL, IN_COLS)),
                  _const_spec((1, SHIFT_COLS)),
                  _const_spec((1, w)),
                  _const_spec((LORA_W + LORA_A, 2 * w)),
                  _const_spec((1, w)),
                  _const_spec((LORA_G, w)),
                  _const_spec((1, w)), _const_spec((1, w)), _const_spec((1, w)),
                  _const_spec((MXU_TILE, MXU_TILE))],
        out_specs=[row_spec(w)] * 9 + [row_spec(2 * D_MODEL)],
        out_shape=outs,
        scratch_shapes=[pltpu.VMEM((1, SHIFT_COLS), F32)],
        compiler_params=_params("arbitrary", "arbitrary"),
        name="inproj",
    )(x2, mod3, g1, win, mu, w0, waup, a0, gup, k_k, k_a, r_k, g64)


def _stack_heads(z):
    lane = lax.broadcasted_iota(jnp.int32, z.shape, 1)
    first = lane < RWKV_HEAD
    return jnp.concatenate([jnp.where(first, z, 0.0), jnp.where(first, 0.0, z)], axis=0)


def _cumsum_rows(x):
    n = x.shape[0]
    row = lax.broadcasted_iota(jnp.int32, x.shape, 0)
    d = 1
    while d < n:
        if d < SUBLANES:
            shifted = jnp.where(row >= d, pltpu.roll(x, d, axis=0), 0.0)
        else:
            shifted = jnp.concatenate([jnp.zeros((d, x.shape[1]), x.dtype), x[:n - d]], axis=0)
        x = x + shifted
        d *= 2
    return x


_WKV_STAGE_BOUNDARIES = 6


def _spread_matrix():
    src = jnp.arange(LANES)
    dst = jnp.arange((SUBLANES - 1) * LANES)
    j, lane = dst // LANES, dst % LANES
    hit = (src[:, None] // SUBLANES == lane[None, :] // SUBLANES) & (src[:, None] % SUBLANES == j[None, :])
    return hit.astype(BF16)


def _wkv_terms(chains, spread, side=()):
    t = WKV_CHUNK
    n = STACK
    jobs = list(side)
    calls = [0]

    def between():
        k = calls[0]
        calls[0] += 1
        for _ in range((k + 1) * len(side) // _WKV_STAGE_BOUNDARIES - k * len(side) // _WKV_STAGE_BOUNDARIES):
            jobs.pop(0)()

    ri = lax.broadcasted_iota(jnp.int32, (n, n), 0)
    ci = lax.broadcasted_iota(jnp.int32, (n, n), 1)
    same_head = (ri // t) == (ci // t)
    strict = same_head & (ri > ci)
    incl = same_head & (ri >= ci)

    a_s, r_s, v_s, lhs, rhs, bk, w_end = [], [], [], [], [], [], []
    for r, k, v, ld, kn, b, cl in chains:
        cl_end = cl[t - 1:t, :]
        e_incl = jnp.exp(cl)
        e_excl = jnp.exp(cl - ld)
        e_neg = jnp.exp(-cl)
        e_end = jnp.exp(cl_end - cl)
        a_s.append(_stack_heads(-kn * e_excl))
        r_s.append(_stack_heads(r * e_incl))
        v_s.append(_stack_heads(v))
        lhs.append(jnp.concatenate([-kn * e_excl, r * e_incl], axis=0).astype(BF16))
        rhs.append(jnp.concatenate([_stack_heads(b * e_neg), _stack_heads(k * e_neg)], axis=0).astype(BF16))
        bk.append(jnp.concatenate([_stack_heads(b * e_end), _stack_heads(k * e_end)], axis=0).astype(BF16))
        w_end.append(jnp.exp(cl_end))

    amat = [lax.dot_general(l, rr, (((1,), (1,)), ((), ())), preferred_element_type=F32)
            for l, rr in zip(lhs, rhs)]
    dup = lambda m: jnp.concatenate([m, m], axis=0)
    a_ab = [jnp.where(strict, dup(a[:t, :n]), 0.0) for a in amat]
    a_ak = [jnp.where(strict, dup(a[:t, n:]), 0.0).astype(BF16) for a in amat]
    a_r = [jnp.concatenate([jnp.where(incl, dup(a[t:, :n]), 0.0), jnp.where(incl, dup(a[t:, n:]), 0.0)],
                           axis=1).astype(BF16) for a in amat]

    base = SUBLANES
    blk = (ri // base) == (ci // base)
    packed = []
    for a in a_ab:
        d = jnp.where(blk, a, 0.0)
        acc = d[0:base]
        for q in range(1, n // base):
            acc = acc + d[q * base:(q + 1) * base]
        packed.append(acc)
    cols = jnp.dot(jnp.concatenate(packed, axis=0).astype(BF16), spread,
                   preferred_element_type=F32)
    sub = lax.broadcasted_iota(jnp.int32, (base, LANES), 0)
    lane = lax.broadcasted_iota(jnp.int32, (base, LANES), 1)
    eye_packed = jnp.where(sub == lane % base, 1.0, 0.0)
    x = []
    for i in range(len(a_ab)):
        xp = eye_packed
        for j in range(base - 1):
            lj = cols[i * base:(i + 1) * base, j * LANES:(j + 1) * LANES]
            xp = xp + lj * jnp.broadcast_to(xp[j:j + 1, :], (base, LANES))
        x.append(jnp.where(blk, jnp.concatenate([xp] * (n // base), axis=0), 0.0))
    between()
    s = base
    while s < t:
        level = ((ri // (2 * s)) == (ci // (2 * s))) & ((ri // s) % 2 == 1) & ((ci // s) % 2 == 0)
        c = [jnp.where(level, a, 0.0).astype(BF16) for a in a_ab]
        xb = [xi.astype(BF16) for xi in x]
        odd = lambda m: jnp.concatenate([m[q * s:(q + 1) * s] for q in range(1, n // s, 2)], axis=0)
        xo = [odd(xi) for xi in x]
        xc = [jnp.dot(xoi.astype(BF16), cc, preferred_element_type=F32).astype(BF16)
              for xoi, cc in zip(xo, c)]
        xo = [xoi + jnp.dot(xci, xbi, preferred_element_type=F32) for xoi, xci, xbi in zip(xo, xc, xb)]
        x = [jnp.concatenate([xoi[(q // 2) * s:(q // 2 + 1) * s] if q % 2 else xi[q * s:(q + 1) * s]
                              for q in range(n // s)], axis=0) for xi, xoi in zip(x, xo)]
        between()
        s *= 2

    akv = [jnp.dot(a, v.astype(BF16), preferred_element_type=F32) for a, v in zip(a_ak, v_s)]
    tg = [_dot(xi, jnp.concatenate([a, kv], axis=1)) for xi, a, kv in zip(x, a_s, akv)]
    between()
    low = [jnp.concatenate([g, jnp.concatenate([jnp.zeros_like(v), v], axis=1)], axis=0).astype(BF16)
           for g, v in zip(tg, v_s)]
    out1 = [jnp.dot(a, lo, preferred_element_type=F32) for a, lo in zip(a_r, low)]
    out2 = [lax.dot_general(bb, lo, (((0,), (0,)), ((), ())), preferred_element_type=F32)
            for bb, lo in zip(bk, low)]
    between()
    ki = lax.broadcasted_iota(jnp.int32, (LANES, LANES), 0)
    kj = lax.broadcasted_iota(jnp.int32, (LANES, LANES), 1)
    terms = []
    for rs, o1, o2, we in zip(r_s, out1, out2, w_end):
        r_hat = rs + o1[:, :LANES]
        m = o2[:, :LANES] + jnp.where(ki == kj, we, 0.0)
        terms.append((r_hat, o1[:, LANES:], m, o2[:, LANES:]))
    return terms


def _wkv_kernel(r_ref, k_ref, v_ref, ld_ref, kn_ref, b_ref, spread_ref, y_ref, s_ref, mr_ref, nc_ref,
                yv_ref, *, rows, group):
    t = WKV_CHUNK
    npair = RWKV_WIDTH // LANES

    @pl.when(pl.program_id(1) == 0)
    def _():
        s_ref[...] = jnp.zeros_like(s_ref)

    def chain_step(c):
        rs = pl.ds(_aligned(c * STACK, STACK), STACK)
        outs = [jnp.dot(mr_ref[p, pl.ds(_aligned(c * 2 * STACK, 2 * STACK), 2 * STACK), :],
                        s_ref[p].astype(BF16), preferred_element_type=F32) for p in range(npair)]
        for p, out in enumerate(outs):
            s_ref[p] = out[:LANES] + nc_ref[p, rs, :]
            ys = out[LANES:] + yv_ref[p, rs, :]
            y_ref[pl.ds(_aligned(c * t, t), t), p * LANES:(p + 1) * LANES] = ys[:t] + ys[t:]

    def precompute(g, side):
        chains, where = [], []
        for j in range(group):
            c = g * group + j
            sl = pl.ds(_aligned(c * t, t), t)
            tiles = [ref[sl, :] for ref in (r_ref, k_ref, v_ref, ld_ref, kn_ref, b_ref)]
            cl = _cumsum_rows(tiles[3])
            for p in range(npair):
                ls = slice(p * LANES, (p + 1) * LANES)
                chains.append(tuple(a[:, ls] for a in tiles) + (cl[:, ls],))
                where.append((c, p))
        for (c, p), (r_hat, y_v, m, n_c) in zip(where, _wkv_terms(chains, spread_ref[...], side)):
            mr_ref[p, pl.ds(_aligned(c * 2 * STACK, 2 * STACK), 2 * STACK), :] = (
                jnp.concatenate([m, r_hat], axis=0).astype(BF16))
            rs = pl.ds(_aligned(c * STACK, STACK), STACK)
            nc_ref[p, rs, :] = n_c
            yv_ref[p, rs, :] = y_v

    ngroup = rows // (t * group)
    precompute(0, ())

    def body(g, _):
        precompute(g, [functools.partial(chain_step, (g - 1) * group + j) for j in range(group)])
        return 0

    lax.fori_loop(1, ngroup, body, 0)
    for j in range(group):
        chain_step((ngroup - 1) * group + j)


def _wkv(r, k, v, ld, kn, b, *, bsz, seq, rows):
    n = bsz * seq
    nt = seq // rows
    npair = RWKV_WIDTH // LANES
    nchunk = rows // WKV_CHUNK
    spec = pl.BlockSpec((rows, RWKV_WIDTH), lambda bi, ti: (bi * nt + ti, 0))
    return pl.pallas_call(
        functools.partial(_wkv_kernel, rows=rows, group=4),
        grid=(bsz, nt),
        in_specs=[spec] * 6 + [_const_spec((LANES, (SUBLANES - 1) * LANES))],
        out_specs=spec,
        out_shape=jax.ShapeDtypeStruct((n, RWKV_WIDTH), F32),
        scratch_shapes=[pltpu.VMEM((npair, LANES, LANES), F32),
                        pltpu.VMEM((npair, nchunk * 2 * STACK, LANES), BF16),
                        pltpu.VMEM((npair, nchunk * STACK, LANES), F32),
                        pltpu.VMEM((npair, nchunk * STACK, LANES), F32)],
        compiler_params=_params("arbitrary", "arbitrary"),
        name="wkv",
    )(r, k, v, ld, kn, b, _spread_matrix())


def _gelu_tanh(x):
    return 0.5 * x * (1.0 + jnp.tanh(math.sqrt(2.0 / math.pi) * (x + 0.044715 * x * x * x)))


def _s5_kernel(u_ref, wre_ref, wim_ref, cc_ref, ar_ref, ai_ref, d_ref, y_ref,
               xre, xim, st_re, st_im, *, steps):
    @pl.when(pl.program_id(1) == 0)
    def _():
        st_re[...] = jnp.zeros_like(st_re)
        st_im[...] = jnp.zeros_like(st_im)

    rows = steps * SUBLANES
    ns = S5_SLAB_STATE
    half = S5_GROUPS * S5_STATE // 2
    u = [jnp.swapaxes(u_ref[g * SUBLANES:(g + 1) * SUBLANES, :, :], 0, 1).reshape(rows, S5_WIDTH)
         for g in range(S5_TILES)]
    ub = [a.astype(BF16) for a in u]

    def in_proj_jobs(g):
        def job(j, ref, w_ref):
            ref[g, :, j * ns:(j + 1) * ns] = jnp.dot(ub[g][:, j * LANES:(j + 1) * LANES], w_ref[j],
                                                     preferred_element_type=F32)
        return [functools.partial(job, j, ref, w_ref)
                for j in range(S5_SLABS) for ref, w_ref in ((xre, wre_ref), (xim, wim_ref))]

    def out_proj_jobs(g):
        def job(j):
            xs = jnp.concatenate([xre[g, :, j * ns:(j + 1) * ns], xim[g, :, j * ns:(j + 1) * ns]], axis=1)
            ls = slice(j * LANES, (j + 1) * LANES)
            yj = (jnp.dot(xs.astype(BF16), cc_ref[j], preferred_element_type=F32)
                  + d_ref[:, ls] * u[g][:, ls])
            y_ref[g * SUBLANES:(g + 1) * SUBLANES, :, ls] = jnp.swapaxes(
                _gelu_tanh(yj).reshape(steps, SUBLANES, LANES), 0, 1).astype(y_ref.dtype)
        return [functools.partial(job, j) for j in range(S5_SLABS)]

    def scan(g, jobs):
        total = 2 * steps
        done = 0
        for c in range(2):
            cs = slice(c * half, (c + 1) * half)
            ar = jnp.broadcast_to(ar_ref[:, cs], (SUBLANES, half))
            ai = jnp.broadcast_to(ai_ref[:, cs], (SUBLANES, half))
            x_re, x_im = st_re[g, :, cs], st_im[g, :, cs]
            for i in range(steps):
                k = c * steps + i
                for _ in range((k + 1) * len(jobs) // total - done):
                    jobs[done]()
                    done += 1
                rs = slice(i * SUBLANES, (i + 1) * SUBLANES)
                n_re = ar * x_re - ai * x_im + xre[g, rs, cs]
                n_im = ar * x_im + ai * x_re + xim[g, rs, cs]
                xre[g, rs, cs] = n_re
                xim[g, rs, cs] = n_im
                x_re, x_im = n_re, n_im
            st_re[g, :, cs] = x_re
            st_im[g, :, cs] = x_im

    for job in in_proj_jobs(0):
        job()
    scan(0, in_proj_jobs(1))
    scan(1, out_proj_jobs(0))
    for job in out_proj_jobs(1):
        job()


def _s5(u3, wre, wim, cc, ar, ai, d, *, bsz, seq, steps):
    nstate = S5_GROUPS * S5_STATE
    rows = steps * SUBLANES
    nb = S5_TILES * SUBLANES
    spec = pl.BlockSpec((nb, steps, S5_WIDTH), lambda g, t: (g, t, 0))
    return pl.pallas_call(
        functools.partial(_s5_kernel, steps=steps),
        grid=(bsz // nb, seq // steps),
        in_specs=[spec,
                  _const_spec((S5_SLABS, LANES, S5_SLAB_STATE)),
                  _const_spec((S5_SLABS, LANES, S5_SLAB_STATE)),
                  _const_spec((S5_SLABS, 2 * S5_SLAB_STATE, LANES)),
                  _const_spec((1, nstate)), _const_spec((1, nstate)),
                  _const_spec((1, S5_WIDTH))],
        out_specs=spec,
        out_shape=jax.ShapeDtypeStruct((bsz, seq, S5_WIDTH), BF16),
        scratch_shapes=[pltpu.VMEM((S5_TILES, rows, nstate), F32), pltpu.VMEM((S5_TILES, rows, nstate), F32),
                        pltpu.VMEM((S5_TILES, SUBLANES, nstate), F32),
                        pltpu.VMEM((S5_TILES, SUBLANES, nstate), F32)],
        compiler_params=_params("arbitrary", "arbitrary"),
        name="s5",
    )(u3, wre, wim, cc, ar, ai, d)


def _merge_kernel(y_ref, g_ref, bg_ref, ys5_ref, gates_ref, x_ref, mod_ref, lng_ref, lnb_ref,
                  gmean_ref, wrw_ref, wglu_ref, wout_ref, o_ref):
    gm = gmean_ref[...]
    y = y_ref[...]
    mean = _head_sum(y, gm, split=True)
    d = y - mean
    var = _head_sum(d * d, gm)
    yn = d * lax.rsqrt(var + GN_EPS) * lng_ref[...] + lnb_ref[...]
    ya = _dot(yn * g_ref[...] + bg_ref[...], wrw_ref[...])
    z = _dot(ys5_ref[...], wglu_ref[...])
    yb = z[:, :D_MODEL] * jax.nn.sigmoid(z[:, D_MODEL:])
    gates = gates_ref[...]
    mixed = _dot(gates[:, :D_MODEL] * ya + gates[:, D_MODEL:] * yb, wout_ref[...])
    gt1 = mod_ref[...][2:3]
    o_ref[...] = x_ref[...] + gt1 * mixed


def _merge(y, g, bg, ys5, gates, x2, mod3, lng, lnb, gmean, wrw, wglu, wout, *, bsz, seq, rows):
    n = bsz * seq
    nt = seq // rows
    w = RWKV_WIDTH
    row_spec = lambda cols: pl.BlockSpec((rows, cols), lambda i: (i, 0))
    return pl.pallas_call(
        _merge_kernel,
        grid=(n // rows,),
        in_specs=[row_spec(w), row_spec(w), row_spec(w), row_spec(w), row_spec(2 * D_MODEL),
                  row_spec(D_MODEL),
                  pl.BlockSpec((None, 6, D_MODEL), lambda i: (i // nt, 0, 0)),
                  _const_spec((1, w)), _const_spec((1, w)), _const_spec((MXU_TILE, MXU_TILE)),
                  _const_spec((w, D_MODEL)), _const_spec((S5_WIDTH, 2 * D_MODEL)),
                  _const_spec((D_MODEL, D_MODEL))],
        out_specs=row_spec(D_MODEL),
        out_shape=jax.ShapeDtypeStruct((n, D_MODEL), F32),
        compiler_params=_params("arbitrary"),
        name="merge",
    )(y, g, bg, ys5, gates, x2, mod3, lng, lnb, gmean, wrw, wglu, wout)


def _ffn_kernel(x_ref, mod_ref, g2_ref, wup_ref, cw_ref, cb_ref, wdn_ref, gf_ref, o_ref, buf, *, rows):
    pad = SUBLANES

    @pl.when(pl.program_id(1) == 0)
    def _():
        buf[:, 0:pad, :] = jnp.zeros((FFN_SPLIT, pad, 2 * FFN_CH), F32)

    mod = mod_ref[...]
    sh2, sc2, gt2 = mod[3:4], mod[4:5], mod[5:6]

    def normed(xh):
        ms = jnp.mean(xh * xh, axis=-1, keepdims=True)
        h = xh * lax.rsqrt(ms + RMS_EPS) * g2_ref[...]
        return (h * (1.0 + sc2) + sh2).astype(BF16)

    halves = 2
    hr = rows // halves
    x = [x_ref[h * hr:(h + 1) * hr, :] for h in range(halves)]
    hb = [normed(x[0]), None]
    acc = [jnp.zeros((hr, D_MODEL), F32) for _ in range(halves)]
    for c in range(FFN_SPLIT):
        for h in range(halves):
            if hb[h] is None:
                hb[h] = normed(x[h])
            r0 = pad + h * hr
            buf[c, r0:r0 + hr, :] = jnp.dot(hb[h], wup_ref[c], preferred_element_type=F32)
        cw = cw_ref[c]
        for h in range(halves):
            hid = cb_ref[c]
            for j in range(CONV_W):
                off = pad + h * hr - (CONV_W - 1) + j
                hid = hid + cw[j:j + 1, :] * buf[c, off:off + hr, :]
            gate, up = hid[:, :FFN_CH], hid[:, FFN_CH:]
            act = (0.5 * gate) * (1.0 + jnp.tanh(0.5 * gate)) * up
            acc[h] = acc[h] + jnp.dot(act.astype(BF16), wdn_ref[c], preferred_element_type=F32)
        buf[c, pad - (CONV_W - 1):pad, :] = buf[c, pad + rows - (CONV_W - 1):pad + rows, :]

    for h in range(halves):
        x2 = x[h] + gt2 * acc[h]
        ms2 = jnp.mean(x2 * x2, axis=-1, keepdims=True)
        o_ref[h * hr:(h + 1) * hr, :] = x2 * lax.rsqrt(ms2 + RMS_EPS) * gf_ref[...]


def _ffn(x1, mod3, g2, wup, cw, cb, wdn, gf, *, bsz, seq, rows):
    n = bsz * seq
    nt = seq // rows
    row_spec = pl.BlockSpec((rows, D_MODEL), lambda b, t: (b * nt + t, 0))
    return pl.pallas_call(
        functools.partial(_ffn_kernel, rows=rows),
        grid=(bsz, nt),
        in_specs=[row_spec,
                  pl.BlockSpec((None, 6, D_MODEL), lambda b, t: (b, 0, 0)),
                  _const_spec((1, D_MODEL)),
                  _const_spec((FFN_SPLIT, D_MODEL, 2 * FFN_CH)),
                  _const_spec((FFN_SPLIT, CONV_W, 2 * FFN_CH)),
                  _const_spec((FFN_SPLIT, 1, 2 * FFN_CH)),
                  _const_spec((FFN_SPLIT, FFN_CH, D_MODEL)),
                  _const_spec((1, D_MODEL))],
        out_specs=row_spec,
        out_shape=jax.ShapeDtypeStruct((n, D_MODEL), F32),
        scratch_shapes=[pltpu.VMEM((FFN_SPLIT, rows + SUBLANES, 2 * FFN_CH), F32)],
        compiler_params=_params("arbitrary", "arbitrary"),
        name="ffn",
    )(x1, mod3, g2, wup, cw, cb, wdn, gf)


def _split_cols(a):
    parts = [jnp.concatenate([a[..., c * FFN_CH:(c + 1) * FFN_CH],
                              a[..., D_FF + c * FFN_CH:D_FF + (c + 1) * FFN_CH]], axis=-1)
             for c in range(FFN_SPLIT)]
    return jnp.stack(parts, axis=0)


def _block_diag_ones(n, blk, value, dtype):
    i = jnp.arange(n) // blk
    return jnp.where(i[:, None] == i[None, :], value, 0.0).astype(dtype)


def kernel(x, c, w_ada, b_ada, norm1_g, w_in, mu_shift, rwkv_w0, rwkv_w_up, rwkv_a0, rwkv_a_up,
           rwkv_g_up, rwkv_k_k, rwkv_k_a, rwkv_r_k, rwkv_ln_g, rwkv_ln_b, w_out_rwkv, s5_a_re,
           s5_a_im, s5_log_dt, s5_b_re, s5_b_im, s5_c_re, s5_c_im, s5_d, w_glu, w_out, norm2_g,
           w_ffn_up, ffn_conv_w, ffn_conv_b, w_ffn_down, norm_f_g):
    bsz, seq, _ = x.shape
    depth = w_ada.shape[0]
    n = bsz * seq
    w = RWKV_WIDTH
    rows_a = min(256, seq)
    rows_m = min(512, seq)
    rows_f = min(512, seq)
    s5_steps = min(64, seq)

    g64 = _block_diag_ones(MXU_TILE, RWKV_HEAD, 1.0, BF16)
    gmean = _block_diag_ones(MXU_TILE, RWKV_HEAD, 1.0 / RWKV_HEAD, BF16)
    eye8 = jnp.eye(LANES // S5_GROUP, dtype=F32)
    row = lambda a: a.reshape(1, -1)

    assert depth == 1, depth
    x2 = x.reshape(n, D_MODEL)
    for l in range(depth):
        mod3 = _adaln(c, w_ada[l], b_ada[l]).reshape(bsz, 6, D_MODEL)

        zero = jnp.zeros((LORA_W, w), F32)
        waup = jnp.concatenate([jnp.concatenate([rwkv_w_up[l], zero], axis=1),
                                jnp.concatenate([zero, rwkv_a_up[l]], axis=1)], axis=0).astype(BF16)
        r, k, v, ld, kn, b, g, bg, u, gates = _inproj(
            x2, mod3, row(norm1_g[l]), w_in[l].astype(BF16), row(mu_shift[l]), row(rwkv_w0[l]), waup,
            row(rwkv_a0[l]), rwkv_g_up[l].astype(BF16), row(rwkv_k_k[l]), row(rwkv_k_a[l]),
            row(rwkv_r_k[l]), g64, bsz=bsz, seq=seq, rows=rows_a)

        y = _wkv(r, k, v, ld, kn, b, bsz=bsz, seq=seq, rows=min(1024, seq))

        ab_re, ab_im, bb_re, bb_im = _s5prep(s5_a_re[l], s5_a_im[l], s5_log_dt[l], s5_b_re[l], s5_b_im[l])
        gl = LANES // S5_GROUP
        slab_in = lambda bb: jnp.einsum('jgcp,gh->jgchp', bb.reshape(S5_SLABS, gl, S5_GROUP, S5_STATE),
                                        eye8).reshape(S5_SLABS, LANES, S5_SLAB_STATE).astype(BF16)
        slab_out = lambda cc: jnp.einsum('jgcp,gh->jgphc', cc.reshape(S5_SLABS, gl, S5_GROUP, S5_STATE),
                                         eye8).reshape(S5_SLABS, S5_SLAB_STATE, LANES)
        cc = jnp.concatenate([slab_out(s5_c_re[l]), -slab_out(s5_c_im[l])], axis=1).astype(BF16)
        ys5 = _s5(u.reshape(bsz, seq, S5_WIDTH), slab_in(bb_re), slab_in(bb_im), cc, row(ab_re),
                  row(ab_im), row(s5_d[l]), bsz=bsz, seq=seq, steps=s5_steps).reshape(n, S5_WIDTH)

        x1 = _merge(y, g, bg, ys5, gates, x2, mod3, row(rwkv_ln_g[l]), row(rwkv_ln_b[l]), gmean,
                    w_out_rwkv[l].astype(BF16), w_glu[l].astype(BF16), w_out[l].astype(BF16),
                    bsz=bsz, seq=seq, rows=rows_m)

        gf = row(norm_f_g)
        wdn = w_ffn_down[l].astype(BF16).reshape(FFN_SPLIT, FFN_CH, D_MODEL)
        x2 = _ffn(x1, mod3, row(norm2_g[l]), _split_cols(w_ffn_up[l]).astype(BF16),
                  _split_cols(ffn_conv_w[l]), _split_cols(ffn_conv_b[l].reshape(1, -1)), wdn,
                  gf, bsz=bsz, seq=seq, rows=rows_f)
    return x2.reshape(bsz, seq, D_MODEL)
```

```python
import functools
import math

import jax
import jax.numpy as jnp
from jax import lax
from jax.experimental import pallas as pl
from jax.experimental.pallas import tpu as pltpu

F32 = jnp.float32
BF16 = jnp.bfloat16
HIGHEST = lax.Precision.HIGHEST

D_MODEL = 1024
RWKV_WIDTH = 512
RWKV_HEAD = 64
LORA_W = 64
LORA_A = 64
LORA_G = 128
S5_WIDTH = 512
S5_GROUP = 16
S5_GROUPS = 32
S5_STATE = 64
D_FF = 2816
CONV_W = 3
RMS_EPS = 1e-6
GN_EPS = 64e-5
L2_EPS = 1e-12
SHIFT_COLS = 3 * RWKV_WIDTH + LORA_W + LORA_A + LORA_G
IN_COLS = SHIFT_COLS + S5_WIDTH + 2 * D_MODEL

LANES = 128
SUBLANES = 8
MXU_TILE = 256
WKV_CHUNK = 64
HEADS_PER_TILE = LANES // RWKV_HEAD
STACK = HEADS_PER_TILE * WKV_CHUNK
S5_SLABS = S5_WIDTH // LANES
S5_SLAB_STATE = (LANES // S5_GROUP) * S5_STATE
S5_TILES = 4
FFN_SPLIT = 2
FFN_CH = D_FF // FFN_SPLIT
VMEM_LIMIT = 56 * 1024 * 1024


def _dot(a, b):
    return jnp.dot(a.astype(BF16), b.astype(BF16), preferred_element_type=F32)


def _dot_split(a, g):
    hi = a.astype(BF16)
    lo = (a - hi.astype(F32)).astype(BF16)
    return (jnp.dot(hi, g, preferred_element_type=F32)
            + jnp.dot(lo, g, preferred_element_type=F32))


def _head_sum(a, g, split=False):
    f = _dot_split if split else _dot
    return jnp.concatenate([f(a[:, i:i + MXU_TILE], g) for i in range(0, a.shape[1], MXU_TILE)], axis=1)


def _aligned(i, m):
    return i if isinstance(i, int) else pl.multiple_of(i, m)


def _const_spec(shape):
    nd = len(shape)
    return pl.BlockSpec(shape, lambda *_: (0,) * nd, pipeline_mode=pl.Buffered(1))


def _params(*sem):
    return pltpu.CompilerParams(dimension_semantics=sem, vmem_limit_bytes=VMEM_LIMIT)


def _adaln_kernel(c_ref, w_ref, b_ref, o_ref):
    c = c_ref[...]
    s = c * jax.nn.sigmoid(c)
    o_ref[...] = jnp.dot(s, w_ref[...], preferred_element_type=F32, precision=HIGHEST) + b_ref[...]


def _adaln(c, w_ada, b_ada):
    bsz = c.shape[0]
    n = w_ada.shape[1]
    bn = D_MODEL
    return pl.pallas_call(
        _adaln_kernel,
        grid=(n // bn,),
        in_specs=[pl.BlockSpec((bsz, D_MODEL), lambda j: (0, 0)),
                  pl.BlockSpec((D_MODEL, bn), lambda j: (0, j)),
                  pl.BlockSpec((1, bn), lambda j: (0, j))],
        out_specs=pl.BlockSpec((bsz, bn), lambda j: (0, j)),
        out_shape=jax.ShapeDtypeStruct((bsz, n), F32),
        compiler_params=_params("arbitrary"),
        name="adaln",
    )(c, w_ada, b_ada.reshape(1, n))


def _s5prep_kernel(are_ref, aim_ref, ldt_ref, bre_ref, bim_ref, abre_o, abim_o, bbre_o, bbim_o):
    a_re = are_ref[...]
    a_im = aim_ref[...]
    dt = jnp.exp(ldt_ref[...])
    z_re, z_im = a_re * dt, a_im * dt
    mag = jnp.exp(z_re)
    ab_re, ab_im = mag * jnp.cos(z_im), mag * jnp.sin(z_im)
    den = a_re * a_re + a_im * a_im
    q_re = ((ab_re - 1.0) * a_re + ab_im * a_im) / den
    q_im = (ab_im * a_re - (ab_re - 1.0) * a_im) / den
    abre_o[...] = ab_re
    abim_o[...] = ab_im
    b_re = bre_ref[...]
    b_im = bim_ref[...]
    bbre_o[...] = q_re[:, None, :] * b_re - q_im[:, None, :] * b_im
    bbim_o[...] = q_re[:, None, :] * b_im + q_im[:, None, :] * b_re


def _s5prep(a_re, a_im, log_dt, b_re, b_im):
    g, p = a_re.shape
    c = b_re.shape[-1]
    b_re_t = jnp.transpose(b_re, (0, 2, 1))
    b_im_t = jnp.transpose(b_im, (0, 2, 1))
    return pl.pallas_call(
        _s5prep_kernel,
        out_shape=(jax.ShapeDtypeStruct((g, p), F32), jax.ShapeDtypeStruct((g, p), F32),
                   jax.ShapeDtypeStruct((g, c, p), F32), jax.ShapeDtypeStruct((g, c, p), F32)),
        name="s5prep",
    )(a_re, a_im, log_dt.reshape(g, 1), b_re_t, b_im_t)


def _inproj_kernel(x_ref, mod_ref, g1_ref, win_ref, mu_ref, w0_ref, waup_ref, a0_ref,
                   kk_ref, ka_ref, rk_ref, g64_ref,
                   r_o, k_o, v_o, ld_o, kn_o, b_o, gd_o, bonus_o, u_o, gates_o, carry, *, rows):
    @pl.when(pl.program_id(1) == 0)
    def _():
        carry[...] = jnp.zeros_like(carry)

    x = x_ref[...]
    mod = mod_ref[...]
    sh1, sc1 = mod[0:1], mod[1:2]
    ms = jnp.mean(x * x, axis=-1, keepdims=True)
    h = x * lax.rsqrt(ms + RMS_EPS) * g1_ref[...]
    h = h * (1.0 + sc1) + sh1
    proj = jnp.dot(h.astype(BF16), win_ref[...], preferred_element_type=F32)

    p = proj[:, :SHIFT_COLS]
    row = lax.broadcasted_iota(jnp.int32, p.shape, 0)
    prev = jnp.where(row == 0, carry[...], pltpu.roll(p, 1, axis=0))
    carry[...] = p[rows - 1:rows, :]
    ps = p + (prev - p) * mu_ref[...]

    w = RWKV_WIDTH
    r, k, v = ps[:, 0:w], ps[:, w:2 * w], ps[:, 2 * w:3 * w]
    wa = ps[:, 3 * w:3 * w + LORA_W + LORA_A]
    lane = lax.broadcasted_iota(jnp.int32, wa.shape, 1)
    wa = jnp.where(lane < LORA_W, jnp.tanh(wa), wa)
    delta = _dot(wa, waup_ref[...])
    w_raw = w0_ref[...] + delta[:, :w]
    ld = -math.exp(-0.5) * jax.nn.sigmoid(w_raw)
    eta = jax.nn.sigmoid(a0_ref[...] + delta[:, w:])
    gd = ps[:, 3 * w + LORA_W + LORA_A:SHIFT_COLS]

    g64 = g64_ref[...]
    kk = k * kk_ref[...]
    kn = kk * lax.rsqrt(_head_sum(kk * kk, g64) + L2_EPS)
    k2 = k * (1.0 + (eta - 1.0) * ka_ref[...])
    bonus = _head_sum(r * k2 * rk_ref[...], g64) * v

    r_o[...] = r
    k_o[...] = k2
    v_o[...] = v
    ld_o[...] = ld
    kn_o[...] = kn
    b_o[...] = kn * eta
    gd_o[...] = gd
    bonus_o[...] = bonus
    u_o[...] = proj[:, SHIFT_COLS:SHIFT_COLS + S5_WIDTH]
    gates_o[...] = jax.nn.sigmoid(proj[:, SHIFT_COLS + S5_WIDTH:])


def _inproj(x2, mod3, g1, win, mu, w0, waup, a0, k_k, k_a, r_k, g64, *, bsz, seq, rows):
    n = bsz * seq
    nt = seq // rows
    w = RWKV_WIDTH
    row_spec = lambda cols: pl.BlockSpec((rows, cols), lambda b, t: (b * nt + t, 0))
    widths = [w] * 6 + [LORA_G, w, w, 2 * D_MODEL]
    outs = [jax.ShapeDtypeStruct((n, c), F32) for c in widths]
    return pl.pallas_call(
        functools.partial(_inproj_kernel, rows=rows),
        grid=(bsz, nt),
        in_specs=[row_spec(D_MODEL),
                  pl.BlockSpec((None, 6, D_MODEL), lambda b, t: (b, 0, 0)),
                  _const_spec((1, D_MODEL)),
                  _const_spec((D_MODEL, IN_COLS)),
                  _const_spec((1, SHIFT_COLS)),
                  _const_spec((1, w)),
                  _const_spec((LORA_W + LORA_A, 2 * w)),
                  _const_spec((1, w)),
                  _const_spec((1, w)), _const_spec((1, w)), _const_spec((1, w)),
                  _const_spec((MXU_TILE, MXU_TILE))],
        out_specs=[row_spec(c) for c in widths],
        out_shape=outs,
        scratch_shapes=[pltpu.VMEM((1, SHIFT_COLS), F32)],
        compiler_params=_params("arbitrary", "arbitrary"),
        name="inproj",
    )(x2, mod3, g1, win, mu, w0, waup, a0, k_k, k_a, r_k, g64)


def _stack_heads(z):
    lane = lax.broadcasted_iota(jnp.int32, z.shape, 1)
    first = lane < RWKV_HEAD
    return jnp.concatenate([jnp.where(first, z, 0.0), jnp.where(first, 0.0, z)], axis=0)


def _cumsum_rows(x):
    n = x.shape[0]
    row = lax.broadcasted_iota(jnp.int32, x.shape, 0)
    d = 1
    while d < n:
        if d < SUBLANES:
            shifted = jnp.where(row >= d, pltpu.roll(x, d, axis=0), 0.0)
        else:
            shifted = jnp.concatenate([jnp.zeros((d, x.shape[1]), x.dtype), x[:n - d]], axis=0)
        x = x + shifted
        d *= 2
    return x


_WKV_STAGE_BOUNDARIES = 6


def _spread_matrix():
    src = jnp.arange(LANES)
    dst = jnp.arange((SUBLANES - 1) * LANES)
    j, lane = dst // LANES, dst % LANES
    hit = (src[:, None] // SUBLANES == lane[None, :] // SUBLANES) & (src[:, None] % SUBLANES == j[None, :])
    return hit.astype(BF16)


def _wkv_terms(chains, spread, side=()):
    t = WKV_CHUNK
    n = STACK
    jobs = list(side)
    calls = [0]

    def between():
        k = calls[0]
        calls[0] += 1
        for _ in range((k + 1) * len(side) // _WKV_STAGE_BOUNDARIES - k * len(side) // _WKV_STAGE_BOUNDARIES):
            jobs.pop(0)()

    ri = lax.broadcasted_iota(jnp.int32, (n, n), 0)
    ci = lax.broadcasted_iota(jnp.int32, (n, n), 1)
    same_head = (ri // t) == (ci // t)
    strict = same_head & (ri > ci)
    incl = same_head & (ri >= ci)

    a_s, r_s, v_s, lhs, rhs, bk, w_end = [], [], [], [], [], [], []
    for r, k, v, ld, kn, b, cl in chains:
        cl_end = cl[t - 1:t, :]
        e_incl = jnp.exp(cl)
        e_excl = jnp.exp(cl - ld)
        e_neg = jnp.exp(-cl)
        e_end = jnp.exp(cl_end - cl)
        a_s.append(_stack_heads(-kn * e_excl))
        r_s.append(_stack_heads(r * e_incl))
        v_s.append(_stack_heads(v))
        lhs.append(jnp.concatenate([-kn * e_excl, r * e_incl], axis=0).astype(BF16))
        rhs.append(jnp.concatenate([_stack_heads(b * e_neg), _stack_heads(k * e_neg)], axis=0).astype(BF16))
        bk.append(jnp.concatenate([_stack_heads(b * e_end), _stack_heads(k * e_end)], axis=0).astype(BF16))
        w_end.append(jnp.exp(cl_end))

    amat = [lax.dot_general(l, rr, (((1,), (1,)), ((), ())), preferred_element_type=F32)
            for l, rr in zip(lhs, rhs)]
    dup = lambda m: jnp.concatenate([m, m], axis=0)
    a_ab = [jnp.where(strict, dup(a[:t, :n]), 0.0) for a in amat]
    a_ak = [jnp.where(strict, dup(a[:t, n:]), 0.0).astype(BF16) for a in amat]
    a_r = [jnp.concatenate([jnp.where(incl, dup(a[t:, :n]), 0.0), jnp.where(incl, dup(a[t:, n:]), 0.0)],
                           axis=1).astype(BF16) for a in amat]

    base = SUBLANES
    blk = (ri // base) == (ci // base)
    packed = []
    for a in a_ab:
        d = jnp.where(blk, a, 0.0)
        acc = d[0:base]
        for q in range(1, n // base):
            acc = acc + d[q * base:(q + 1) * base]
        packed.append(acc)
    cols = jnp.dot(jnp.concatenate(packed, axis=0).astype(BF16), spread,
                   preferred_element_type=F32)
    sub = lax.broadcasted_iota(jnp.int32, (base, LANES), 0)
    lane = lax.broadcasted_iota(jnp.int32, (base, LANES), 1)
    eye_packed = jnp.where(sub == lane % base, 1.0, 0.0)
    x = []
    for i in range(len(a_ab)):
        xp = eye_packed
        for j in range(base - 1):
            lj = cols[i * base:(i + 1) * base, j * LANES:(j + 1) * LANES]
            xp = xp + lj * jnp.broadcast_to(xp[j:j + 1, :], (base, LANES))
        x.append(jnp.where(blk, jnp.concatenate([xp] * (n // base), axis=0), 0.0))
    between()
    s = base
    while s < t:
        level = ((ri // (2 * s)) == (ci // (2 * s))) & ((ri // s) % 2 == 1) & ((ci // s) % 2 == 0)
        c = [jnp.where(level, a, 0.0).astype(BF16) for a in a_ab]
        xb = [xi.astype(BF16) for xi in x]
        odd = lambda m: jnp.concatenate([m[q * s:(q + 1) * s] for q in range(1, n // s, 2)], axis=0)
        xo = [odd(xi) for xi in x]
        xc = [jnp.dot(xoi.astype(BF16), cc, preferred_element_type=F32).astype(BF16)
              for xoi, cc in zip(xo, c)]
        xo = [xoi + jnp.dot(xci, xbi, preferred_element_type=F32) for xoi, xci, xbi in zip(xo, xc, xb)]
        x = [jnp.concatenate([xoi[(q // 2) * s:(q // 2 + 1) * s] if q % 2 else xi[q * s:(q + 1) * s]
                              for q in range(n // s)], axis=0) for xi, xoi in zip(x, xo)]
        between()
        s *= 2

    akv = [jnp.dot(a, v.astype(BF16), preferred_element_type=F32) for a, v in zip(a_ak, v_s)]
    tg = [_dot(xi, jnp.concatenate([a, kv], axis=1)) for xi, a, kv in zip(x, a_s, akv)]
    between()
    low = [jnp.concatenate([g, jnp.concatenate([jnp.zeros_like(v), v], axis=1)], axis=0).astype(BF16)
           for g, v in zip(tg, v_s)]
    out1 = [jnp.dot(a, lo, preferred_element_type=F32) for a, lo in zip(a_r, low)]
    out2 = [lax.dot_general(bb, lo, (((0,), (0,)), ((), ())), preferred_element_type=F32)
            for bb, lo in zip(bk, low)]
    between()
    ki = lax.broadcasted_iota(jnp.int32, (LANES, LANES), 0)
    kj = lax.broadcasted_iota(jnp.int32, (LANES, LANES), 1)
    terms = []
    for rs, o1, o2, we in zip(r_s, out1, out2, w_end):
        r_hat = rs + o1[:, :LANES]
        m = o2[:, :LANES] + jnp.where(ki == kj, we, 0.0)
        terms.append((r_hat, o1[:, LANES:], m, o2[:, LANES:]))
    return terms


def _wkv_kernel(r_ref, k_ref, v_ref, ld_ref, kn_ref, b_ref, spread_ref, y_ref, s_ref, mr_ref, nc_ref,
                yv_ref, *, rows, groups):
    t = WKV_CHUNK
    npair = RWKV_WIDTH // LANES

    @pl.when(pl.program_id(1) == 0)
    def _():
        s_ref[...] = jnp.zeros_like(s_ref)

    def chain_step(c):
        rs = pl.ds(_aligned(c * STACK, STACK), STACK)
        outs = [jnp.dot(mr_ref[p, pl.ds(_aligned(c * 2 * STACK, 2 * STACK), 2 * STACK), :],
                        s_ref[p].astype(BF16), preferred_element_type=F32) for p in range(npair)]
        for p, out in enumerate(outs):
            s_ref[p] = out[:LANES] + nc_ref[p, rs, :]
            ys = out[LANES:] + yv_ref[p, rs, :]
            y_ref[pl.ds(_aligned(c * t, t), t), p * LANES:(p + 1) * LANES] = ys[:t] + ys[t:]

    def precompute(chunks, side):
        chains, where = [], []
        for c in chunks:
            sl = pl.ds(_aligned(c * t, t), t)
            tiles = [ref[sl, :] for ref in (r_ref, k_ref, v_ref, ld_ref, kn_ref, b_ref)]
            cl = _cumsum_rows(tiles[3])
            for p in range(npair):
                ls = slice(p * LANES, (p + 1) * LANES)
                chains.append(tuple(a[:, ls] for a in tiles) + (cl[:, ls],))
                where.append((c, p))
        for (c, p), (r_hat, y_v, m, n_c) in zip(where, _wkv_terms(chains, spread_ref[...], side)):
            mr_ref[p, pl.ds(_aligned(c * 2 * STACK, 2 * STACK), 2 * STACK), :] = (
                jnp.concatenate([m, r_hat], axis=0).astype(BF16))
            rs = pl.ds(_aligned(c * STACK, STACK), STACK)
            nc_ref[p, rs, :] = n_c
            yv_ref[p, rs, :] = y_v

    assert sum(groups) * t == rows, (groups, rows)
    previous, start = (), 0
    for count in groups:
        chunks = tuple(range(start, start + count))
        precompute(chunks, [functools.partial(chain_step, c) for c in previous])
        previous, start = chunks, start + count
    for c in previous:
        chain_step(c)


WKV_MAX_GROUP = 8
WKV_LAST_GROUP = 2


def _wkv_groups(nchunk):
    if nchunk <= WKV_LAST_GROUP:
        return (nchunk,)
    sizes, left = [], nchunk - WKV_LAST_GROUP
    while left > 0:
        sizes.append(min(WKV_MAX_GROUP, left))
        left -= sizes[-1]
    return tuple(sizes) + (WKV_LAST_GROUP,)


def _wkv(r, k, v, ld, kn, b, *, bsz, seq, rows):
    n = bsz * seq
    nt = seq // rows
    npair = RWKV_WIDTH // LANES
    nchunk = rows // WKV_CHUNK
    spec = pl.BlockSpec((rows, RWKV_WIDTH), lambda bi, ti: (bi * nt + ti, 0))
    return pl.pallas_call(
        functools.partial(_wkv_kernel, rows=rows, groups=_wkv_groups(nchunk)),
        grid=(bsz, nt),
        in_specs=[spec] * 6 + [_const_spec((LANES, (SUBLANES - 1) * LANES))],
        out_specs=spec,
        out_shape=jax.ShapeDtypeStruct((n, RWKV_WIDTH), F32),
        scratch_shapes=[pltpu.VMEM((npair, LANES, LANES), F32),
                        pltpu.VMEM((npair, nchunk * 2 * STACK, LANES), BF16),
                        pltpu.VMEM((npair, nchunk * STACK, LANES), F32),
                        pltpu.VMEM((npair, nchunk * STACK, LANES), F32)],
        compiler_params=_params("arbitrary", "arbitrary"),
        name="wkv",
    )(r, k, v, ld, kn, b, _spread_matrix())


def _gelu_tanh(x):
    return 0.5 * x * (1.0 + jnp.tanh(math.sqrt(2.0 / math.pi) * (x + 0.044715 * x * x * x)))


def _s5_kernel(u_ref, wre_ref, wim_ref, cc_ref, ar_ref, ai_ref, d_ref, y_ref,
               xre, xim, st_re, st_im, *, steps):
    @pl.when(pl.program_id(1) == 0)
    def _():
        st_re[...] = jnp.zeros_like(st_re)
        st_im[...] = jnp.zeros_like(st_im)

    rows = steps * SUBLANES
    ns = S5_SLAB_STATE
    half = S5_GROUPS * S5_STATE // 2
    inputs = {}

    def u_of(g):
        if g not in inputs:
            u = jnp.swapaxes(u_ref[g * SUBLANES:(g + 1) * SUBLANES, :, :], 0, 1).reshape(rows, S5_WIDTH)
            inputs[g] = (u, u.astype(BF16))
        return inputs[g]

    def in_proj_jobs(g):
        def job(j, ref, w_ref):
            ref[g % 2, :, j * ns:(j + 1) * ns] = jnp.dot(u_of(g)[1][:, j * LANES:(j + 1) * LANES], w_ref[j],
                                                         preferred_element_type=F32)
        return [functools.partial(job, j, ref, w_ref)
                for j in range(S5_SLABS) for ref, w_ref in ((xre, wre_ref), (xim, wim_ref))]

    def out_proj_jobs(g):
        def job(j):
            xs = jnp.concatenate([xre[g % 2, :, j * ns:(j + 1) * ns], xim[g % 2, :, j * ns:(j + 1) * ns]],
                                 axis=1)
            ls = slice(j * LANES, (j + 1) * LANES)
            yj = (jnp.dot(xs.astype(BF16), cc_ref[j], preferred_element_type=F32)
                  + d_ref[:, ls] * u_of(g)[0][:, ls])
            y_ref[g * SUBLANES:(g + 1) * SUBLANES, :, ls] = jnp.swapaxes(
                _gelu_tanh(yj).reshape(steps, SUBLANES, LANES), 0, 1).astype(y_ref.dtype)
        return [functools.partial(job, j) for j in range(S5_SLABS)]

    def scan(g, jobs):
        total = 2 * steps
        done = 0
        for c in range(2):
            cs = slice(c * half, (c + 1) * half)
            ar = jnp.broadcast_to(ar_ref[:, cs], (SUBLANES, half))
            ai = jnp.broadcast_to(ai_ref[:, cs], (SUBLANES, half))
            x_re, x_im = st_re[g, :, cs], st_im[g, :, cs]
            for i in range(steps):
                k = c * steps + i
                for _ in range((k + 1) * len(jobs) // total - done):
                    jobs[done]()
                    done += 1
                rs = slice(i * SUBLANES, (i + 1) * SUBLANES)
                n_re = ar * x_re - ai * x_im + xre[g % 2, rs, cs]
                n_im = ar * x_im + ai * x_re + xim[g % 2, rs, cs]
                xre[g % 2, rs, cs] = n_re
                xim[g % 2, rs, cs] = n_im
                x_re, x_im = n_re, n_im
            st_re[g, :, cs] = x_re
            st_im[g, :, cs] = x_im

    for job in in_proj_jobs(0):
        job()
    for g in range(S5_TILES):
        jobs = (out_proj_jobs(g - 1) if g >= 1 else []) + (in_proj_jobs(g + 1) if g + 1 < S5_TILES else [])
        scan(g, jobs)
    for job in out_proj_jobs(S5_TILES - 1):
        job()


def _s5(u3, wre, wim, cc, ar, ai, d, *, bsz, seq, steps):
    nstate = S5_GROUPS * S5_STATE
    rows = steps * SUBLANES
    nb = S5_TILES * SUBLANES
    spec = pl.BlockSpec((nb, steps, S5_WIDTH), lambda g, t: (g, t, 0))
    return pl.pallas_call(
        functools.partial(_s5_kernel, steps=steps),
        grid=(bsz // nb, seq // steps),
        in_specs=[spec,
                  _const_spec((S5_SLABS, LANES, S5_SLAB_STATE)),
                  _const_spec((S5_SLABS, LANES, S5_SLAB_STATE)),
                  _const_spec((S5_SLABS, 2 * S5_SLAB_STATE, LANES)),
                  _const_spec((1, nstate)), _const_spec((1, nstate)),
                  _const_spec((1, S5_WIDTH))],
        out_specs=spec,
        out_shape=jax.ShapeDtypeStruct((bsz, seq, S5_WIDTH), BF16),
        scratch_shapes=[pltpu.VMEM((2, rows, nstate), F32), pltpu.VMEM((2, rows, nstate), F32),
                        pltpu.VMEM((S5_TILES, SUBLANES, nstate), F32),
                        pltpu.VMEM((S5_TILES, SUBLANES, nstate), F32)],
        compiler_params=_params("arbitrary", "arbitrary"),
        name="s5",
    )(u3, wre, wim, cc, ar, ai, d)


def _merge_kernel(y_ref, gd_ref, bonus_ref, ys5_ref, gates_ref, x_ref, mod_ref, lng_ref, lnb_ref,
                  gmean_ref, gup_ref, wrw_ref, wglu_ref, wout_ref, o_ref):
    gm = gmean_ref[...]
    y = y_ref[...]
    mean = _head_sum(y, gm, split=True)
    d = y - mean
    var = _head_sum(d * d, gm)
    yn = d * lax.rsqrt(var + GN_EPS) * lng_ref[...] + lnb_ref[...]
    g = _dot(jax.nn.sigmoid(gd_ref[...]), gup_ref[...])
    ya = _dot((yn + bonus_ref[...]) * g, wrw_ref[...])
    z = _dot(ys5_ref[...], wglu_ref[...])
    yb = z[:, :D_MODEL] * jax.nn.sigmoid(z[:, D_MODEL:])
    gates = gates_ref[...]
    mixed = _dot(gates[:, :D_MODEL] * ya + gates[:, D_MODEL:] * yb, wout_ref[...])
    gt1 = mod_ref[...][2:3]
    o_ref[...] = x_ref[...] + gt1 * mixed


def _merge(y, gd, bonus, ys5, gates, x2, mod3, lng, lnb, gmean, gup, wrw, wglu, wout, *, bsz, seq, rows):
    n = bsz * seq
    nt = seq // rows
    w = RWKV_WIDTH
    row_spec = lambda cols: pl.BlockSpec((rows, cols), lambda i: (i, 0))
    return pl.pallas_call(
        _merge_kernel,
        grid=(n // rows,),
        in_specs=[row_spec(w), row_spec(LORA_G), row_spec(w), row_spec(w), row_spec(2 * D_MODEL),
                  row_spec(D_MODEL),
                  pl.BlockSpec((None, 6, D_MODEL), lambda i: (i // nt, 0, 0)),
                  _const_spec((1, w)), _const_spec((1, w)), _const_spec((MXU_TILE, MXU_TILE)),
                  _const_spec((LORA_G, w)), _const_spec((w, D_MODEL)), _const_spec((S5_WIDTH, 2 * D_MODEL)),
                  _const_spec((D_MODEL, D_MODEL))],
        out_specs=row_spec(D_MODEL),
        out_shape=jax.ShapeDtypeStruct((n, D_MODEL), F32),
        compiler_params=_params("arbitrary"),
        name="merge",
    )(y, gd, bonus, ys5, gates, x2, mod3, lng, lnb, gmean, gup, wrw, wglu, wout)


def _ffn_kernel(x_ref, mod_ref, g2_ref, wup_ref, cw_ref, cb_ref, wdn_ref, gf_ref, o_ref, buf, *, rows):
    pad = SUBLANES

    @pl.when(pl.program_id(1) == 0)
    def _():
        buf[:, 0:pad, :] = jnp.zeros((FFN_SPLIT, pad, 2 * FFN_CH), F32)

    mod = mod_ref[...]
    sh2, sc2, gt2 = mod[3:4], mod[4:5], mod[5:6]

    def normed(xh):
        ms = jnp.mean(xh * xh, axis=-1, keepdims=True)
        h = xh * lax.rsqrt(ms + RMS_EPS) * g2_ref[...]
        return (h * (1.0 + sc2) + sh2).astype(BF16)

    halves = 2
    hr = rows // halves
    x = [x_ref[h * hr:(h + 1) * hr, :] for h in range(halves)]
    hb = [normed(x[0]), None]
    acc = [jnp.zeros((hr, D_MODEL), F32) for _ in range(halves)]
    for c in range(FFN_SPLIT):
        for h in range(halves):
            if hb[h] is None:
                hb[h] = normed(x[h])
            r0 = pad + h * hr
            buf[c, r0:r0 + hr, :] = jnp.dot(hb[h], wup_ref[c], preferred_element_type=F32)
        cw = cw_ref[c]
        for h in range(halves):
            hid = cb_ref[c]
            for j in range(CONV_W):
                off = pad + h * hr - (CONV_W - 1) + j
                hid = hid + cw[j:j + 1, :] * buf[c, off:off + hr, :]
            gate, up = hid[:, :FFN_CH], hid[:, FFN_CH:]
            act = (0.5 * gate) * (1.0 + jnp.tanh(0.5 * gate)) * up
            acc[h] = acc[h] + jnp.dot(act.astype(BF16), wdn_ref[c], preferred_element_type=F32)
        buf[c, pad - (CONV_W - 1):pad, :] = buf[c, pad + rows - (CONV_W - 1):pad + rows, :]

    for h in range(halves):
        x2 = x[h] + gt2 * acc[h]
        ms2 = jnp.mean(x2 * x2, axis=-1, keepdims=True)
        o_ref[h * hr:(h + 1) * hr, :] = x2 * lax.rsqrt(ms2 + RMS_EPS) * gf_ref[...]


def _ffn(x1, mod3, g2, wup, cw, cb, wdn, gf, *, bsz, seq, rows):
    n = bsz * seq
    nt = seq // rows
    row_spec = pl.BlockSpec((rows, D_MODEL), lambda b, t: (b * nt + t, 0))
    return pl.pallas_call(
        functools.partial(_ffn_kernel, rows=rows),
        grid=(bsz, nt),
        in_specs=[row_spec,
                  pl.BlockSpec((None, 6, D_MODEL), lambda b, t: (b, 0, 0)),
                  _const_spec((1, D_MODEL)),
                  _const_spec((FFN_SPLIT, D_MODEL, 2 * FFN_CH)),
                  _const_spec((FFN_SPLIT, CONV_W, 2 * FFN_CH)),
                  _const_spec((FFN_SPLIT, 1, 2 * FFN_CH)),
                  _const_spec((FFN_SPLIT, FFN_CH, D_MODEL)),
                  _const_spec((1, D_MODEL))],
        out_specs=row_spec,
        out_shape=jax.ShapeDtypeStruct((n, D_MODEL), F32),
        scratch_shapes=[pltpu.VMEM((FFN_SPLIT, rows + SUBLANES, 2 * FFN_CH), F32)],
        compiler_params=_params("arbitrary", "arbitrary"),
        name="ffn",
    )(x1, mod3, g2, wup, cw, cb, wdn, gf)


def _split_cols(a):
    parts = [jnp.concatenate([a[..., c * FFN_CH:(c + 1) * FFN_CH],
                              a[..., D_FF + c * FFN_CH:D_FF + (c + 1) * FFN_CH]], axis=-1)
             for c in range(FFN_SPLIT)]
    return jnp.stack(parts, axis=0)


def _block_diag_ones(n, blk, value, dtype):
    i = jnp.arange(n) // blk
    return jnp.where(i[:, None] == i[None, :], value, 0.0).astype(dtype)


def kernel(x, c, w_ada, b_ada, norm1_g, w_in, mu_shift, rwkv_w0, rwkv_w_up, rwkv_a0, rwkv_a_up,
           rwkv_g_up, rwkv_k_k, rwkv_k_a, rwkv_r_k, rwkv_ln_g, rwkv_ln_b, w_out_rwkv, s5_a_re,
           s5_a_im, s5_log_dt, s5_b_re, s5_b_im, s5_c_re, s5_c_im, s5_d, w_glu, w_out, norm2_g,
           w_ffn_up, ffn_conv_w, ffn_conv_b, w_ffn_down, norm_f_g):
    bsz, seq, _ = x.shape
    depth = w_ada.shape[0]
    n = bsz * seq
    w = RWKV_WIDTH
    rows_a = min(256, seq)
    rows_m = min(512, seq)
    rows_f = min(512, seq)
    s5_steps = min(64, seq)

    g64 = _block_diag_ones(MXU_TILE, RWKV_HEAD, 1.0, BF16)
    gmean = _block_diag_ones(MXU_TILE, RWKV_HEAD, 1.0 / RWKV_HEAD, BF16)
    eye8 = jnp.eye(LANES // S5_GROUP, dtype=F32)
    row = lambda a: a.reshape(1, -1)

    assert depth == 1, depth
    x2 = x.reshape(n, D_MODEL)
    for l in range(depth):
        mod3 = _adaln(c, w_ada[l], b_ada[l]).reshape(bsz, 6, D_MODEL)

        zero = jnp.zeros((LORA_W, w), F32)
        waup = jnp.concatenate([jnp.concatenate([rwkv_w_up[l], zero], axis=1),
                                jnp.concatenate([zero, rwkv_a_up[l]], axis=1)], axis=0).astype(BF16)
        r, k, v, ld, kn, b, gd, bonus, u, gates = _inproj(
            x2, mod3, row(norm1_g[l]), w_in[l].astype(BF16), row(mu_shift[l]), row(rwkv_w0[l]), waup,
            row(rwkv_a0[l]), row(rwkv_k_k[l]), row(rwkv_k_a[l]),
            row(rwkv_r_k[l]), g64, bsz=bsz, seq=seq, rows=rows_a)

        y = _wkv(r, k, v, ld, kn, b, bsz=bsz, seq=seq, rows=min(1024, seq))

        ab_re, ab_im, bb_re, bb_im = _s5prep(s5_a_re[l], s5_a_im[l], s5_log_dt[l], s5_b_re[l], s5_b_im[l])
        gl = LANES // S5_GROUP
        slab_in = lambda bb: jnp.einsum('jgcp,gh->jgchp', bb.reshape(S5_SLABS, gl, S5_GROUP, S5_STATE),
                                        eye8).reshape(S5_SLABS, LANES, S5_SLAB_STATE).astype(BF16)
        slab_out = lambda cc: jnp.einsum('jgcp,gh->jgphc', cc.reshape(S5_SLABS, gl, S5_GROUP, S5_STATE),
                                         eye8).reshape(S5_SLABS, S5_SLAB_STATE, LANES)
        cc = jnp.concatenate([slab_out(s5_c_re[l]), -slab_out(s5_c_im[l])], axis=1).astype(BF16)
        ys5 = _s5(u.reshape(bsz, seq, S5_WIDTH), slab_in(bb_re), slab_in(bb_im), cc, row(ab_re),
                  row(ab_im), row(s5_d[l]), bsz=bsz, seq=seq, steps=s5_steps).reshape(n, S5_WIDTH)

        x1 = _merge(y, gd, bonus, ys5, gates, x2, mod3, row(rwkv_ln_g[l]), row(rwkv_ln_b[l]), gmean,
                    rwkv_g_up[l].astype(BF16), w_out_rwkv[l].astype(BF16), w_glu[l].astype(BF16), w_out[l].astype(BF16),
                    bsz=bsz, seq=seq, rows=rows_m)

        gf = row(norm_f_g)
        wdn = w_ffn_down[l].astype(BF16).reshape(FFN_SPLIT, FFN_CH, D_MODEL)
        x2 = _ffn(x1, mod3, row(norm2_g[l]), _split_cols(w_ffn_up[l]).astype(BF16),
                  _split_cols(ffn_conv_w[l]), _split_cols(ffn_conv_b[l].reshape(1, -1)), wdn,
                  gf, bsz=bsz, seq=seq, rows=rows_f)
    return x2.reshape(bsz, seq, D_MODEL)
```

```python
import functools
import math

import jax
import jax.numpy as jnp
from jax import lax
from jax.experimental import pallas as pl
from jax.experimental.pallas import tpu as pltpu

F32 = jnp.float32
BF16 = jnp.bfloat16
HIGHEST = lax.Precision.HIGHEST

D_MODEL = 1024
RWKV_WIDTH = 512
RWKV_HEAD = 64
LORA_W = 64
LORA_A = 64
LORA_G = 128
S5_WIDTH = 512
S5_GROUP = 16
S5_GROUPS = 32
S5_STATE = 64
D_FF = 2816
CONV_W = 3
RMS_EPS = 1e-6
GN_EPS = 64e-5
L2_EPS = 1e-12
SHIFT_COLS = 3 * RWKV_WIDTH + LORA_W + LORA_A + LORA_G
IN_COLS = SHIFT_COLS + S5_WIDTH + 2 * D_MODEL

LANES = 128
SUBLANES = 8
MXU_TILE = 256
WKV_CHUNK = 64
HEADS_PER_TILE = LANES // RWKV_HEAD
STACK = HEADS_PER_TILE * WKV_CHUNK
S5_SLABS = S5_WIDTH // LANES
S5_SLAB_STATE = (LANES // S5_GROUP) * S5_STATE
S5_TILES = 4
FFN_SPLIT = 2
FFN_CH = D_FF // FFN_SPLIT
VMEM_LIMIT = 56 * 1024 * 1024


def _dot(a, b):
    return jnp.dot(a.astype(BF16), b.astype(BF16), preferred_element_type=F32)


def _dot_split(a, g):
    hi = a.astype(BF16)
    lo = (a - hi.astype(F32)).astype(BF16)
    return (jnp.dot(hi, g, preferred_element_type=F32)
            + jnp.dot(lo, g, preferred_element_type=F32))


def _head_sum(a, g, split=False):
    f = _dot_split if split else _dot
    return jnp.concatenate([f(a[:, i:i + MXU_TILE], g) for i in range(0, a.shape[1], MXU_TILE)], axis=1)


def _aligned(i, m):
    return i if isinstance(i, int) else pl.multiple_of(i, m)


def _const_spec(shape):
    nd = len(shape)
    return pl.BlockSpec(shape, lambda *_: (0,) * nd, pipeline_mode=pl.Buffered(1))


def _params(*sem):
    return pltpu.CompilerParams(dimension_semantics=sem, vmem_limit_bytes=VMEM_LIMIT)


def _adaln_kernel(c_ref, w_ref, b_ref, o_ref):
    c = c_ref[...]
    s = c * jax.nn.sigmoid(c)
    o_ref[...] = jnp.dot(s, w_ref[...], preferred_element_type=F32, precision=HIGHEST) + b_ref[...]


def _adaln(c, w_ada, b_ada):
    bsz = c.shape[0]
    n = w_ada.shape[1]
    bn = D_MODEL
    return pl.pallas_call(
        _adaln_kernel,
        grid=(n // bn,),
        in_specs=[pl.BlockSpec((bsz, D_MODEL), lambda j: (0, 0)),
                  pl.BlockSpec((D_MODEL, bn), lambda j: (0, j)),
                  pl.BlockSpec((1, bn), lambda j: (0, j))],
        out_specs=pl.BlockSpec((bsz, bn), lambda j: (0, j)),
        out_shape=jax.ShapeDtypeStruct((bsz, n), F32),
        compiler_params=_params("arbitrary"),
        name="adaln",
    )(c, w_ada, b_ada.reshape(1, n))


def _s5prep_kernel(are_ref, aim_ref, ldt_ref, bre_ref, bim_ref, abre_o, abim_o, bbre_o, bbim_o):
    a_re = are_ref[...]
    a_im = aim_ref[...]
    dt = jnp.exp(ldt_ref[...])
    z_re, z_im = a_re * dt, a_im * dt
    mag = jnp.exp(z_re)
    ab_re, ab_im = mag * jnp.cos(z_im), mag * jnp.sin(z_im)
    den = a_re * a_re + a_im * a_im
    q_re = ((ab_re - 1.0) * a_re + ab_im * a_im) / den
    q_im = (ab_im * a_re - (ab_re - 1.0) * a_im) / den
    abre_o[...] = ab_re
    abim_o[...] = ab_im
    b_re = bre_ref[...]
    b_im = bim_ref[...]
    bbre_o[...] = q_re[:, None, :] * b_re - q_im[:, None, :] * b_im
    bbim_o[...] = q_re[:, None, :] * b_im + q_im[:, None, :] * b_re


def _s5prep(a_re, a_im, log_dt, b_re, b_im):
    g, p = a_re.shape
    c = b_re.shape[-1]
    b_re_t = jnp.transpose(b_re, (0, 2, 1))
    b_im_t = jnp.transpose(b_im, (0, 2, 1))
    return pl.pallas_call(
        _s5prep_kernel,
        out_shape=(jax.ShapeDtypeStruct((g, p), F32), jax.ShapeDtypeStruct((g, p), F32),
                   jax.ShapeDtypeStruct((g, c, p), F32), jax.ShapeDtypeStruct((g, c, p), F32)),
        name="s5prep",
    )(a_re, a_im, log_dt.reshape(g, 1), b_re_t, b_im_t)


def _inproj_kernel(x_ref, mod_ref, g1_ref, win_ref, mu_ref, w0_ref, waup_ref, a0_ref,
                   kk_ref, ka_ref, rk_ref, g64_ref,
                   r_o, k_o, v_o, ld_o, kn_o, b_o, gd_o, bonus_o, u_o, gates_o, carry, *, rows):
    @pl.when(pl.program_id(1) == 0)
    def _():
        carry[...] = jnp.zeros_like(carry)

    x = x_ref[...]
    mod = mod_ref[...]
    sh1, sc1 = mod[0:1], mod[1:2]
    ms = jnp.mean(x * x, axis=-1, keepdims=True)
    h = x * lax.rsqrt(ms + RMS_EPS) * g1_ref[...]
    h = h * (1.0 + sc1) + sh1
    proj = jnp.dot(h.astype(BF16), win_ref[...], preferred_element_type=F32)

    p = proj[:, :SHIFT_COLS]
    row = lax.broadcasted_iota(jnp.int32, p.shape, 0)
    prev = jnp.where(row == 0, carry[...], pltpu.roll(p, 1, axis=0))
    carry[...] = p[rows - 1:rows, :]
    ps = p + (prev - p) * mu_ref[...]

    w = RWKV_WIDTH
    r, k, v = ps[:, 0:w], ps[:, w:2 * w], ps[:, 2 * w:3 * w]
    wa = ps[:, 3 * w:3 * w + LORA_W + LORA_A]
    lane = lax.broadcasted_iota(jnp.int32, wa.shape, 1)
    wa = jnp.where(lane < LORA_W, jnp.tanh(wa), wa)
    delta = _dot(wa, waup_ref[...])
    w_raw = w0_ref[...] + delta[:, :w]
    ld = -math.exp(-0.5) * jax.nn.sigmoid(w_raw)
    eta = jax.nn.sigmoid(a0_ref[...] + delta[:, w:])
    gd = ps[:, 3 * w + LORA_W + LORA_A:SHIFT_COLS]

    g64 = g64_ref[...]
    kk = k * kk_ref[...]
    kn = kk * lax.rsqrt(_head_sum(kk * kk, g64) + L2_EPS)
    k2 = k * (1.0 + (eta - 1.0) * ka_ref[...])
    bonus = _head_sum(r * k2 * rk_ref[...], g64) * v

    r_o[...] = r
    k_o[...] = k2
    v_o[...] = v
    ld_o[...] = ld
    kn_o[...] = kn
    b_o[...] = kn * eta
    gd_o[...] = gd
    bonus_o[...] = bonus
    u_o[...] = proj[:, SHIFT_COLS:SHIFT_COLS + S5_WIDTH]
    gates_o[...] = jax.nn.sigmoid(proj[:, SHIFT_COLS + S5_WIDTH:]).astype(gates_o.dtype)


def _inproj(x2, mod3, g1, win, mu, w0, waup, a0, k_k, k_a, r_k, g64, *, bsz, seq, rows):
    n = bsz * seq
    nt = seq // rows
    w = RWKV_WIDTH
    row_spec = lambda cols: pl.BlockSpec((rows, cols), lambda b, t: (b * nt + t, 0))
    widths = [w] * 6 + [LORA_G, w, w, 2 * D_MODEL]
    outs = [jax.ShapeDtypeStruct((n, c), F32) for c in widths[:-1]] + [
        jax.ShapeDtypeStruct((n, widths[-1]), BF16)]
    return pl.pallas_call(
        functools.partial(_inproj_kernel, rows=rows),
        grid=(bsz, nt),
        in_specs=[row_spec(D_MODEL),
                  pl.BlockSpec((None, 6, D_MODEL), lambda b, t: (b, 0, 0)),
                  _const_spec((1, D_MODEL)),
                  _const_spec((D_MODEL, IN_COLS)),
                  _const_spec((1, SHIFT_COLS)),
                  _const_spec((1, w)),
                  _const_spec((LORA_W + LORA_A, 2 * w)),
                  _const_spec((1, w)),
                  _const_spec((1, w)), _const_spec((1, w)), _const_spec((1, w)),
                  _const_spec((MXU_TILE, MXU_TILE))],
        out_specs=[row_spec(c) for c in widths],
        out_shape=outs,
        scratch_shapes=[pltpu.VMEM((1, SHIFT_COLS), F32)],
        compiler_params=_params("arbitrary", "arbitrary"),
        name="inproj",
    )(x2, mod3, g1, win, mu, w0, waup, a0, k_k, k_a, r_k, g64)


def _stack_heads(z):
    lane = lax.broadcasted_iota(jnp.int32, z.shape, 1)
    first = lane < RWKV_HEAD
    return jnp.concatenate([jnp.where(first, z, 0.0), jnp.where(first, 0.0, z)], axis=0)


def _cumsum_rows(x):
    n = x.shape[0]
    row = lax.broadcasted_iota(jnp.int32, x.shape, 0)
    d = 1
    while d < n:
        if d < SUBLANES:
            shifted = jnp.where(row >= d, pltpu.roll(x, d, axis=0), 0.0)
        else:
            shifted = jnp.concatenate([jnp.zeros((d, x.shape[1]), x.dtype), x[:n - d]], axis=0)
        x = x + shifted
        d *= 2
    return x


_WKV_STAGE_BOUNDARIES = 6


def _spread_matrix():
    src = jnp.arange(LANES)
    dst = jnp.arange((SUBLANES - 1) * LANES)
    j, lane = dst // LANES, dst % LANES
    hit = (src[:, None] // SUBLANES == lane[None, :] // SUBLANES) & (src[:, None] % SUBLANES == j[None, :])
    return hit.astype(BF16)


def _wkv_terms(chains, spread, side=()):
    t = WKV_CHUNK
    n = STACK
    jobs = list(side)
    calls = [0]

    def between():
        k = calls[0]
        calls[0] += 1
        for _ in range((k + 1) * len(side) // _WKV_STAGE_BOUNDARIES - k * len(side) // _WKV_STAGE_BOUNDARIES):
            jobs.pop(0)()

    ri = lax.broadcasted_iota(jnp.int32, (n, n), 0)
    ci = lax.broadcasted_iota(jnp.int32, (n, n), 1)
    same_head = (ri // t) == (ci // t)
    strict = same_head & (ri > ci)
    incl = same_head & (ri >= ci)

    a_s, r_s, v_s, lhs, rhs, bk, w_end = [], [], [], [], [], [], []
    for r, k, v, ld, kn, b, cl in chains:
        cl_end = cl[t - 1:t, :]
        e_incl = jnp.exp(cl)
        e_excl = jnp.exp(cl - ld)
        e_neg = jnp.exp(-cl)
        e_end = jnp.exp(cl_end - cl)
        a_s.append(_stack_heads(-kn * e_excl))
        r_s.append(_stack_heads(r * e_incl))
        v_s.append(_stack_heads(v))
        lhs.append(jnp.concatenate([-kn * e_excl, r * e_incl], axis=0).astype(BF16))
        rhs.append(jnp.concatenate([_stack_heads(b * e_neg), _stack_heads(k * e_neg)], axis=0).astype(BF16))
        bk.append(jnp.concatenate([_stack_heads(b * e_end), _stack_heads(k * e_end)], axis=0).astype(BF16))
        w_end.append(jnp.exp(cl_end))

    amat = [lax.dot_general(l, rr, (((1,), (1,)), ((), ())), preferred_element_type=F32)
            for l, rr in zip(lhs, rhs)]
    dup = lambda m: jnp.concatenate([m, m], axis=0)
    a_ab = [jnp.where(strict, dup(a[:t, :n]), 0.0) for a in amat]
    a_ak = [jnp.where(strict, dup(a[:t, n:]), 0.0).astype(BF16) for a in amat]
    a_r = [jnp.concatenate([jnp.where(incl, dup(a[t:, :n]), 0.0), jnp.where(incl, dup(a[t:, n:]), 0.0)],
                           axis=1).astype(BF16) for a in amat]

    base = SUBLANES
    blk = (ri // base) == (ci // base)
    packed = []
    for a in a_ab:
        d = jnp.where(blk, a, 0.0)
        acc = d[0:base]
        for q in range(1, n // base):
            acc = acc + d[q * base:(q + 1) * base]
        packed.append(acc)
    cols = jnp.dot(jnp.concatenate(packed, axis=0).astype(BF16), spread,
                   preferred_element_type=F32)
    sub = lax.broadcasted_iota(jnp.int32, (base, LANES), 0)
    lane = lax.broadcasted_iota(jnp.int32, (base, LANES), 1)
    eye_packed = jnp.where(sub == lane % base, 1.0, 0.0)
    x = []
    for i in range(len(a_ab)):
        xp = eye_packed
        for j in range(base - 1):
            lj = cols[i * base:(i + 1) * base, j * LANES:(j + 1) * LANES]
            xp = xp + lj * jnp.broadcast_to(xp[j:j + 1, :], (base, LANES))
        x.append(jnp.where(blk, jnp.concatenate([xp] * (n // base), axis=0), 0.0))
    between()
    s = base
    while s < t:
        level = ((ri // (2 * s)) == (ci // (2 * s))) & ((ri // s) % 2 == 1) & ((ci // s) % 2 == 0)
        c = [jnp.where(level, a, 0.0).astype(BF16) for a in a_ab]
        xb = [xi.astype(BF16) for xi in x]
        odd = lambda m: jnp.concatenate([m[q * s:(q + 1) * s] for q in range(1, n // s, 2)], axis=0)
        xo = [odd(xi) for xi in x]
        xc = [jnp.dot(xoi.astype(BF16), cc, preferred_element_type=F32).astype(BF16)
              for xoi, cc in zip(xo, c)]
        xo = [xoi + jnp.dot(xci, xbi, preferred_element_type=F32) for xoi, xci, xbi in zip(xo, xc, xb)]
        x = [jnp.concatenate([xoi[(q // 2) * s:(q // 2 + 1) * s] if q % 2 else xi[q * s:(q + 1) * s]
                              for q in range(n // s)], axis=0) for xi, xoi in zip(x, xo)]
        between()
        s *= 2

    akv = [jnp.dot(a, v.astype(BF16), preferred_element_type=F32) for a, v in zip(a_ak, v_s)]
    tg = [_dot(xi, jnp.concatenate([a, kv], axis=1)) for xi, a, kv in zip(x, a_s, akv)]
    between()
    low = [jnp.concatenate([g, jnp.concatenate([jnp.zeros_like(v), v], axis=1)], axis=0).astype(BF16)
           for g, v in zip(tg, v_s)]
    out1 = [jnp.dot(a, lo, preferred_element_type=F32) for a, lo in zip(a_r, low)]
    out2 = [lax.dot_general(bb, lo, (((0,), (0,)), ((), ())), preferred_element_type=F32)
            for bb, lo in zip(bk, low)]
    between()
    ki = lax.broadcasted_iota(jnp.int32, (LANES, LANES), 0)
    kj = lax.broadcasted_iota(jnp.int32, (LANES, LANES), 1)
    terms = []
    for rs, o1, o2, we in zip(r_s, out1, out2, w_end):
        r_hat = rs + o1[:, :LANES]
        m = o2[:, :LANES] + jnp.where(ki == kj, we, 0.0)
        terms.append((r_hat, o1[:, LANES:], m, o2[:, LANES:]))
    return terms


def _wkv_kernel(r_ref, k_ref, v_ref, ld_ref, kn_ref, b_ref, spread_ref, y_ref, s_ref, mr_ref, nc_ref,
                yv_ref, *, rows, groups):
    t = WKV_CHUNK
    npair = RWKV_WIDTH // LANES

    @pl.when(pl.program_id(1) == 0)
    def _():
        s_ref[...] = jnp.zeros_like(s_ref)

    def chain_step(c):
        rs = pl.ds(_aligned(c * STACK, STACK), STACK)
        outs = [jnp.dot(mr_ref[p, pl.ds(_aligned(c * 2 * STACK, 2 * STACK), 2 * STACK), :],
                        s_ref[p].astype(BF16), preferred_element_type=F32) for p in range(npair)]
        for p, out in enumerate(outs):
            s_ref[p] = out[:LANES] + nc_ref[p, rs, :]
            ys = out[LANES:] + yv_ref[p, rs, :]
            y_ref[pl.ds(_aligned(c * t, t), t), p * LANES:(p + 1) * LANES] = ys[:t] + ys[t:]

    def precompute(chunks, side):
        chains, where = [], []
        for c in chunks:
            sl = pl.ds(_aligned(c * t, t), t)
            tiles = [ref[sl, :] for ref in (r_ref, k_ref, v_ref, ld_ref, kn_ref, b_ref)]
            cl = _cumsum_rows(tiles[3])
            for p in range(npair):
                ls = slice(p * LANES, (p + 1) * LANES)
                chains.append(tuple(a[:, ls] for a in tiles) + (cl[:, ls],))
                where.append((c, p))
        for (c, p), (r_hat, y_v, m, n_c) in zip(where, _wkv_terms(chains, spread_ref[...], side)):
            mr_ref[p, pl.ds(_aligned(c * 2 * STACK, 2 * STACK), 2 * STACK), :] = (
                jnp.concatenate([m, r_hat], axis=0).astype(BF16))
            rs = pl.ds(_aligned(c * STACK, STACK), STACK)
            nc_ref[p, rs, :] = n_c
            yv_ref[p, rs, :] = y_v

    assert sum(groups) * t == rows, (groups, rows)
    previous, start = (), 0
    for count in groups:
        chunks = tuple(range(start, start + count))
        precompute(chunks, [functools.partial(chain_step, c) for c in previous])
        previous, start = chunks, start + count
    for c in previous:
        chain_step(c)


WKV_MAX_GROUP = 8
WKV_LAST_GROUP = 2


def _wkv_groups(nchunk):
    if nchunk <= WKV_LAST_GROUP:
        return (nchunk,)
    sizes, left = [], nchunk - WKV_LAST_GROUP
    while left > 0:
        sizes.append(min(WKV_MAX_GROUP, left))
        left -= sizes[-1]
    return tuple(sizes) + (WKV_LAST_GROUP,)


def _wkv(r, k, v, ld, kn, b, *, bsz, seq, rows):
    n = bsz * seq
    nt = seq // rows
    npair = RWKV_WIDTH // LANES
    nchunk = rows // WKV_CHUNK
    spec = pl.BlockSpec((rows, RWKV_WIDTH), lambda bi, ti: (bi * nt + ti, 0))
    return pl.pallas_call(
        functools.partial(_wkv_kernel, rows=rows, groups=_wkv_groups(nchunk)),
        grid=(bsz, nt),
        in_specs=[spec] * 6 + [_const_spec((LANES, (SUBLANES - 1) * LANES))],
        out_specs=spec,
        out_shape=jax.ShapeDtypeStruct((n, RWKV_WIDTH), F32),
        scratch_shapes=[pltpu.VMEM((npair, LANES, LANES), F32),
                        pltpu.VMEM((npair, nchunk * 2 * STACK, LANES), BF16),
                        pltpu.VMEM((npair, nchunk * STACK, LANES), F32),
                        pltpu.VMEM((npair, nchunk * STACK, LANES), F32)],
        compiler_params=_params("arbitrary", "arbitrary"),
        name="wkv",
    )(r, k, v, ld, kn, b, _spread_matrix())


def _gelu_tanh(x):
    return 0.5 * x * (1.0 + jnp.tanh(math.sqrt(2.0 / math.pi) * (x + 0.044715 * x * x * x)))


def _s5_kernel(u_ref, wre_ref, wim_ref, cc_ref, ar_ref, ai_ref, d_ref, y_ref,
               xre, xim, st_re, st_im, *, steps):
    @pl.when(pl.program_id(1) == 0)
    def _():
        st_re[...] = jnp.zeros_like(st_re)
        st_im[...] = jnp.zeros_like(st_im)

    rows = steps * SUBLANES
    ns = S5_SLAB_STATE
    half = S5_GROUPS * S5_STATE // 2
    inputs = {}

    def u_of(g):
        if g not in inputs:
            u = jnp.swapaxes(u_ref[g * SUBLANES:(g + 1) * SUBLANES, :, :], 0, 1).reshape(rows, S5_WIDTH)
            inputs[g] = (u, u.astype(BF16))
        return inputs[g]

    def in_proj_jobs(g):
        def job(j, ref, w_ref):
            ref[g % 2, :, j * ns:(j + 1) * ns] = jnp.dot(u_of(g)[1][:, j * LANES:(j + 1) * LANES], w_ref[j],
                                                         preferred_element_type=F32)
        return [functools.partial(job, j, ref, w_ref)
                for j in range(S5_SLABS) for ref, w_ref in ((xre, wre_ref), (xim, wim_ref))]

    def out_proj_jobs(g):
        def job(j):
            xs = jnp.concatenate([xre[g % 2, :, j * ns:(j + 1) * ns], xim[g % 2, :, j * ns:(j + 1) * ns]],
                                 axis=1)
            ls = slice(j * LANES, (j + 1) * LANES)
            yj = (jnp.dot(xs.astype(BF16), cc_ref[j], preferred_element_type=F32)
                  + d_ref[:, ls] * u_of(g)[0][:, ls])
            y_ref[g * SUBLANES:(g + 1) * SUBLANES, :, ls] = jnp.swapaxes(
                _gelu_tanh(yj).reshape(steps, SUBLANES, LANES), 0, 1).astype(y_ref.dtype)
        return [functools.partial(job, j) for j in range(S5_SLABS)]

    def scan(g, jobs):
        total = 2 * steps
        done = 0
        for c in range(2):
            cs = slice(c * half, (c + 1) * half)
            ar = jnp.broadcast_to(ar_ref[:, cs], (SUBLANES, half))
            ai = jnp.broadcast_to(ai_ref[:, cs], (SUBLANES, half))
            x_re, x_im = st_re[g, :, cs], st_im[g, :, cs]
            for i in range(steps):
                k = c * steps + i
                for _ in range((k + 1) * len(jobs) // total - done):
                    jobs[done]()
                    done += 1
                rs = slice(i * SUBLANES, (i + 1) * SUBLANES)
                n_re = ar * x_re - ai * x_im + xre[g % 2, rs, cs]
                n_im = ar * x_im + ai * x_re + xim[g % 2, rs, cs]
                xre[g % 2, rs, cs] = n_re
                xim[g % 2, rs, cs] = n_im
                x_re, x_im = n_re, n_im
            st_re[g, :, cs] = x_re
            st_im[g, :, cs] = x_im

    for job in in_proj_jobs(0):
        job()
    for g in range(S5_TILES):
        jobs = (out_proj_jobs(g - 1) if g >= 1 else []) + (in_proj_jobs(g + 1) if g + 1 < S5_TILES else [])
        scan(g, jobs)
    for job in out_proj_jobs(S5_TILES - 1):
        job()


def _s5(u3, wre, wim, cc, ar, ai, d, *, bsz, seq, steps):
    nstate = S5_GROUPS * S5_STATE
    rows = steps * SUBLANES
    nb = S5_TILES * SUBLANES
    spec = pl.BlockSpec((nb, steps, S5_WIDTH), lambda g, t: (g, t, 0))
    return pl.pallas_call(
        functools.partial(_s5_kernel, steps=steps),
        grid=(bsz // nb, seq // steps),
        in_specs=[spec,
                  _const_spec((S5_SLABS, LANES, S5_SLAB_STATE)),
                  _const_spec((S5_SLABS, LANES, S5_SLAB_STATE)),
                  _const_spec((S5_SLABS, 2 * S5_SLAB_STATE, LANES)),
                  _const_spec((1, nstate)), _const_spec((1, nstate)),
                  _const_spec((1, S5_WIDTH))],
        out_specs=spec,
        out_shape=jax.ShapeDtypeStruct((bsz, seq, S5_WIDTH), BF16),
        scratch_shapes=[pltpu.VMEM((2, rows, nstate), F32), pltpu.VMEM((2, rows, nstate), F32),
                        pltpu.VMEM((S5_TILES, SUBLANES, nstate), F32),
                        pltpu.VMEM((S5_TILES, SUBLANES, nstate), F32)],
        compiler_params=_params("arbitrary", "arbitrary"),
        name="s5",
    )(u3, wre, wim, cc, ar, ai, d)


def _merge_kernel(y_ref, gd_ref, bonus_ref, ys5_ref, gates_ref, x_ref, mod_ref, lng_ref, lnb_ref,
                  gmean_ref, gup_ref, wrw_ref, wglu_ref, wout_ref, o_ref):
    gm = gmean_ref[...]
    y = y_ref[...]
    mean = _head_sum(y, gm, split=True)
    d = y - mean
    var = _head_sum(d * d, gm)
    yn = d * lax.rsqrt(var + GN_EPS) * lng_ref[...] + lnb_ref[...]
    g = _dot(jax.nn.sigmoid(gd_ref[...]), gup_ref[...])
    ya = _dot((yn + bonus_ref[...]) * g, wrw_ref[...])
    z = _dot(ys5_ref[...], wglu_ref[...])
    yb = z[:, :D_MODEL] * jax.nn.sigmoid(z[:, D_MODEL:])
    gates = gates_ref[...].astype(F32)
    mixed = _dot(gates[:, :D_MODEL] * ya + gates[:, D_MODEL:] * yb, wout_ref[...])
    gt1 = mod_ref[...][2:3]
    o_ref[...] = x_ref[...] + gt1 * mixed


def _merge(y, gd, bonus, ys5, gates, x2, mod3, lng, lnb, gmean, gup, wrw, wglu, wout, *, bsz, seq, rows):
    n = bsz * seq
    nt = seq // rows
    w = RWKV_WIDTH
    row_spec = lambda cols: pl.BlockSpec((rows, cols), lambda i: (i, 0))
    return pl.pallas_call(
        _merge_kernel,
        grid=(n // rows,),
        in_specs=[row_spec(w), row_spec(LORA_G), row_spec(w), row_spec(w), row_spec(2 * D_MODEL),
                  row_spec(D_MODEL),
                  pl.BlockSpec((None, 6, D_MODEL), lambda i: (i // nt, 0, 0)),
                  _const_spec((1, w)), _const_spec((1, w)), _const_spec((MXU_TILE, MXU_TILE)),
                  _const_spec((LORA_G, w)), _const_spec((w, D_MODEL)), _const_spec((S5_WIDTH, 2 * D_MODEL)),
                  _const_spec((D_MODEL, D_MODEL))],
        out_specs=row_spec(D_MODEL),
        out_shape=jax.ShapeDtypeStruct((n, D_MODEL), F32),
        compiler_params=_params("arbitrary"),
        name="merge",
    )(y, gd, bonus, ys5, gates, x2, mod3, lng, lnb, gmean, gup, wrw, wglu, wout)


def _ffn_kernel(x_ref, mod_ref, g2_ref, wup_ref, cw_ref, cb_ref, wdn_ref, gf_ref, o_ref, buf, *, rows):
    pad = SUBLANES

    @pl.when(pl.program_id(1) == 0)
    def _():
        buf[:, 0:pad, :] = jnp.zeros((FFN_SPLIT, pad, 2 * FFN_CH), F32)

    mod = mod_ref[...]
    sh2, sc2, gt2 = mod[3:4], mod[4:5], mod[5:6]

    def normed(xh):
        ms = jnp.mean(xh * xh, axis=-1, keepdims=True)
        h = xh * lax.rsqrt(ms + RMS_EPS) * g2_ref[...]
        return (h * (1.0 + sc2) + sh2).astype(BF16)

    halves = 2
    hr = rows // halves
    x = [x_ref[h * hr:(h + 1) * hr, :] for h in range(halves)]
    hb = [normed(x[0]), None]
    acc = [jnp.zeros((hr, D_MODEL), F32) for _ in range(halves)]
    for c in range(FFN_SPLIT):
        for h in range(halves):
            if hb[h] is None:
                hb[h] = normed(x[h])
            r0 = pad + h * hr
            buf[c, r0:r0 + hr, :] = jnp.dot(hb[h], wup_ref[c], preferred_element_type=F32)
        cw = cw_ref[c]
        for h in range(halves):
            hid = cb_ref[c]
            for j in range(CONV_W):
                off = pad + h * hr - (CONV_W - 1) + j
                hid = hid + cw[j:j + 1, :] * buf[c, off:off + hr, :]
            gate, up = hid[:, :FFN_CH], hid[:, FFN_CH:]
            act = (0.5 * gate) * (1.0 + jnp.tanh(0.5 * gate)) * up
            acc[h] = acc[h] + jnp.dot(act.astype(BF16), wdn_ref[c], preferred_element_type=F32)
        buf[c, pad - (CONV_W - 1):pad, :] = buf[c, pad + rows - (CONV_W - 1):pad + rows, :]

    for h in range(halves):
        x2 = x[h] + gt2 * acc[h]
        ms2 = jnp.mean(x2 * x2, axis=-1, keepdims=True)
        o_ref[h * hr:(h + 1) * hr, :] = x2 * lax.rsqrt(ms2 + RMS_EPS) * gf_ref[...]


def _ffn(x1, mod3, g2, wup, cw, cb, wdn, gf, *, bsz, seq, rows):
    n = bsz * seq
    nt = seq // rows
    row_spec = pl.BlockSpec((rows, D_MODEL), lambda b, t: (b * nt + t, 0))
    return pl.pallas_call(
        functools.partial(_ffn_kernel, rows=rows),
        grid=(bsz, nt),
        in_specs=[row_spec,
                  pl.BlockSpec((None, 6, D_MODEL), lambda b, t: (b, 0, 0)),
                  _const_spec((1, D_MODEL)),
                  _const_spec((FFN_SPLIT, D_MODEL, 2 * FFN_CH)),
                  _const_spec((FFN_SPLIT, CONV_W, 2 * FFN_CH)),
                  _const_spec((FFN_SPLIT, 1, 2 * FFN_CH)),
                  _const_spec((FFN_SPLIT, FFN_CH, D_MODEL)),
                  _const_spec((1, D_MODEL))],
        out_specs=row_spec,
        out_shape=jax.ShapeDtypeStruct((n, D_MODEL), F32),
        scratch_shapes=[pltpu.VMEM((FFN_SPLIT, rows + SUBLANES, 2 * FFN_CH), F32)],
        compiler_params=_params("arbitrary", "arbitrary"),
        name="ffn",
    )(x1, mod3, g2, wup, cw, cb, wdn, gf)


def _split_cols(a):
    parts = [jnp.concatenate([a[..., c * FFN_CH:(c + 1) * FFN_CH],
                              a[..., D_FF + c * FFN_CH:D_FF + (c + 1) * FFN_CH]], axis=-1)
             for c in range(FFN_SPLIT)]
    return jnp.stack(parts, axis=0)


def _block_diag_ones(n, blk, value, dtype):
    i = jnp.arange(n) // blk
    return jnp.where(i[:, None] == i[None, :], value, 0.0).astype(dtype)


def kernel(x, c, w_ada, b_ada, norm1_g, w_in, mu_shift, rwkv_w0, rwkv_w_up, rwkv_a0, rwkv_a_up,
           rwkv_g_up, rwkv_k_k, rwkv_k_a, rwkv_r_k, rwkv_ln_g, rwkv_ln_b, w_out_rwkv, s5_a_re,
           s5_a_im, s5_log_dt, s5_b_re, s5_b_im, s5_c_re, s5_c_im, s5_d, w_glu, w_out, norm2_g,
           w_ffn_up, ffn_conv_w, ffn_conv_b, w_ffn_down, norm_f_g):
    bsz, seq, _ = x.shape
    depth = w_ada.shape[0]
    n = bsz * seq
    w = RWKV_WIDTH
    rows_a = min(256, seq)
    rows_m = min(512, seq)
    rows_f = min(512, seq)
    s5_steps = min(64, seq)

    g64 = _block_diag_ones(MXU_TILE, RWKV_HEAD, 1.0, BF16)
    gmean = _block_diag_ones(MXU_TILE, RWKV_HEAD, 1.0 / RWKV_HEAD, BF16)
    eye8 = jnp.eye(LANES // S5_GROUP, dtype=F32)
    row = lambda a: a.reshape(1, -1)

    assert depth == 1, depth
    x2 = x.reshape(n, D_MODEL)
    for l in range(depth):
        mod3 = _adaln(c, w_ada[l], b_ada[l]).reshape(bsz, 6, D_MODEL)

        zero = jnp.zeros((LORA_W, w), F32)
        waup = jnp.concatenate([jnp.concatenate([rwkv_w_up[l], zero], axis=1),
                                jnp.concatenate([zero, rwkv_a_up[l]], axis=1)], axis=0).astype(BF16)
        r, k, v, ld, kn, b, gd, bonus, u, gates = _inproj(
            x2, mod3, row(norm1_g[l]), w_in[l].astype(BF16), row(mu_shift[l]), row(rwkv_w0[l]), waup,
            row(rwkv_a0[l]), row(rwkv_k_k[l]), row(rwkv_k_a[l]),
            row(rwkv_r_k[l]), g64, bsz=bsz, seq=seq, rows=rows_a)

        y = _wkv(r, k, v, ld, kn, b, bsz=bsz, seq=seq, rows=min(1024, seq))

        ab_re, ab_im, bb_re, bb_im = _s5prep(s5_a_re[l], s5_a_im[l], s5_log_dt[l], s5_b_re[l], s5_b_im[l])
        gl = LANES // S5_GROUP
        slab_in = lambda bb: jnp.einsum('jgcp,gh->jgchp', bb.reshape(S5_SLABS, gl, S5_GROUP, S5_STATE),
                                        eye8).reshape(S5_SLABS, LANES, S5_SLAB_STATE).astype(BF16)
        slab_out = lambda cc: jnp.einsum('jgcp,gh->jgphc', cc.reshape(S5_SLABS, gl, S5_GROUP, S5_STATE),
                                         eye8).reshape(S5_SLABS, S5_SLAB_STATE, LANES)
        cc = jnp.concatenate([slab_out(s5_c_re[l]), -slab_out(s5_c_im[l])], axis=1).astype(BF16)
        ys5 = _s5(u.reshape(bsz, seq, S5_WIDTH), slab_in(bb_re), slab_in(bb_im), cc, row(ab_re),
                  row(ab_im), row(s5_d[l]), bsz=bsz, seq=seq, steps=s5_steps).reshape(n, S5_WIDTH)

        x1 = _merge(y, gd, bonus, ys5, gates, x2, mod3, row(rwkv_ln_g[l]), row(rwkv_ln_b[l]), gmean,
                    rwkv_g_up[l].astype(BF16), w_out_rwkv[l].astype(BF16), w_glu[l].astype(BF16), w_out[l].astype(BF16),
                    bsz=bsz, seq=seq, rows=rows_m)

        gf = row(norm_f_g)
        wdn = w_ffn_down[l].astype(BF16).reshape(FFN_SPLIT, FFN_CH, D_MODEL)
        x2 = _ffn(x1, mod3, row(norm2_g[l]), _split_cols(w_ffn_up[l]).astype(BF16),
                  _split_cols(ffn_conv_w[l]), _split_cols(ffn_conv_b[l].reshape(1, -1)), wdn,
                  gf, bsz=bsz, seq=seq, rows=rows_f)
    return x2.reshape(bsz, seq, D_MODEL)
```

```python
import functools
import math

import jax
import jax.numpy as jnp
from jax import lax
from jax.experimental import pallas as pl
from jax.experimental.pallas import tpu as pltpu

F32 = jnp.float32
BF16 = jnp.bfloat16
HIGHEST = lax.Precision.HIGHEST

D_MODEL = 1024
RWKV_WIDTH = 512
RWKV_HEAD = 64
LORA_W = 64
LORA_A = 64
LORA_G = 128
S5_WIDTH = 512
S5_GROUP = 16
S5_GROUPS = 32
S5_STATE = 64
D_FF = 2816
CONV_W = 3
RMS_EPS = 1e-6
GN_EPS = 64e-5
L2_EPS = 1e-12
SHIFT_COLS = 3 * RWKV_WIDTH + LORA_W + LORA_A + LORA_G
IN_COLS = SHIFT_COLS + S5_WIDTH + 2 * D_MODEL

LANES = 128
SUBLANES = 8
MXU_TILE = 256
WKV_CHUNK = 64
HEADS_PER_TILE = LANES // RWKV_HEAD
STACK = HEADS_PER_TILE * WKV_CHUNK
S5_SLABS = S5_WIDTH // LANES
S5_SLAB_STATE = (LANES // S5_GROUP) * S5_STATE
S5_TILES = 4
FFN_SPLIT = 2
FFN_CH = D_FF // FFN_SPLIT
VMEM_LIMIT = 56 * 1024 * 1024


def _dot(a, b):
    return jnp.dot(a.astype(BF16), b.astype(BF16), preferred_element_type=F32)


def _dot_split(a, g):
    hi = a.astype(BF16)
    lo = (a - hi.astype(F32)).astype(BF16)
    return (jnp.dot(hi, g, preferred_element_type=F32)
            + jnp.dot(lo, g, preferred_element_type=F32))


def _head_sum(a, g, split=False):
    f = _dot_split if split else _dot
    return jnp.concatenate([f(a[:, i:i + MXU_TILE], g) for i in range(0, a.shape[1], MXU_TILE)], axis=1)


def _aligned(i, m):
    return i if isinstance(i, int) else pl.multiple_of(i, m)


def _const_spec(shape):
    nd = len(shape)
    return pl.BlockSpec(shape, lambda *_: (0,) * nd, pipeline_mode=pl.Buffered(1))


def _params(*sem):
    return pltpu.CompilerParams(dimension_semantics=sem, vmem_limit_bytes=VMEM_LIMIT)


def _adaln_kernel(c_ref, w_ref, b_ref, o_ref):
    c = c_ref[...]
    s = c * jax.nn.sigmoid(c)
    o_ref[...] = jnp.dot(s, w_ref[...], preferred_element_type=F32, precision=HIGHEST) + b_ref[...]


def _adaln(c, w_ada, b_ada):
    bsz = c.shape[0]
    n = w_ada.shape[1]
    bn = D_MODEL
    return pl.pallas_call(
        _adaln_kernel,
        grid=(n // bn,),
        in_specs=[pl.BlockSpec((bsz, D_MODEL), lambda j: (0, 0)),
                  pl.BlockSpec((D_MODEL, bn), lambda j: (0, j)),
                  pl.BlockSpec((1, bn), lambda j: (0, j))],
        out_specs=pl.BlockSpec((bsz, bn), lambda j: (0, j)),
        out_shape=jax.ShapeDtypeStruct((bsz, n), F32),
        compiler_params=_params("arbitrary"),
        name="adaln",
    )(c, w_ada, b_ada.reshape(1, n))


def _s5prep_kernel(are_ref, aim_ref, ldt_ref, bre_ref, bim_ref, abre_o, abim_o, bbre_o, bbim_o):
    a_re = are_ref[...]
    a_im = aim_ref[...]
    dt = jnp.exp(ldt_ref[...])
    z_re, z_im = a_re * dt, a_im * dt
    mag = jnp.exp(z_re)
    ab_re, ab_im = mag * jnp.cos(z_im), mag * jnp.sin(z_im)
    den = a_re * a_re + a_im * a_im
    q_re = ((ab_re - 1.0) * a_re + ab_im * a_im) / den
    q_im = (ab_im * a_re - (ab_re - 1.0) * a_im) / den
    abre_o[...] = ab_re
    abim_o[...] = ab_im
    b_re = bre_ref[...]
    b_im = bim_ref[...]
    bbre_o[...] = q_re[:, None, :] * b_re - q_im[:, None, :] * b_im
    bbim_o[...] = q_re[:, None, :] * b_im + q_im[:, None, :] * b_re


def _s5prep(a_re, a_im, log_dt, b_re, b_im):
    g, p = a_re.shape
    c = b_re.shape[-1]
    b_re_t = jnp.transpose(b_re, (0, 2, 1))
    b_im_t = jnp.transpose(b_im, (0, 2, 1))
    return pl.pallas_call(
        _s5prep_kernel,
        out_shape=(jax.ShapeDtypeStruct((g, p), F32), jax.ShapeDtypeStruct((g, p), F32),
                   jax.ShapeDtypeStruct((g, c, p), F32), jax.ShapeDtypeStruct((g, c, p), F32)),
        name="s5prep",
    )(a_re, a_im, log_dt.reshape(g, 1), b_re_t, b_im_t)


def _inproj_kernel(x_ref, mod_ref, g1_ref, win_ref, mu_ref, w0_ref, waup_ref, a0_ref,
                   kk_ref, ka_ref, rk_ref, g64_ref,
                   r_o, k_o, v_o, ld_o, kn_o, b_o, gd_o, bonus_o, u_o, gates_o, carry, *, rows):
    @pl.when(pl.program_id(1) == 0)
    def _():
        carry[...] = jnp.zeros_like(carry)

    x = x_ref[...]
    mod = mod_ref[...]
    sh1, sc1 = mod[0:1], mod[1:2]
    ms = jnp.mean(x * x, axis=-1, keepdims=True)
    h = x * lax.rsqrt(ms + RMS_EPS) * g1_ref[...]
    h = h * (1.0 + sc1) + sh1
    proj = jnp.dot(h.astype(BF16), win_ref[...], preferred_element_type=F32)

    p = proj[:, :SHIFT_COLS]
    row = lax.broadcasted_iota(jnp.int32, p.shape, 0)
    prev = jnp.where(row == 0, carry[...], pltpu.roll(p, 1, axis=0))
    carry[...] = p[rows - 1:rows, :]
    ps = p + (prev - p) * mu_ref[...]

    w = RWKV_WIDTH
    r, k, v = ps[:, 0:w], ps[:, w:2 * w], ps[:, 2 * w:3 * w]
    wa = ps[:, 3 * w:3 * w + LORA_W + LORA_A]
    lane = lax.broadcasted_iota(jnp.int32, wa.shape, 1)
    wa = jnp.where(lane < LORA_W, jnp.tanh(wa), wa)
    delta = _dot(wa, waup_ref[...])
    w_raw = w0_ref[...] + delta[:, :w]
    ld = -math.exp(-0.5) * jax.nn.sigmoid(w_raw)
    eta = jax.nn.sigmoid(a0_ref[...] + delta[:, w:])
    gd = ps[:, 3 * w + LORA_W + LORA_A:SHIFT_COLS]

    g64 = g64_ref[...]
    kk = k * kk_ref[...]
    kn = kk * lax.rsqrt(_head_sum(kk * kk, g64) + L2_EPS)
    k2 = k * (1.0 + (eta - 1.0) * ka_ref[...])
    bonus = _head_sum(r * k2 * rk_ref[...], g64) * v

    r_o[...] = r
    k_o[...] = k2
    v_o[...] = v
    ld_o[...] = ld
    kn_o[...] = kn
    b_o[...] = kn * eta
    gd_o[...] = gd
    bonus_o[...] = bonus
    u_o[...] = proj[:, SHIFT_COLS:SHIFT_COLS + S5_WIDTH]
    gates_o[...] = jax.nn.sigmoid(proj[:, SHIFT_COLS + S5_WIDTH:])


def _inproj(x2, mod3, g1, win, mu, w0, waup, a0, k_k, k_a, r_k, g64, *, bsz, seq, rows):
    n = bsz * seq
    nt = seq // rows
    w = RWKV_WIDTH
    row_spec = lambda cols: pl.BlockSpec((rows, cols), lambda b, t: (b * nt + t, 0))
    widths = [w] * 6 + [LORA_G, w, w, 2 * D_MODEL]
    outs = [jax.ShapeDtypeStruct((n, c), F32) for c in widths]
    return pl.pallas_call(
        functools.partial(_inproj_kernel, rows=rows),
        grid=(bsz, nt),
        in_specs=[row_spec(D_MODEL),
                  pl.BlockSpec((None, 6, D_MODEL), lambda b, t: (b, 0, 0)),
                  _const_spec((1, D_MODEL)),
                  _const_spec((D_MODEL, IN_COLS)),
                  _const_spec((1, SHIFT_COLS)),
                  _const_spec((1, w)),
                  _const_spec((LORA_W + LORA_A, 2 * w)),
                  _const_spec((1, w)),
                  _const_spec((1, w)), _const_spec((1, w)), _const_spec((1, w)),
                  _const_spec((MXU_TILE, MXU_TILE))],
        out_specs=[row_spec(c) for c in widths],
        out_shape=outs,
        scratch_shapes=[pltpu.VMEM((1, SHIFT_COLS), F32)],
        compiler_params=_params("arbitrary", "arbitrary"),
        name="inproj",
    )(x2, mod3, g1, win, mu, w0, waup, a0, k_k, k_a, r_k, g64)


def _stack_heads(z):
    lane = lax.broadcasted_iota(jnp.int32, z.shape, 1)
    first = lane < RWKV_HEAD
    return jnp.concatenate([jnp.where(first, z, 0.0), jnp.where(first, 0.0, z)], axis=0)


def _cumsum_rows(x):
    n = x.shape[0]
    row = lax.broadcasted_iota(jnp.int32, x.shape, 0)
    d = 1
    while d < n:
        if d < SUBLANES:
            shifted = jnp.where(row >= d, pltpu.roll(x, d, axis=0), 0.0)
        else:
            shifted = jnp.concatenate([jnp.zeros((d, x.shape[1]), x.dtype), x[:n - d]], axis=0)
        x = x + shifted
        d *= 2
    return x


_WKV_STAGE_BOUNDARIES = 6


def _spread_matrix():
    src = jnp.arange(LANES)
    dst = jnp.arange((SUBLANES - 1) * LANES)
    j, lane = dst // LANES, dst % LANES
    hit = (src[:, None] // SUBLANES == lane[None, :] // SUBLANES) & (src[:, None] % SUBLANES == j[None, :])
    return hit.astype(BF16)


def _wkv_terms(chains, spread, side=()):
    t = WKV_CHUNK
    n = STACK
    jobs = list(side)
    calls = [0]

    def between():
        k = calls[0]
        calls[0] += 1
        for _ in range((k + 1) * len(side) // _WKV_STAGE_BOUNDARIES - k * len(side) // _WKV_STAGE_BOUNDARIES):
            jobs.pop(0)()

    ri = lax.broadcasted_iota(jnp.int32, (n, n), 0)
    ci = lax.broadcasted_iota(jnp.int32, (n, n), 1)
    same_head = (ri // t) == (ci // t)
    strict = same_head & (ri > ci)
    incl = same_head & (ri >= ci)

    a_s, r_s, v_s, lhs, rhs, bk, w_end = [], [], [], [], [], [], []
    for r, k, v, ld, kn, b, cl in chains:
        cl_end = cl[t - 1:t, :]
        e_incl = jnp.exp(cl)
        e_excl = jnp.exp(cl - ld)
        e_neg = jnp.exp(-cl)
        e_end = jnp.exp(cl_end - cl)
        a_s.append(_stack_heads(-kn * e_excl))
        r_s.append(_stack_heads(r * e_incl))
        v_s.append(_stack_heads(v))
        lhs.append(jnp.concatenate([-kn * e_excl, r * e_incl], axis=0).astype(BF16))
        rhs.append(jnp.concatenate([_stack_heads(b * e_neg), _stack_heads(k * e_neg)], axis=0).astype(BF16))
        bk.append(jnp.concatenate([_stack_heads(b * e_end), _stack_heads(k * e_end)], axis=0).astype(BF16))
        w_end.append(jnp.exp(cl_end))

    amat = [lax.dot_general(l, rr, (((1,), (1,)), ((), ())), preferred_element_type=F32)
            for l, rr in zip(lhs, rhs)]
    dup = lambda m: jnp.concatenate([m, m], axis=0)
    a_ab = [jnp.where(strict, dup(a[:t, :n]), 0.0) for a in amat]
    a_ak = [jnp.where(strict, dup(a[:t, n:]), 0.0).astype(BF16) for a in amat]
    a_r = [jnp.concatenate([jnp.where(incl, dup(a[t:, :n]), 0.0), jnp.where(incl, dup(a[t:, n:]), 0.0)],
                           axis=1).astype(BF16) for a in amat]

    base = SUBLANES
    blk = (ri // base) == (ci // base)
    packed = []
    for a in a_ab:
        d = jnp.where(blk, a, 0.0)
        acc = d[0:base]
        for q in range(1, n // base):
            acc = acc + d[q * base:(q + 1) * base]
        packed.append(acc)
    cols = jnp.dot(jnp.concatenate(packed, axis=0).astype(BF16), spread,
                   preferred_element_type=F32)
    sub = lax.broadcasted_iota(jnp.int32, (base, LANES), 0)
    lane = lax.broadcasted_iota(jnp.int32, (base, LANES), 1)
    eye_packed = jnp.where(sub == lane % base, 1.0, 0.0)
    x = []
    for i in range(len(a_ab)):
        xp = eye_packed
        for j in range(base - 1):
            lj = cols[i * base:(i + 1) * base, j * LANES:(j + 1) * LANES]
            xp = xp + lj * jnp.broadcast_to(xp[j:j + 1, :], (base, LANES))
        x.append(jnp.where(blk, jnp.concatenate([xp] * (n // base), axis=0), 0.0))
    between()
    s = base
    while s < t:
        level = ((ri // (2 * s)) == (ci // (2 * s))) & ((ri // s) % 2 == 1) & ((ci // s) % 2 == 0)
        c = [jnp.where(level, a, 0.0).astype(BF16) for a in a_ab]
        xb = [xi.astype(BF16) for xi in x]
        odd = lambda m: jnp.concatenate([m[q * s:(q + 1) * s] for q in range(1, n // s, 2)], axis=0)
        xo = [odd(xi) for xi in x]
        xc = [jnp.dot(xoi.astype(BF16), cc, preferred_element_type=F32).astype(BF16)
              for xoi, cc in zip(xo, c)]
        xo = [xoi + jnp.dot(xci, xbi, preferred_element_type=F32) for xoi, xci, xbi in zip(xo, xc, xb)]
        x = [jnp.concatenate([xoi[(q // 2) * s:(q // 2 + 1) * s] if q % 2 else xi[q * s:(q + 1) * s]
                              for q in range(n // s)], axis=0) for xi, xoi in zip(x, xo)]
        between()
        s *= 2

    akv = [jnp.dot(a, v.astype(BF16), preferred_element_type=F32) for a, v in zip(a_ak, v_s)]
    tg = [_dot(xi, jnp.concatenate([a, kv], axis=1)) for xi, a, kv in zip(x, a_s, akv)]
    between()
    low = [jnp.concatenate([g, jnp.concatenate([jnp.zeros_like(v), v], axis=1)], axis=0).astype(BF16)
           for g, v in zip(tg, v_s)]
    out1 = [jnp.dot(a, lo, preferred_element_type=F32) for a, lo in zip(a_r, low)]
    out2 = [lax.dot_general(bb, lo, (((0,), (0,)), ((), ())), preferred_element_type=F32)
            for bb, lo in zip(bk, low)]
    between()
    ki = lax.broadcasted_iota(jnp.int32, (LANES, LANES), 0)
    kj = lax.broadcasted_iota(jnp.int32, (LANES, LANES), 1)
    terms = []
    for rs, o1, o2, we in zip(r_s, out1, out2, w_end):
        r_hat = rs + o1[:, :LANES]
        m = o2[:, :LANES] + jnp.where(ki == kj, we, 0.0)
        terms.append((r_hat, o1[:, LANES:], m, o2[:, LANES:]))
    return terms


def _wkv_kernel(r_ref, k_ref, v_ref, ld_ref, kn_ref, b_ref, spread_ref, y_ref, s_ref, mr_ref, nc_ref,
                yv_ref, *, rows, groups):
    t = WKV_CHUNK
    npair = RWKV_WIDTH // LANES

    @pl.when(pl.program_id(1) == 0)
    def _():
        s_ref[...] = jnp.zeros_like(s_ref)

    def chain_step(c):
        rs = pl.ds(_aligned(c * STACK, STACK), STACK)
        outs = [jnp.dot(mr_ref[p, pl.ds(_aligned(c * 2 * STACK, 2 * STACK), 2 * STACK), :],
                        s_ref[p].astype(BF16), preferred_element_type=F32) for p in range(npair)]
        for p, out in enumerate(outs):
            s_ref[p] = out[:LANES] + nc_ref[p, rs, :]
            ys = out[LANES:] + yv_ref[p, rs, :]
            y_ref[pl.ds(_aligned(c * t, t), t), p * LANES:(p + 1) * LANES] = ys[:t] + ys[t:]

    def precompute(chunks, side):
        chains, where = [], []
        for c in chunks:
            sl = pl.ds(_aligned(c * t, t), t)
            tiles = [ref[sl, :] for ref in (r_ref, k_ref, v_ref, ld_ref, kn_ref, b_ref)]
            cl = _cumsum_rows(tiles[3])
            for p in range(npair):
                ls = slice(p * LANES, (p + 1) * LANES)
                chains.append(tuple(a[:, ls] for a in tiles) + (cl[:, ls],))
                where.append((c, p))
        for (c, p), (r_hat, y_v, m, n_c) in zip(where, _wkv_terms(chains, spread_ref[...], side)):
            mr_ref[p, pl.ds(_aligned(c * 2 * STACK, 2 * STACK), 2 * STACK), :] = (
                jnp.concatenate([m, r_hat], axis=0).astype(BF16))
            rs = pl.ds(_aligned(c * STACK, STACK), STACK)
            nc_ref[p, rs, :] = n_c
            yv_ref[p, rs, :] = y_v

    assert sum(groups) * t == rows, (groups, rows)
    previous, start = (), 0
    for count in groups:
        chunks = tuple(range(start, start + count))
        precompute(chunks, [functools.partial(chain_step, c) for c in previous])
        previous, start = chunks, start + count
    for c in previous:
        chain_step(c)


WKV_MAX_GROUP = 8
WKV_LAST_GROUP = 2


def _wkv_groups(nchunk):
    if nchunk <= WKV_LAST_GROUP:
        return (nchunk,)
    sizes, left = [], nchunk - WKV_LAST_GROUP
    while left > 0:
        sizes.append(min(WKV_MAX_GROUP, left))
        left -= sizes[-1]
    return tuple(sizes) + (WKV_LAST_GROUP,)


def _wkv(r, k, v, ld, kn, b, *, bsz, seq, rows):
    n = bsz * seq
    nt = seq // rows
    npair = RWKV_WIDTH // LANES
    nchunk = rows // WKV_CHUNK
    spec = pl.BlockSpec((rows, RWKV_WIDTH), lambda bi, ti: (bi * nt + ti, 0))
    return pl.pallas_call(
        functools.partial(_wkv_kernel, rows=rows, groups=_wkv_groups(nchunk)),
        grid=(bsz, nt),
        in_specs=[spec] * 6 + [_const_spec((LANES, (SUBLANES - 1) * LANES))],
        out_specs=spec,
        out_shape=jax.ShapeDtypeStruct((n, RWKV_WIDTH), F32),
        scratch_shapes=[pltpu.VMEM((npair, LANES, LANES), F32),
                        pltpu.VMEM((npair, nchunk * 2 * STACK, LANES), BF16),
                        pltpu.VMEM((npair, nchunk * STACK, LANES), F32),
                        pltpu.VMEM((npair, nchunk * STACK, LANES), F32)],
        compiler_params=_params("arbitrary", "arbitrary"),
        name="wkv",
    )(r, k, v, ld, kn, b, _spread_matrix())


def _gelu_tanh(x):
    return 0.5 * x * (1.0 + jnp.tanh(math.sqrt(2.0 / math.pi) * (x + 0.044715 * x * x * x)))


def _s5_kernel(u_ref, wre_ref, wim_ref, cc_ref, ar_ref, ai_ref, d_ref, y_ref,
               xre, xim, st_re, st_im, *, steps):
    @pl.when(pl.program_id(1) == 0)
    def _():
        st_re[...] = jnp.zeros_like(st_re)
        st_im[...] = jnp.zeros_like(st_im)

    rows = steps * SUBLANES
    ns = S5_SLAB_STATE
    half = S5_GROUPS * S5_STATE // 2
    inputs = {}

    def u_of(g):
        if g not in inputs:
            u = jnp.swapaxes(u_ref[g * SUBLANES:(g + 1) * SUBLANES, :, :], 0, 1).reshape(rows, S5_WIDTH)
            inputs[g] = (u, u.astype(BF16))
        return inputs[g]

    def in_proj_jobs(g):
        def job(j, ref, w_ref):
            ref[g % 2, :, j * ns:(j + 1) * ns] = jnp.dot(u_of(g)[1][:, j * LANES:(j + 1) * LANES], w_ref[j],
                                                         preferred_element_type=F32)
        return [functools.partial(job, j, ref, w_ref)
                for j in range(S5_SLABS) for ref, w_ref in ((xre, wre_ref), (xim, wim_ref))]

    def out_proj_jobs(g):
        def job(j):
            ls = slice(j * LANES, (j + 1) * LANES)
            yj = jnp.dot(xre[g % 2, :, j * ns:(j + 1) * ns].astype(BF16), cc_ref[j, 0:ns, :],
                         preferred_element_type=F32)
            yj = yj + jnp.dot(xim[g % 2, :, j * ns:(j + 1) * ns].astype(BF16), cc_ref[j, ns:2 * ns, :],
                              preferred_element_type=F32)
            yj = yj + d_ref[:, ls] * u_of(g)[0][:, ls]
            y_ref[g * SUBLANES:(g + 1) * SUBLANES, :, ls] = jnp.swapaxes(
                _gelu_tanh(yj).reshape(steps, SUBLANES, LANES), 0, 1).astype(y_ref.dtype)
        return [functools.partial(job, j) for j in range(S5_SLABS)]

    def scan(g, jobs):
        total = 2 * steps
        done = 0
        for c in range(2):
            cs = slice(c * half, (c + 1) * half)
            ar = jnp.broadcast_to(ar_ref[:, cs], (SUBLANES, half))
            ai = jnp.broadcast_to(ai_ref[:, cs], (SUBLANES, half))
            x_re, x_im = st_re[g, :, cs], st_im[g, :, cs]
            for i in range(steps):
                k = c * steps + i
                for _ in range((k + 1) * len(jobs) // total - done):
                    jobs[done]()
                    done += 1
                rs = slice(i * SUBLANES, (i + 1) * SUBLANES)
                n_re = ar * x_re - ai * x_im + xre[g % 2, rs, cs]
                n_im = ar * x_im + ai * x_re + xim[g % 2, rs, cs]
                xre[g % 2, rs, cs] = n_re
                xim[g % 2, rs, cs] = n_im
                x_re, x_im = n_re, n_im
            st_re[g, :, cs] = x_re
            st_im[g, :, cs] = x_im

    for job in in_proj_jobs(0):
        job()
    for g in range(S5_TILES):
        jobs = (out_proj_jobs(g - 1) if g >= 1 else []) + (in_proj_jobs(g + 1) if g + 1 < S5_TILES else [])
        scan(g, jobs)
    for job in out_proj_jobs(S5_TILES - 1):
        job()


def _s5(u3, wre, wim, cc, ar, ai, d, *, bsz, seq, steps):
    nstate = S5_GROUPS * S5_STATE
    rows = steps * SUBLANES
    nb = S5_TILES * SUBLANES
    spec = pl.BlockSpec((nb, steps, S5_WIDTH), lambda g, t: (g, t, 0))
    return pl.pallas_call(
        functools.partial(_s5_kernel, steps=steps),
        grid=(bsz // nb, seq // steps),
        in_specs=[spec,
                  _const_spec((S5_SLABS, LANES, S5_SLAB_STATE)),
                  _const_spec((S5_SLABS, LANES, S5_SLAB_STATE)),
                  _const_spec((S5_SLABS, 2 * S5_SLAB_STATE, LANES)),
                  _const_spec((1, nstate)), _const_spec((1, nstate)),
                  _const_spec((1, S5_WIDTH))],
        out_specs=spec,
        out_shape=jax.ShapeDtypeStruct((bsz, seq, S5_WIDTH), BF16),
        scratch_shapes=[pltpu.VMEM((2, rows, nstate), F32), pltpu.VMEM((2, rows, nstate), F32),
                        pltpu.VMEM((S5_TILES, SUBLANES, nstate), F32),
                        pltpu.VMEM((S5_TILES, SUBLANES, nstate), F32)],
        compiler_params=_params("arbitrary", "arbitrary"),
        name="s5",
    )(u3, wre, wim, cc, ar, ai, d)


def _merge_kernel(y_ref, gd_ref, bonus_ref, ys5_ref, gates_ref, x_ref, mod_ref, lng_ref, lnb_ref,
                  gmean_ref, gup_ref, wrw_ref, wglu_ref, wout_ref, o_ref):
    gm = gmean_ref[...]
    y = y_ref[...]
    mean = _head_sum(y, gm, split=True)
    d = y - mean
    var = _head_sum(d * d, gm)
    yn = d * lax.rsqrt(var + GN_EPS) * lng_ref[...] + lnb_ref[...]
    g = _dot(jax.nn.sigmoid(gd_ref[...]), gup_ref[...])
    ya = _dot((yn + bonus_ref[...]) * g, wrw_ref[...])
    z = _dot(ys5_ref[...], wglu_ref[...])
    yb = z[:, :D_MODEL] * jax.nn.sigmoid(z[:, D_MODEL:])
    gates = gates_ref[...]
    mixed = _dot(gates[:, :D_MODEL] * ya + gates[:, D_MODEL:] * yb, wout_ref[...])
    gt1 = mod_ref[...][2:3]
    o_ref[...] = x_ref[...] + gt1 * mixed


def _merge(y, gd, bonus, ys5, gates, x2, mod3, lng, lnb, gmean, gup, wrw, wglu, wout, *, bsz, seq, rows):
    n = bsz * seq
    nt = seq // rows
    w = RWKV_WIDTH
    row_spec = lambda cols: pl.BlockSpec((rows, cols), lambda i: (i, 0))
    return pl.pallas_call(
        _merge_kernel,
        grid=(n // rows,),
        in_specs=[row_spec(w), row_spec(LORA_G), row_spec(w), row_spec(w), row_spec(2 * D_MODEL),
                  row_spec(D_MODEL),
                  pl.BlockSpec((None, 6, D_MODEL), lambda i: (i // nt, 0, 0)),
                  _const_spec((1, w)), _const_spec((1, w)), _const_spec((MXU_TILE, MXU_TILE)),
                  _const_spec((LORA_G, w)), _const_spec((w, D_MODEL)), _const_spec((S5_WIDTH, 2 * D_MODEL)),
                  _const_spec((D_MODEL, D_MODEL))],
        out_specs=row_spec(D_MODEL),
        out_shape=jax.ShapeDtypeStruct((n, D_MODEL), F32),
        compiler_params=_params("arbitrary"),
        name="merge",
    )(y, gd, bonus, ys5, gates, x2, mod3, lng, lnb, gmean, gup, wrw, wglu, wout)


def _ffn_kernel(x_ref, mod_ref, g2_ref, wup_ref, cw_ref, cb_ref, wdn_ref, gf_ref, o_ref, buf, *, rows):
    pad = SUBLANES

    @pl.when(pl.program_id(1) == 0)
    def _():
        buf[:, 0:pad, :] = jnp.zeros((FFN_SPLIT, pad, 2 * FFN_CH), F32)

    mod = mod_ref[...]
    sh2, sc2, gt2 = mod[3:4], mod[4:5], mod[5:6]

    def normed(xh):
        ms = jnp.mean(xh * xh, axis=-1, keepdims=True)
        h = xh * lax.rsqrt(ms + RMS_EPS) * g2_ref[...]
        return (h * (1.0 + sc2) + sh2).astype(BF16)

    halves = 2
    hr = rows // halves
    x = [x_ref[h * hr:(h + 1) * hr, :] for h in range(halves)]
    hb = [normed(x[0]), None]
    acc = [jnp.zeros((hr, D_MODEL), F32) for _ in range(halves)]
    for c in range(FFN_SPLIT):
        for h in range(halves):
            if hb[h] is None:
                hb[h] = normed(x[h])
            r0 = pad + h * hr
            buf[c, r0:r0 + hr, :] = jnp.dot(hb[h], wup_ref[c], preferred_element_type=F32)
        cw = cw_ref[c]
        for h in range(halves):
            hid = cb_ref[c]
            for j in range(CONV_W):
                off = pad + h * hr - (CONV_W - 1) + j
                hid = hid + cw[j:j + 1, :] * buf[c, off:off + hr, :]
            gate, up = hid[:, :FFN_CH], hid[:, FFN_CH:]
            act = (0.5 * gate) * (1.0 + jnp.tanh(0.5 * gate)) * up
            acc[h] = acc[h] + jnp.dot(act.astype(BF16), wdn_ref[c], preferred_element_type=F32)
        buf[c, pad - (CONV_W - 1):pad, :] = buf[c, pad + rows - (CONV_W - 1):pad + rows, :]

    for h in range(halves):
        x2 = x[h] + gt2 * acc[h]
        ms2 = jnp.mean(x2 * x2, axis=-1, keepdims=True)
        o_ref[h * hr:(h + 1) * hr, :] = x2 * lax.rsqrt(ms2 + RMS_EPS) * gf_ref[...]


def _ffn(x1, mod3, g2, wup, cw, cb, wdn, gf, *, bsz, seq, rows):
    n = bsz * seq
    nt = seq // rows
    row_spec = pl.BlockSpec((rows, D_MODEL), lambda b, t: (b * nt + t, 0))
    return pl.pallas_call(
        functools.partial(_ffn_kernel, rows=rows),
        grid=(bsz, nt),
        in_specs=[row_spec,
                  pl.BlockSpec((None, 6, D_MODEL), lambda b, t: (b, 0, 0)),
                  _const_spec((1, D_MODEL)),
                  _const_spec((FFN_SPLIT, D_MODEL, 2 * FFN_CH)),
                  _const_spec((FFN_SPLIT, CONV_W, 2 * FFN_CH)),
                  _const_spec((FFN_SPLIT, 1, 2 * FFN_CH)),
                  _const_spec((FFN_SPLIT, FFN_CH, D_MODEL)),
                  _const_spec((1, D_MODEL))],
        out_specs=row_spec,
        out_shape=jax.ShapeDtypeStruct((n, D_MODEL), F32),
        scratch_shapes=[pltpu.VMEM((FFN_SPLIT, rows + SUBLANES, 2 * FFN_CH), F32)],
        compiler_params=_params("arbitrary", "arbitrary"),
        name="ffn",
    )(x1, mod3, g2, wup, cw, cb, wdn, gf)


def _split_cols(a):
    parts = [jnp.concatenate([a[..., c * FFN_CH:(c + 1) * FFN_CH],
                              a[..., D_FF + c * FFN_CH:D_FF + (c + 1) * FFN_CH]], axis=-1)
             for c in range(FFN_SPLIT)]
    return jnp.stack(parts, axis=0)


def _block_diag_ones(n, blk, value, dtype):
    i = jnp.arange(n) // blk
    return jnp.where(i[:, None] == i[None, :], value, 0.0).astype(dtype)


def kernel(x, c, w_ada, b_ada, norm1_g, w_in, mu_shift, rwkv_w0, rwkv_w_up, rwkv_a0, rwkv_a_up,
           rwkv_g_up, rwkv_k_k, rwkv_k_a, rwkv_r_k, rwkv_ln_g, rwkv_ln_b, w_out_rwkv, s5_a_re,
           s5_a_im, s5_log_dt, s5_b_re, s5_b_im, s5_c_re, s5_c_im, s5_d, w_glu, w_out, norm2_g,
           w_ffn_up, ffn_conv_w, ffn_conv_b, w_ffn_down, norm_f_g):
    bsz, seq, _ = x.shape
    depth = w_ada.shape[0]
    n = bsz * seq
    w = RWKV_WIDTH
    rows_a = min(256, seq)
    rows_m = min(512, seq)
    rows_f = min(512, seq)
    s5_steps = min(64, seq)

    g64 = _block_diag_ones(MXU_TILE, RWKV_HEAD, 1.0, BF16)
    gmean = _block_diag_ones(MXU_TILE, RWKV_HEAD, 1.0 / RWKV_HEAD, BF16)
    eye8 = jnp.eye(LANES // S5_GROUP, dtype=F32)
    row = lambda a: a.reshape(1, -1)

    assert depth == 1, depth
    x2 = x.reshape(n, D_MODEL)
    for l in range(depth):
        mod3 = _adaln(c, w_ada[l], b_ada[l]).reshape(bsz, 6, D_MODEL)

        zero = jnp.zeros((LORA_W, w), F32)
        waup = jnp.concatenate([jnp.concatenate([rwkv_w_up[l], zero], axis=1),
                                jnp.concatenate([zero, rwkv_a_up[l]], axis=1)], axis=0).astype(BF16)
        r, k, v, ld, kn, b, gd, bonus, u, gates = _inproj(
            x2, mod3, row(norm1_g[l]), w_in[l].astype(BF16), row(mu_shift[l]), row(rwkv_w0[l]), waup,
            row(rwkv_a0[l]), row(rwkv_k_k[l]), row(rwkv_k_a[l]),
            row(rwkv_r_k[l]), g64, bsz=bsz, seq=seq, rows=rows_a)

        y = _wkv(r, k, v, ld, kn, b, bsz=bsz, seq=seq, rows=min(1024, seq))

        ab_re, ab_im, bb_re, bb_im = _s5prep(s5_a_re[l], s5_a_im[l], s5_log_dt[l], s5_b_re[l], s5_b_im[l])
        gl = LANES // S5_GROUP
        slab_in = lambda bb: jnp.einsum('jgcp,gh->jgchp', bb.reshape(S5_SLABS, gl, S5_GROUP, S5_STATE),
                                        eye8).reshape(S5_SLABS, LANES, S5_SLAB_STATE).astype(BF16)
        slab_out = lambda cc: jnp.einsum('jgcp,gh->jgphc', cc.reshape(S5_SLABS, gl, S5_GROUP, S5_STATE),
                                         eye8).reshape(S5_SLABS, S5_SLAB_STATE, LANES)
        cc = jnp.concatenate([slab_out(s5_c_re[l]), -slab_out(s5_c_im[l])], axis=1).astype(BF16)
        ys5 = _s5(u.reshape(bsz, seq, S5_WIDTH), slab_in(bb_re), slab_in(bb_im), cc, row(ab_re),
                  row(ab_im), row(s5_d[l]), bsz=bsz, seq=seq, steps=s5_steps).reshape(n, S5_WIDTH)

        x1 = _merge(y, gd, bonus, ys5, gates, x2, mod3, row(rwkv_ln_g[l]), row(rwkv_ln_b[l]), gmean,
                    rwkv_g_up[l].astype(BF16), w_out_rwkv[l].astype(BF16), w_glu[l].astype(BF16), w_out[l].astype(BF16),
                    bsz=bsz, seq=seq, rows=rows_m)

        gf = row(norm_f_g)
        wdn = w_ffn_down[l].astype(BF16).reshape(FFN_SPLIT, FFN_CH, D_MODEL)
        x2 = _ffn(x1, mod3, row(norm2_g[l]), _split_cols(w_ffn_up[l]).astype(BF16),
                  _split_cols(ffn_conv_w[l]), _split_cols(ffn_conv_b[l].reshape(1, -1)), wdn,
                  gf, bsz=bsz, seq=seq, rows=rows_f)
    return x2.reshape(bsz, seq, D_MODEL)
```
